```python
import math
import jax, jax.numpy as jnp
from jax import lax
import numpy as np


D_MODEL = 1024
BATCH = 4
SEQ = 4096
DEPTH = 2
DEC_BATCH = 16
DEC_SEQ = 16
PAST_LEN = 4096

CHUNK = 64
N_MIXERS = 2
N_ATTN_LAYERS = (DEPTH + 1) // 2
N_SSM_LAYERS = DEPTH // 2
N_HEADS = 16
HEAD_DIM = D_MODEL // N_HEADS
N_KV_HEADS = 4
GQ = N_HEADS // N_KV_HEADS
IDX_HEADS = 8
IDX_DIM = 64
TOPK_MAX = 256
QBLOCK = 128
IDX_SCALE = (IDX_HEADS * IDX_DIM) ** -0.5
ATTN_SCALE = HEAD_DIM ** -0.5
ATTN_IN_COLS = N_HEADS * HEAD_DIM + 2 * N_KV_HEADS * HEAD_DIM + IDX_HEADS * IDX_DIM + IDX_DIM + IDX_HEADS
NUM_BUCKETS = 32
MAX_DISTANCE = 128
EXPAND = 2
D_INNER = EXPAND * D_MODEL
SSM_HEAD_DIM = 64
SSM_HEADS = D_INNER // SSM_HEAD_DIM
SSM_GROUPS = 8
HEADS_PER_GROUP = SSM_HEADS // SSM_GROUPS
D_STATE = 128
D_CONV = 4
CONV_DIM = D_INNER + 2 * SSM_GROUPS * D_STATE
SSM_IN_COLS = D_INNER + CONV_DIM + SSM_HEADS
D_FF = 4 * D_MODEL
EPS = 1e-6

kernel_name = 'dsa_mamba2_streaming_hybrid_step'


def rmsnorm(x, g):
    xf = x.astype(jnp.float32)
    y = xf * lax.rsqrt(jnp.mean(xf * xf, axis=-1, keepdims=True) + EPS)
    return (y * g.astype(jnp.float32)).astype(x.dtype)


def rel_bucket(rel):
    half = NUM_BUCKETS // 2
    max_exact = half // 2
    base = jnp.where(rel > 0, half, 0)
    n = jnp.abs(rel)
    nf = jnp.maximum(n, 1).astype(jnp.float32)
    large = max_exact + (jnp.log(nf / max_exact) / math.log(MAX_DISTANCE / max_exact)
                         * (half - max_exact)).astype(jnp.int32)
    large = jnp.minimum(large, half - 1)
    return base + jnp.where(n < max_exact, n, large)


def dsa_mixer(h, past_k, past_v, past_ik, w_in, w_o, rel_bias):
    b, t, _ = h.shape
    p_len = past_k.shape[1]
    n_keys = p_len + t
    top = min(TOPK_MAX, n_keys // 4)
    o1 = N_HEADS * HEAD_DIM
    o2 = o1 + N_KV_HEADS * HEAD_DIM
    o3 = o2 + N_KV_HEADS * HEAD_DIM
    o4 = o3 + IDX_HEADS * IDX_DIM
    o5 = o4 + IDX_DIM
    proj = h @ w_in
    q = proj[..., :o1].reshape(b, t, N_KV_HEADS, GQ, HEAD_DIM)
    k = proj[..., o1:o2].reshape(b, t, N_KV_HEADS, HEAD_DIM)
    v = proj[..., o2:o3].reshape(b, t, N_KV_HEADS, HEAD_DIM)
    qi = proj[..., o3:o4].reshape(b, t, IDX_HEADS, IDX_DIM)
    ki = proj[..., o4:o5]
    wi = proj[..., o5:]
    keys = jnp.concatenate([past_k.astype(k.dtype), k], axis=1)
    vals = jnp.concatenate([past_v.astype(v.dtype), v], axis=1)
    keys_i = jnp.concatenate([past_ik.astype(ki.dtype), ki], axis=1).astype(jnp.float32)
    k_chunk = jnp.arange(n_keys) // CHUNK
    q_pos = p_len + jnp.arange(t)
    qb = QBLOCK if t % QBLOCK == 0 else t
    nb = t // qb
    bias_tab = rel_bias.astype(jnp.float32).reshape(NUM_BUCKETS, N_KV_HEADS, GQ)

    def to_blocks(a):
        return jnp.swapaxes(a.reshape((b, nb, qb) + a.shape[2:]), 0, 1)

    def block(args):
        q_b, qi_b, wi_b, pos_b = args
        s = jnp.einsum('bqhd,bsd->bqhs', qi_b.astype(jnp.float32), keys_i)
        score = jnp.einsum('bqhs,bqh->bqs', jax.nn.relu(s), wi_b.astype(jnp.float32)) * IDX_SCALE
        visible = k_chunk[None, :] <= (pos_b // CHUNK)[:, None]
        score = jnp.where(visible[None], score, -jnp.inf)
        top_s, top_i = lax.top_k(score, top)
        valid = top_s > -jnp.inf
        k_sel = jax.vmap(lambda kk, ii: kk[ii])(keys, top_i)
        v_sel = jax.vmap(lambda vv, ii: vv[ii])(vals, top_i)
        bias = bias_tab[rel_bucket(top_i - pos_b[None, :, None])]
        logits = (jnp.einsum('bqgjd,bqkgd->bqgjk', q_b, k_sel).astype(jnp.float32) * ATTN_SCALE
                  + jnp.moveaxis(bias, 2, -1))
        logits = jnp.where(valid[:, :, None, None, :], logits, -jnp.inf)
        probs = jax.nn.softmax(logits, axis=-1).astype(v_sel.dtype)
        return jnp.einsum('bqgjk,bqkgd->bqgjd', probs, v_sel)

    out = lax.map(block, (to_blocks(q), to_blocks(qi), to_blocks(wi), q_pos.reshape(nb, qb)))
    out = jnp.swapaxes(out, 0, 1).reshape(b, t, N_HEADS * HEAD_DIM)
    return out @ w_o, k, v, ki


def ssd_scan(x, dt, A, Bm, Cm, init_state, blk):
    b, l, g, j, p = x.shape
    n = Bm.shape[-1]
    c = l // blk
    xd = (x * dt[..., None]).reshape(b, c, blk, g, j, p)
    a = (dt * A).reshape(b, c, blk, g, j)
    Bc = Bm.reshape(b, c, blk, g, n)
    Cc = Cm.reshape(b, c, blk, g, n)
    a_cs = jnp.cumsum(a, axis=2)
    seg = a_cs[:, :, :, None] - a_cs[:, :, None, :]
    causal = jnp.tril(jnp.ones((blk, blk), dtype=bool))[None, None, :, :, None, None]
    decay = jnp.where(causal, jnp.exp(jnp.where(causal, seg, 0.0)), 0.0)
    cb = jnp.einsum('bclgn,bcsgn->bclsg', Cc, Bc)
    y_diag = jnp.einsum('bclsgj,bcsgjp->bclgjp', cb[..., None] * decay, xd)
    decay_to_end = jnp.exp(a_cs[:, :, -1:] - a_cs)
    chunk_states = jnp.einsum('bclgn,bclgj,bclgjp->bcgjpn', Bc, decay_to_end, xd)
    chunk_decay = jnp.exp(a_cs[:, :, -1])

    def step(s, inp):
        st, dec = inp
        return s * dec[..., None, None] + st, s

    final, prev = lax.scan(step, init_state,
                           (jnp.moveaxis(chunk_states, 1, 0), jnp.moveaxis(chunk_decay, 1, 0)))
    prev = jnp.moveaxis(prev, 0, 1)
    y_off = jnp.einsum('bclgn,bcgjpn,bclgj->bclgjp', Cc, prev, jnp.exp(a_cs))
    return (y_diag + y_off).reshape(b, l, g, j, p), final


def ssd_mixer(h, conv_state, ssm_state, w_in, conv_w, conv_b, dt_bias, a_log, d_skip, norm_w, w_out):
    b, t, _ = h.shape
    proj = h @ w_in
    z = proj[..., :D_INNER]
    xbc = proj[..., D_INNER:D_INNER + CONV_DIM]
    dt_raw = proj[..., D_INNER + CONV_DIM:]
    padded = jnp.concatenate([conv_state.astype(xbc.dtype), xbc], axis=1)
    conv = conv_b
    for tap in range(D_CONV):
        conv = conv + padded[:, tap:tap + t] * conv_w[tap]
    xbc_act = jax.nn.silu(conv)
    xs = xbc_act[..., :D_INNER].reshape(b, t, SSM_GROUPS, HEADS_PER_GROUP, SSM_HEAD_DIM).astype(jnp.float32)
    Bm = xbc_act[..., D_INNER:D_INNER + SSM_GROUPS * D_STATE].reshape(b, t, SSM_GROUPS, D_STATE).astype(jnp.float32)
    Cm = xbc_act[..., D_INNER + SSM_GROUPS * D_STATE:].reshape(b, t, SSM_GROUPS, D_STATE).astype(jnp.float32)
    dt = jax.nn.softplus(dt_raw.astype(jnp.float32) + dt_bias.astype(jnp.float32))
    dt = dt.reshape(b, t, SSM_GROUPS, HEADS_PER_GROUP)
    A = -jnp.exp(a_log.astype(jnp.float32)).reshape(SSM_GROUPS, HEADS_PER_GROUP)
    init = ssm_state.astype(jnp.float32).reshape(b, SSM_GROUPS, HEADS_PER_GROUP, SSM_HEAD_DIM, D_STATE)
    blk = CHUNK if t % CHUNK == 0 else t
    y, final = ssd_scan(xs, dt, A, Bm, Cm, init, blk)
    y = y + xs * d_skip.astype(jnp.float32).reshape(SSM_GROUPS, HEADS_PER_GROUP)[..., None]
    yg = (y.reshape(b, t, D_INNER) * jax.nn.silu(z.astype(jnp.float32)))
    yg = yg.reshape(b, t, SSM_GROUPS, D_INNER // SSM_GROUPS)
    yg = yg * lax.rsqrt(jnp.mean(yg * yg, axis=-1, keepdims=True) + EPS)
    y_out = yg.reshape(b, t, D_INNER) * norm_w.astype(jnp.float32)
    new_conv = padded[:, padded.shape[1] - (D_CONV - 1):]
    new_ssm = final.reshape(b, SSM_HEADS, SSM_HEAD_DIM, D_STATE).astype(ssm_state.dtype)
    return y_out.astype(h.dtype) @ w_out, new_conv, new_ssm


def sqrelu_mlp(h, w_up, w_down):
    a = jax.nn.relu(h @ w_up)
    return (a * a) @ w_down


def trunk(x, past_k, past_v, past_ik, conv_st, ssm_st, rel_bias, norm_mix, norm_ffn, norm_final,
          attn_w_in, attn_w_o, ssm_w_in, ssm_conv_w, ssm_conv_b, ssm_dt_bias, ssm_a_log, ssm_d,
          ssm_norm, ssm_w_out, mlp_w_up, mlp_w_down):
    new_k, new_v, new_ik, new_conv, new_ssm = [], [], [], [], []
    for i in range(DEPTH):
        h = rmsnorm(x, norm_mix[i])
        j = i // N_MIXERS
        if i % N_MIXERS == 0:
            out, kr, vr, ir = dsa_mixer(h, past_k[j], past_v[j], past_ik[j],
                                        attn_w_in[j], attn_w_o[j], rel_bias)
            new_k.append(kr)
            new_v.append(vr)
            new_ik.append(ir)
        else:
            out, cr, sr = ssd_mixer(h, conv_st[j], ssm_st[j], ssm_w_in[j], ssm_conv_w[j], ssm_conv_b[j],
                                    ssm_dt_bias[j], ssm_a_log[j], ssm_d[j], ssm_norm[j], ssm_w_out[j])
            new_conv.append(cr)
            new_ssm.append(sr)
        x = x + out.astype(x.dtype)
        x = x + sqrelu_mlp(rmsnorm(x, norm_ffn[i]), mlp_w_up[i], mlp_w_down[i]).astype(x.dtype)
    return (rmsnorm(x, norm_final), jnp.stack(new_k), jnp.stack(new_v), jnp.stack(new_ik),
            jnp.stack(new_conv), jnp.stack(new_ssm))


def setup_inputs(seed: int = 0) -> dict:
    key = jax.random.key(seed)
    ks = jax.random.split(key, 26)

    def nrm(k, shape, scale):
        return jax.random.normal(k, shape, jnp.float32) * scale

    u = jax.random.uniform(ks[16], (N_SSM_LAYERS, SSM_HEADS), jnp.float32)
    dt0 = jnp.exp(u * (math.log(0.1) - math.log(0.001)) + math.log(0.001))
    return {
        'x_prompt': nrm(ks[0], (BATCH, SEQ, D_MODEL), 1.0),
        'x_sample': nrm(ks[1], (DEC_BATCH, DEC_SEQ, D_MODEL), 1.0),
        'cache_k': nrm(ks[2], (N_ATTN_LAYERS, DEC_BATCH, PAST_LEN, N_KV_HEADS, HEAD_DIM), 1.0),
        'cache_v': nrm(ks[3], (N_ATTN_LAYERS, DEC_BATCH, PAST_LEN, N_KV_HEADS, HEAD_DIM), 1.0),
        'cache_idx_k': nrm(ks[4], (N_ATTN_LAYERS, DEC_BATCH, PAST_LEN, IDX_DIM), 1.0),
        'state_conv': nrm(ks[5], (N_SSM_LAYERS, DEC_BATCH, D_CONV - 1, CONV_DIM), 1.0),
        'state_ssm': nrm(ks[6], (N_SSM_LAYERS, DEC_BATCH, SSM_HEADS, SSM_HEAD_DIM, D_STATE), 0.1),
        'rel_bias': nrm(ks[7], (NUM_BUCKETS, N_HEADS), 0.5),
        'norm_mix': 1.0 + nrm(ks[8], (DEPTH, D_MODEL), 0.02),
        'norm_ffn': 1.0 + nrm(ks[9], (DEPTH, D_MODEL), 0.02),
        'norm_final': 1.0 + nrm(ks[10], (D_MODEL,), 0.02),
        'attn_w_in': nrm(ks[11], (N_ATTN_LAYERS, D_MODEL, ATTN_IN_COLS), D_MODEL ** -0.5),
        'attn_w_o': nrm(ks[12], (N_ATTN_LAYERS, N_HEADS * HEAD_DIM, D_MODEL), (N_HEADS * HEAD_DIM) ** -0.5),
        'ssm_w_in': nrm(ks[13], (N_SSM_LAYERS, D_MODEL, SSM_IN_COLS), D_MODEL ** -0.5),
        'ssm_conv_w': nrm(ks[14], (N_SSM_LAYERS, D_CONV, CONV_DIM), D_CONV ** -0.5),
        'ssm_conv_b': nrm(ks[15], (N_SSM_LAYERS, CONV_DIM), 0.02),
        'ssm_dt_bias': dt0 + jnp.log(-jnp.expm1(-dt0)),
        'ssm_a_log': jnp.log(jax.random.uniform(ks[17], (N_SSM_LAYERS, SSM_HEADS), jnp.float32, 1.0, 16.0)),
        'ssm_d': 1.0 + nrm(ks[18], (N_SSM_LAYERS, SSM_HEADS), 0.1),
        'ssm_norm': 1.0 + nrm(ks[19], (N_SSM_LAYERS, D_INNER), 0.02),
        'ssm_w_out': nrm(ks[20], (N_SSM_LAYERS, D_INNER, D_MODEL), D_INNER ** -0.5),
        'mlp_w_up': nrm(ks[21], (DEPTH, D_MODEL, D_FF), D_MODEL ** -0.5),
        'mlp_w_down': nrm(ks[22], (DEPTH, D_FF, D_MODEL), D_FF ** -0.5),
    }


def reference(x_prompt, x_sample, cache_k, cache_v, cache_idx_k, state_conv, state_ssm, rel_bias,
              norm_mix, norm_ffn, norm_final, attn_w_in, attn_w_o, ssm_w_in, ssm_conv_w, ssm_conv_b,
              ssm_dt_bias, ssm_a_log, ssm_d, ssm_norm, ssm_w_out, mlp_w_up, mlp_w_down):
    bp = x_prompt.shape[0]
    dtp = x_prompt.dtype
    empty_k = jnp.zeros((N_ATTN_LAYERS, bp, 0, N_KV_HEADS, HEAD_DIM), dtp)
    empty_ik = jnp.zeros((N_ATTN_LAYERS, bp, 0, IDX_DIM), dtp)
    zero_conv = jnp.zeros((N_SSM_LAYERS, bp, D_CONV - 1, CONV_DIM), dtp)
    zero_ssm = jnp.zeros((N_SSM_LAYERS, bp, SSM_HEADS, SSM_HEAD_DIM, D_STATE), dtp)
    y_prompt, k_prompt, v_prompt, idx_k_prompt, conv_prompt, ssm_prompt = trunk(
        x_prompt, empty_k, empty_k, empty_ik, zero_conv, zero_ssm, rel_bias, norm_mix, norm_ffn,
        norm_final, attn_w_in, attn_w_o, ssm_w_in, ssm_conv_w, ssm_conv_b, ssm_dt_bias, ssm_a_log,
        ssm_d, ssm_norm, ssm_w_out, mlp_w_up, mlp_w_down)
    y_sample, k_sample, v_sample, idx_k_sample, conv_sample, ssm_sample = trunk(
        x_sample, cache_k, cache_v, cache_idx_k, state_conv, state_ssm, rel_bias, norm_mix, norm_ffn,
        norm_final, attn_w_in, attn_w_o, ssm_w_in, ssm_conv_w, ssm_conv_b, ssm_dt_bias, ssm_a_log,
        ssm_d, ssm_norm, ssm_w_out, mlp_w_up, mlp_w_down)
    return (y_prompt, y_sample, k_prompt, v_prompt, idx_k_prompt, conv_prompt, ssm_prompt,
            k_sample, v_sample, idx_k_sample, conv_sample, ssm_sample)
```

```python
import functools
import math

import jax
import jax.numpy as jnp
from jax import lax
from jax.experimental import pallas as pl
from jax.experimental.pallas import tpu as pltpu

F32 = jnp.float32
BF16 = jnp.bfloat16

D_MODEL = 1024
CHUNK = 64
CHUNK_SHIFT = 6
N_HEADS = 16
HEAD_DIM = 64
N_KV_HEADS = 4
GQ = N_HEADS // N_KV_HEADS
IDX_HEADS = 8
IDX_DIM = 64
TOPK_MAX = 256
IDX_SCALE = (IDX_HEADS * IDX_DIM) ** -0.5
ATTN_SCALE = HEAD_DIM ** -0.5
NUM_BUCKETS = 32
MAX_DISTANCE = 128
D_INNER = 2 * D_MODEL
SSM_HEAD_DIM = 64
SSM_HEADS = D_INNER // SSM_HEAD_DIM
SSM_GROUPS = 8
HEADS_PER_GROUP = SSM_HEADS // SSM_GROUPS
GROUP_W = HEADS_PER_GROUP * SSM_HEAD_DIM
D_STATE = 128
D_CONV = 4
CONV_DIM = D_INNER + 2 * SSM_GROUPS * D_STATE
D_FF = 4 * D_MODEL
EPS = 1e-6

Q_OFF = 0
K_OFF = N_HEADS * HEAD_DIM
V_OFF = K_OFF + N_KV_HEADS * HEAD_DIM
QI_OFF = V_OFF + N_KV_HEADS * HEAD_DIM
KI_OFF = QI_OFF + IDX_HEADS * IDX_DIM
WI_OFF = KI_OFF + IDX_DIM
ATTN_COLS = WI_OFF + IDX_HEADS

SSM_XBC_OFF = 0
SSM_Z_OFF = CONV_DIM
SSM_DT_OFF = CONV_DIM + D_INNER
SSM_COLS = SSM_DT_OFF + SSM_HEADS

LANE = 128
KEY_WINDOW = 512
VMEM_LIMIT = 56 * 1024 * 1024

NEG_BIG = -1e30
KEY_NEG_INF = -2139095041
INT_MIN = -2147483648


def _round_up(n, m):
    return (n + m - 1) // m * m


def _rms(x, g):
    ms = jnp.mean(x * x, axis=-1, keepdims=True)
    return x * lax.rsqrt(ms + EPS) * g


def _sigmoid(x):
    return 1.0 / (1.0 + jnp.exp(-x))


def _params(*sem):
    return pltpu.CompilerParams(dimension_semantics=sem, vmem_limit_bytes=VMEM_LIMIT)


def _norm_matmul_kernel(x_ref, g_ref, w_ref, *rest, emit_bf16):
    if emit_bf16:
        o_ref, ob_ref, xn_ref = rest
    else:
        o_ref, xn_ref = rest

    @pl.when(pl.program_id(1) == 0)
    def _():
        xn_ref[...] = _rms(x_ref[...], g_ref[...]).astype(BF16)

    acc = jnp.dot(xn_ref[...], w_ref[...], preferred_element_type=F32)
    o_ref[...] = acc
    if emit_bf16:
        ob_ref[...] = acc.astype(BF16)


def _norm_matmul(x, g, w, *, tm, tn, emit_bf16):
    m, k = x.shape
    n = w.shape[1]
    out_shape = [jax.ShapeDtypeStruct((m, n), F32)]
    out_specs = [pl.BlockSpec((tm, tn), lambda i, j: (i, j))]
    if emit_bf16:
        out_shape.append(jax.ShapeDtypeStruct((m, n), BF16))
        out_specs.append(pl.BlockSpec((tm, tn), lambda i, j: (i, j)))
    return pl.pallas_call(
        functools.partial(_norm_matmul_kernel, emit_bf16=emit_bf16),
        out_shape=out_shape,
        grid=(m // tm, n // tn),
        in_specs=[pl.BlockSpec((tm, k), lambda i, j: (i, 0)),
                  pl.BlockSpec((1, k), lambda i, j: (0, 0)),
                  pl.BlockSpec((k, tn), lambda i, j: (0, j))],
        out_specs=out_specs,
        scratch_shapes=[pltpu.VMEM((tm, k), BF16)],
        compiler_params=_params("parallel", "arbitrary"),
        name="norm_matmul",
    )(x, g, w)


def _matmul_resid_kernel(a_ref, w_ref, r_ref, o_ref):
    o_ref[...] = r_ref[...] + jnp.dot(a_ref[...], w_ref[...], preferred_element_type=F32)


def _matmul_resid(a, w, resid, *, tm):
    m, k = a.shape
    n = w.shape[1]
    return pl.pallas_call(
        _matmul_resid_kernel,
        out_shape=jax.ShapeDtypeStruct((m, n), F32),
        grid=(m // tm,),
        in_specs=[pl.BlockSpec((tm, k), lambda i: (i, 0)),
                  pl.BlockSpec((k, n), lambda i: (0, 0)),
                  pl.BlockSpec((tm, n), lambda i: (i, 0))],
        out_specs=pl.BlockSpec((tm, n), lambda i: (i, 0)),
        compiler_params=_params("parallel"),
        name="matmul_resid",
    )(a, w, resid)


def _mlp_kernel(x_ref, g_ref, wu_ref, wd_ref, gf_ref, o_ref, xn_ref, acc_ref, *, final_norm):
    f = pl.program_id(1)

    @pl.when(f == 0)
    def _():
        xn_ref[...] = _rms(x_ref[...], g_ref[...]).astype(BF16)
        acc_ref[...] = jnp.zeros_like(acc_ref)

    a = jnp.dot(xn_ref[...], wu_ref[...], preferred_element_type=F32)
    a = jnp.maximum(a, 0.0)
    a = a * a
    acc_ref[...] += jnp.dot(a.astype(BF16), wd_ref[...], preferred_element_type=F32)

    @pl.when(f == pl.num_programs(1) - 1)
    def _():
        y = x_ref[...] + acc_ref[...]
        if final_norm:
            y = _rms(y, gf_ref[...])
        o_ref[...] = y


def _mlp(x, g, wu, wd, gf, *, tm, tf, final_norm):
    m, d = x.shape
    ff = wu.shape[1]
    return pl.pallas_call(
        functools.partial(_mlp_kernel, final_norm=final_norm),
        out_shape=jax.ShapeDtypeStruct((m, d), F32),
        grid=(m // tm, ff // tf),
        in_specs=[pl.BlockSpec((tm, d), lambda i, f: (i, 0)),
                  pl.BlockSpec((1, d), lambda i, f: (0, 0)),
                  pl.BlockSpec((d, tf), lambda i, f: (0, f)),
                  pl.BlockSpec((tf, d), lambda i, f: (f, 0)),
                  pl.BlockSpec((1, d), lambda i, f: (0, 0))],
        out_specs=pl.BlockSpec((tm, d), lambda i, f: (i, 0)),
        scratch_shapes=[pltpu.VMEM((tm, d), BF16), pltpu.VMEM((tm, d), F32)],
        compiler_params=_params("parallel", "arbitrary"),
        name="mlp",
    )(x, g, wu, wd, gf)


def _attn_kernel(q_ref, qi_ref, wi_ref, k_ref, v_ref, ki_ref, nb_ref, o_ref,
                 skey_ref, m_ref, l_ref, acc_ref, *, tq, past, front_pad, topk):
    w = KEY_WINDOW
    i = pl.program_id(1)
    q0 = past + i * tq
    lv = q0 + tq
    nw = jnp.right_shift(lv + (w - 1), 9)
    qchunk = jnp.right_shift(q0 + lax.broadcasted_iota(jnp.int32, (tq, 1), 0), CHUNK_SHIFT)
    col = lax.broadcasted_iota(jnp.int32, (1, w), 1)

    def win_row(j):
        return pl.multiple_of(lv + front_pad - w * (j + 1), tq)

    wi = wi_ref[...]

    def score_body(j, carry):
        kib = ki_ref[pl.ds(win_row(j), w), :]
        sc = jnp.zeros((tq, w), F32)
        for h in range(IDX_HEADS):
            s = lax.dot_general(qi_ref[:, h * IDX_DIM:(h + 1) * IDX_DIM], kib,
                                (((1,), (1,)), ((), ())), preferred_element_type=F32)
            sc = sc + jnp.maximum(s, 0.0) * wi[:, h:h + 1]
        sc = sc * IDX_SCALE
        kpos = (lv - w * (j + 1)) + col
        vis = (kpos >= 0) & (jnp.right_shift(kpos, CHUNK_SHIFT) <= qchunk)
        bits = pltpu.bitcast(sc, jnp.int32)
        key = bits ^ (jnp.right_shift(bits, 31) & 0x7FFFFFFF)
        key = jnp.where(sc == 0.0, 0, key)
        skey_ref[j] = jnp.where(vis, key, KEY_NEG_INF)
        return carry

    lax.fori_loop(0, nw, score_body, 0)

    def fold(c):
        return (c[:, 0:LANE] + c[:, LANE:2 * LANE]) + (c[:, 2 * LANE:3 * LANE] + c[:, 3 * LANE:4 * LANE])

    def count_ge(cand):
        def body(j, acc):
            return acc + fold(jnp.where(skey_ref[j] >= cand, 1.0, 0.0))
        acc = lax.fori_loop(0, nw, body, jnp.zeros((tq, LANE), F32))
        return jnp.sum(acc, axis=1, keepdims=True)

    def search_body(it, prefix):
        cand = prefix + lax.shift_left(jnp.int32(1), 31 - it)
        return jnp.where(count_ge(cand) >= float(topk), cand, prefix)

    tstar = lax.fori_loop(0, 32, search_body, jnp.full((tq, 1), INT_MIN, jnp.int32))

    def tie_count_body(j, carry):
        ag, ae = carry
        kt = skey_ref[j]
        return (ag + fold(jnp.where(kt > tstar, 1.0, 0.0)),
                ae + fold(jnp.where(kt == tstar, 1.0, 0.0)))

    ag, ae = lax.fori_loop(0, nw, tie_count_body,
                           (jnp.zeros((tq, LANE), F32), jnp.zeros((tq, LANE), F32)))
    need = float(topk) - jnp.sum(ag, axis=1, keepdims=True)
    n_eq = jnp.sum(ae, axis=1, keepdims=True)
    bad = (n_eq > need) & (tstar > KEY_NEG_INF)
    n_bad = jnp.sum(jnp.where(bad, 1.0, 0.0))

    @pl.when(n_bad > 0.0)
    def _():
        before = jnp.where(lax.broadcasted_iota(jnp.int32, (w, w), 0)
                           < lax.broadcasted_iota(jnp.int32, (w, w), 1), 1.0, 0.0).astype(BF16)

        def tie_body(t, seen):
            j = nw - 1 - t
            kt = skey_ref[j]
            eq = kt == tstar
            eqf = jnp.where(eq, 1.0, 0.0)
            rank = seen + jnp.dot(eqf.astype(BF16), before, preferred_element_type=F32)
            skey_ref[j] = jnp.where(eq & bad & (rank >= need), KEY_NEG_INF, kt)
            return seen + jnp.sum(eqf, axis=1, keepdims=True)

        lax.fori_loop(0, nw, tie_body, jnp.zeros((tq, 1), F32))

    teff = jnp.maximum(tstar, KEY_NEG_INF + 1)

    m_ref[...] = jnp.full(m_ref.shape, NEG_BIG, F32)
    l_ref[...] = jnp.zeros(l_ref.shape, F32)
    acc_ref[...] = jnp.zeros(acc_ref.shape, F32)

    def attend(j, near):
        row = win_row(j)
        sel = skey_ref[j] >= teff
        for g in range(N_KV_HEADS):
            kb = k_ref[g, pl.ds(row, w), :]
            vb = v_ref[g, pl.ds(row, w), :]
            for jj in range(GQ):
                h = g * GQ + jj
                s = lax.dot_general(q_ref[:, h * HEAD_DIM:(h + 1) * HEAD_DIM], kb,
                                    (((1,), (1,)), ((), ())), preferred_element_type=F32) * ATTN_SCALE
                if near:
                    s = s + nb_ref[h]
                s = jnp.where(sel, s, NEG_BIG)
                m_old = m_ref[h]
                m_new = jnp.maximum(m_old, jnp.max(s, axis=1, keepdims=True))
                alpha = jnp.exp(m_old - m_new)
                p = jnp.exp(s - m_new)
                l_ref[h] = alpha * l_ref[h] + jnp.sum(p, axis=1, keepdims=True)
                acc_ref[h] = alpha * acc_ref[h] + jnp.dot(p.astype(BF16), vb, preferred_element_type=F32)
                m_ref[h] = m_new

    attend(0, True)

    def far_body(j, carry):
        attend(j, False)
        return carry

    lax.fori_loop(1, nw, far_body, 0)

    for h in range(N_HEADS):
        o_ref[:, h * HEAD_DIM:(h + 1) * HEAD_DIM] = (acc_ref[h] / l_ref[h]).astype(o_ref.dtype)


def _rel_bucket(rel):
    half = NUM_BUCKETS // 2
    max_exact = half // 2
    base = jnp.where(rel > 0, half, 0)
    n = jnp.abs(rel)
    nf = jnp.maximum(n, 1).astype(jnp.float32)
    large = max_exact + (jnp.log(nf / max_exact) / math.log(MAX_DISTANCE / max_exact)
                         * (half - max_exact)).astype(jnp.int32)
    large = jnp.minimum(large, half - 1)
    return base + jnp.where(n < max_exact, n, large)


def _near_bias(rel_bias, tq):
    r = jnp.arange(tq, dtype=jnp.int32)[:, None]
    c = jnp.arange(KEY_WINDOW, dtype=jnp.int32)[None, :]
    rel = (tq - KEY_WINDOW + c) - r
    tab = rel_bias.astype(F32)
    far = tab[_rel_bucket(jnp.full((), -(1 << 20), jnp.int32))]
    return jnp.transpose(tab[_rel_bucket(rel)] - far, (2, 0, 1))


def _attention(q, qi, wi, k, v, ki, nb, *, tq, past, front_pad, topk):
    b, t, _ = q.shape
    lp = k.shape[2]
    nw_max = (past + t + KEY_WINDOW - 1) // KEY_WINDOW
    kern = functools.partial(_attn_kernel, tq=tq, past=past, front_pad=front_pad, topk=topk)
    return pl.pallas_call(
        kern,
        out_shape=jax.ShapeDtypeStruct((b, t, N_HEADS * HEAD_DIM), BF16),
        grid=(b, t // tq),
        in_specs=[pl.BlockSpec((None, tq, N_HEADS * HEAD_DIM), lambda bi, i: (bi, i, 0)),
                  pl.BlockSpec((None, tq, IDX_HEADS * IDX_DIM), lambda bi, i: (bi, i, 0)),
                  pl.BlockSpec((None, tq, IDX_HEADS), lambda bi, i: (bi, i, 0)),
                  pl.BlockSpec((None, N_KV_HEADS, lp, HEAD_DIM), lambda bi, i: (bi, 0, 0, 0)),
                  pl.BlockSpec((None, N_KV_HEADS, lp, HEAD_DIM), lambda bi, i: (bi, 0, 0, 0)),
                  pl.BlockSpec((None, lp, IDX_DIM), lambda bi, i: (bi, 0, 0)),
                  pl.BlockSpec((N_HEADS, tq, KEY_WINDOW), lambda bi, i: (0, 0, 0))],
        out_specs=pl.BlockSpec((None, tq, N_HEADS * HEAD_DIM), lambda bi, i: (bi, i, 0)),
        scratch_shapes=[pltpu.VMEM((nw_max, tq, KEY_WINDOW), jnp.int32),
                        pltpu.VMEM((N_HEADS, tq, 1), F32),
                        pltpu.VMEM((N_HEADS, tq, 1), F32),
                        pltpu.VMEM((N_HEADS, tq, HEAD_DIM), F32)],
        compiler_params=_params("parallel", "arbitrary"),
        name="dsa_attention",
    )(q, qi, wi, k, v, ki, nb)


def _dsa_mixer(proj, projb, past_k, past_v, past_ik, rel_bias, b, t):
    past = past_k.shape[1]
    n_keys = past + t
    topk = min(TOPK_MAX, n_keys // 4)
    tq = 128 if t % 128 == 0 else t
    assert KEY_WINDOW % tq == 0 and past % KEY_WINDOW == 0 and tq % 16 == 0 and n_keys >= KEY_WINDOW
    front_pad = KEY_WINDOW - tq

    k_new = proj[:, K_OFF:V_OFF].reshape(b, t, N_KV_HEADS, HEAD_DIM)
    v_new = proj[:, V_OFF:QI_OFF].reshape(b, t, N_KV_HEADS, HEAD_DIM)
    ki_new = proj[:, KI_OFF:WI_OFF].reshape(b, t, IDX_DIM)
    wi = proj[:, WI_OFF:ATTN_COLS].reshape(b, t, IDX_HEADS)
    q = projb[:, Q_OFF:K_OFF].reshape(b, t, N_HEADS * HEAD_DIM)
    qi = projb[:, QI_OFF:KI_OFF].reshape(b, t, IDX_HEADS * IDX_DIM)

    def cat(past_x, new_x):
        x = jnp.concatenate([past_x.astype(BF16), new_x.astype(BF16)], axis=1)
        return jnp.pad(x, ((0, 0), (front_pad, 0)) + ((0, 0),) * (x.ndim - 2))

    k_all = jnp.transpose(cat(past_k, k_new), (0, 2, 1, 3))
    v_all = jnp.transpose(cat(past_v, v_new), (0, 2, 1, 3))
    ki_all = cat(past_ik, ki_new)
    out = _attention(q, qi, wi, k_all, v_all, ki_all, _near_bias(rel_bias, tq),
                     tq=tq, past=past, front_pad=front_pad, topk=topk)
    return out.reshape(b * t, N_HEADS * HEAD_DIM), k_new, v_new, ki_new


def _ssd_kernel(z_ref, xbc_ref, dt_ref, cst_ref, st0_ref, cw_ref, cb_ref, dtb_ref, alog_ref,
                dexp_ref, nw_ref, expand_ref, y_ref, stout_ref,
                xpad_ref, act_ref, yd_ref, st_ref, *, q):
    c = pl.program_id(1)

    @pl.when(c == 0)
    def _():
        xpad_ref[0:8, :] = cst_ref[...]
        st_ref[...] = st0_ref[...]

    cc = 512
    for c0 in range(0, CONV_DIM, cc):
        cs = slice(c0, c0 + cc)
        xpad_ref[8:8 + q, cs] = xbc_ref[:, cs]
        conv = cb_ref[:, cs] + xpad_ref[5:5 + q, cs] * cw_ref[0:1, cs]
        conv = conv + xpad_ref[6:6 + q, cs] * cw_ref[1:2, cs]
        conv = conv + xpad_ref[7:7 + q, cs] * cw_ref[2:3, cs]
        conv = conv + xpad_ref[8:8 + q, cs] * cw_ref[3:4, cs]
        act_ref[:, cs] = conv * _sigmoid(conv)
        xpad_ref[0:8, cs] = xpad_ref[q:q + 8, cs]

    hi = lax.Precision.HIGHEST
    x_dt = dt_ref[:, 0:SSM_HEADS] + dtb_ref[...]
    dt = jnp.maximum(x_dt, 0.0) + jnp.log1p(jnp.exp(-jnp.abs(x_dt)))
    a = dt * (-jnp.exp(alog_ref[...]))
    rr = lax.broadcasted_iota(jnp.int32, (q, q), 0)
    cl = lax.broadcasted_iota(jnp.int32, (q, q), 1)
    causal = rr >= cl
    tri = jnp.where(causal, 1.0, 0.0)
    a_cs = jnp.dot(tri, a, precision=hi, preferred_element_type=F32)
    eye = jnp.where(lax.broadcasted_iota(jnp.int32, (SSM_HEADS, SSM_HEADS), 0)
                    == lax.broadcasted_iota(jnp.int32, (SSM_HEADS, SSM_HEADS), 1), 1.0, 0.0)
    a_cs_t = lax.dot_general(eye, a_cs, (((1,), (1,)), ((), ())), precision=hi,
                             preferred_element_type=F32)
    a_end = a_cs[q - 1:q, :]
    expand = expand_ref[...]
    dt_x = jnp.dot(dt, expand, precision=hi, preferred_element_type=F32)
    e_x = jnp.dot(jnp.exp(a_cs), expand, precision=hi, preferred_element_type=F32)
    dte_x = jnp.dot(jnp.exp(a_end - a_cs), expand, precision=hi, preferred_element_type=F32)
    dec_x = jnp.dot(jnp.exp(a_cs[q - 8:q, :]), expand, precision=hi,
                    preferred_element_type=F32)[7:8, :]

    for g in range(SSM_GROUPS):
        gs = slice(g * GROUP_W, (g + 1) * GROUP_W)
        bg = act_ref[:, D_INNER + g * D_STATE:D_INNER + (g + 1) * D_STATE].astype(BF16)
        cg = act_ref[:, D_INNER + (SSM_GROUPS + g) * D_STATE:
                     D_INNER + (SSM_GROUPS + g + 1) * D_STATE].astype(BF16)
        cb = lax.dot_general(cg, bg, (((1,), (1,)), ((), ())), preferred_element_type=F32)
        xs_g = act_ref[:, gs]
        xd_g = xs_g * dt_x[:, gs]
        st_g = st_ref[g]
        y_off = jnp.dot(cg, st_g.astype(BF16), preferred_element_type=F32) * e_x[:, gs]
        for jj in range(HEADS_PER_GROUP):
            h = g * HEADS_PER_GROUP + jj
            seg = a_cs[:, h:h + 1] - a_cs_t[h:h + 1, :]
            decay = jnp.where(causal, jnp.exp(jnp.where(causal, seg, 0.0)), 0.0)
            hs = slice(jj * SSM_HEAD_DIM, (jj + 1) * SSM_HEAD_DIM)
            yd_ref[:, hs] = jnp.dot((cb * decay).astype(BF16), xd_g[:, hs].astype(BF16),
                                    preferred_element_type=F32)
        st_ref[g] = st_g * dec_x[:, gs] + lax.dot_general(
            bg, (xd_g * dte_x[:, gs]).astype(BF16), (((0,), (0,)), ((), ())),
            preferred_element_type=F32)
        y = yd_ref[...] + y_off + xs_g * dexp_ref[:, gs]
        zg = z_ref[:, gs]
        yg = y * (zg * _sigmoid(zg))
        ms = jnp.mean(yg * yg, axis=-1, keepdims=True)
        y_ref[:, gs] = (yg * lax.rsqrt(ms + EPS) * nw_ref[:, gs]).astype(y_ref.dtype)

    @pl.when(c == pl.num_programs(1) - 1)
    def _():
        stout_ref[...] = st_ref[...]


def _ssd(proj, cst, st0, cw, cb, dtb, alog, dexp, nw, expand, *, b, t, q):
    nc = t // q
    return pl.pallas_call(
        functools.partial(_ssd_kernel, q=q),
        out_shape=[jax.ShapeDtypeStruct((b * t, D_INNER), BF16),
                   jax.ShapeDtypeStruct((b, SSM_GROUPS, D_STATE, GROUP_W), F32)],
        grid=(b, nc),
        in_specs=[pl.BlockSpec((q, D_INNER), lambda bi, c: (bi * nc + c, SSM_Z_OFF // D_INNER)),
                  pl.BlockSpec((q, CONV_DIM), lambda bi, c: (bi * nc + c, SSM_XBC_OFF // CONV_DIM)),
                  pl.BlockSpec((q, LANE), lambda bi, c: (bi * nc + c, SSM_DT_OFF // LANE)),
                  pl.BlockSpec((None, 8, CONV_DIM), lambda bi, c: (bi, 0, 0)),
                  pl.BlockSpec((None, SSM_GROUPS, D_STATE, GROUP_W), lambda bi, c: (bi, 0, 0, 0)),
                  pl.BlockSpec((D_CONV, CONV_DIM), lambda bi, c: (0, 0)),
                  pl.BlockSpec((1, CONV_DIM), lambda bi, c: (0, 0)),
                  pl.BlockSpec((1, SSM_HEADS), lambda bi, c: (0, 0)),
                  pl.BlockSpec((1, SSM_HEADS), lambda bi, c: (0, 0)),
                  pl.BlockSpec((1, D_INNER), lambda bi, c: (0, 0)),
                  pl.BlockSpec((1, D_INNER), lambda bi, c: (0, 0)),
                  pl.BlockSpec((SSM_HEADS, D_INNER), lambda bi, c: (0, 0))],
        out_specs=[pl.BlockSpec((q, D_INNER), lambda bi, c: (bi * nc + c, 0)),
                   pl.BlockSpec((None, SSM_GROUPS, D_STATE, GROUP_W), lambda bi, c: (bi, 0, 0, 0))],
        scratch_shapes=[pltpu.VMEM((q + 8, CONV_DIM), F32),
                        pltpu.VMEM((q, CONV_DIM), F32),
                        pltpu.VMEM((q, GROUP_W), F32),
                        pltpu.VMEM((SSM_GROUPS, D_STATE, GROUP_W), F32)],
        compiler_params=_params("parallel", "arbitrary"),
        name="ssd_mixer",
    )(proj, proj, proj, cst, st0, cw, cb, dtb, alog, dexp, nw, expand)


def _ssd_mixer(proj, conv_state, ssm_state, conv_w, conv_b, dt_bias, a_log, d_skip, norm_w, b, t):
    q = 128 if t % 128 == 0 else t
    assert q % 8 == 0 and t >= D_CONV - 1
    cst = jnp.pad(conv_state.astype(F32), ((0, 0), (8 - (D_CONV - 1), 0), (0, 0)))
    st0 = ssm_state.astype(F32).reshape(b, SSM_GROUPS, HEADS_PER_GROUP, SSM_HEAD_DIM, D_STATE)
    st0 = jnp.transpose(st0, (0, 1, 4, 2, 3)).reshape(b, SSM_GROUPS, D_STATE, GROUP_W)
    expand = jnp.repeat(jnp.eye(SSM_HEADS, dtype=F32), SSM_HEAD_DIM, axis=1)
    dexp = jnp.repeat(d_skip.astype(F32), SSM_HEAD_DIM)[None, :]
    y, st = _ssd(proj, cst, st0, conv_w.astype(F32), conv_b.astype(F32)[None, :],
                 dt_bias.astype(F32)[None, :], a_log.astype(F32)[None, :], dexp,
                 norm_w.astype(F32)[None, :], expand, b=b, t=t, q=q)
    xbc = proj[:, SSM_XBC_OFF:SSM_XBC_OFF + CONV_DIM].reshape(b, t, CONV_DIM)
    new_conv = xbc[:, t - (D_CONV - 1):]
    st = st.reshape(b, SSM_GROUPS, D_STATE, HEADS_PER_GROUP, SSM_HEAD_DIM)
    new_ssm = jnp.transpose(st, (0, 1, 3, 4, 2)).reshape(b, SSM_HEADS, SSM_HEAD_DIM, D_STATE)
    return y, new_conv, new_ssm.astype(ssm_state.dtype)


def _tile_m(m):
    for tm in (1024, 512, 256, 128, 64, 32, 16, 8):
        if m % tm == 0:
            return tm
    raise ValueError(m)


def _pad_cols(w, n):
    return jnp.pad(w, ((0, 0), (0, n - w.shape[1])))


def _prep_weights(norm_mix, norm_ffn, norm_final, attn_w_in, attn_w_o, ssm_w_in, ssm_w_out,
                  mlp_w_up, mlp_w_down):
    return dict(
        attn_in=_pad_cols(attn_w_in[0], _round_up(ATTN_COLS, LANE)).astype(BF16),
        attn_o=attn_w_o[0].astype(BF16),
        ssm_in=_pad_cols(jnp.concatenate([ssm_w_in[0][:, D_INNER:D_INNER + CONV_DIM],
                                          ssm_w_in[0][:, :D_INNER],
                                          ssm_w_in[0][:, D_INNER + CONV_DIM:]], axis=1),
                         _round_up(SSM_COLS, LANE)).astype(BF16),
        ssm_out=ssm_w_out[0].astype(BF16),
        up=[mlp_w_up[i].astype(BF16) for i in range(2)],
        down=[mlp_w_down[i].astype(BF16) for i in range(2)],
        g_mix=[norm_mix[i].astype(F32)[None, :] for i in range(2)],
        g_ffn=[norm_ffn[i].astype(F32)[None, :] for i in range(2)],
        g_final=norm_final.astype(F32)[None, :],
    )


def _trunk(x, past_k, past_v, past_ik, conv_st, ssm_st, rel_bias, wts, ssm_conv_w, ssm_conv_b,
           ssm_dt_bias, ssm_a_log, ssm_d, ssm_norm):
    b, t, d = x.shape
    m = b * t
    tm = _tile_m(m)
    x2 = x.reshape(m, d).astype(F32)

    n_attn = wts["attn_in"].shape[1]
    proj, projb = _norm_matmul(x2, wts["g_mix"][0], wts["attn_in"], tm=min(tm, 512), tn=n_attn,
                               emit_bf16=True)
    ao, k_new, v_new, ki_new = _dsa_mixer(proj, projb, past_k, past_v, past_ik, rel_bias, b, t)
    x2 = _matmul_resid(ao, wts["attn_o"], x2, tm=min(tm, 512))
    x2 = _mlp(x2, wts["g_ffn"][0], wts["up"][0], wts["down"][0], wts["g_final"],
              tm=tm, tf=512, final_norm=False)

    n_ssm = wts["ssm_in"].shape[1]
    (proj,) = _norm_matmul(x2, wts["g_mix"][1], wts["ssm_in"], tm=tm, tn=n_ssm // 7,
                           emit_bf16=False)
    y, new_conv, new_ssm = _ssd_mixer(proj, conv_st, ssm_st, ssm_conv_w, ssm_conv_b, ssm_dt_bias,
                                      ssm_a_log, ssm_d, ssm_norm, b, t)
    x2 = _matmul_resid(y, wts["ssm_out"], x2, tm=min(tm, 512))
    x2 = _mlp(x2, wts["g_ffn"][1], wts["up"][1], wts["down"][1], wts["g_final"],
              tm=tm, tf=512, final_norm=True)

    return (x2.reshape(b, t, d).astype(x.dtype), k_new[None].astype(x.dtype), v_new[None].astype(x.dtype),
            ki_new[None].astype(x.dtype), new_conv[None].astype(x.dtype), new_ssm[None])


def kernel(x_prompt, x_sample, cache_k, cache_v, cache_idx_k, state_conv, state_ssm, rel_bias, norm_mix, norm_ffn, norm_final, attn_w_in, attn_w_o, ssm_w_in, ssm_conv_w, ssm_conv_b, ssm_dt_bias, ssm_a_log, ssm_d, ssm_norm, ssm_w_out, mlp_w_up, mlp_w_down):
    wts = _prep_weights(norm_mix, norm_ffn, norm_final, attn_w_in, attn_w_o, ssm_w_in, ssm_w_out,
                        mlp_w_up, mlp_w_down)
    bp = x_prompt.shape[0]
    dtp = x_prompt.dtype
    empty_k = jnp.zeros((bp, 0, N_KV_HEADS, HEAD_DIM), dtp)
    empty_ik = jnp.zeros((bp, 0, IDX_DIM), dtp)
    zero_conv = jnp.zeros((bp, D_CONV - 1, CONV_DIM), dtp)
    zero_ssm = jnp.zeros((bp, SSM_HEADS, SSM_HEAD_DIM, D_STATE), dtp)
    args = (rel_bias, wts, ssm_conv_w[0], ssm_conv_b[0], ssm_dt_bias[0], ssm_a_log[0], ssm_d[0],
            ssm_norm[0])
    yp, kp, vp, ikp, cp, sp = _trunk(x_prompt, empty_k, empty_k, empty_ik, zero_conv, zero_ssm, *args)
    ys, ks, vs, iks, cs, ss = _trunk(x_sample, cache_k[0], cache_v[0], cache_idx_k[0],
                                     state_conv[0], state_ssm[0], *args)
    return (yp, ys, kp, vp, ikp, cp, sp, ks, vs, iks, cs, ss)
```

```python
import functools
import math

import jax
import jax.numpy as jnp
from jax import lax
from jax.experimental import pallas as pl
from jax.experimental.pallas import tpu as pltpu

F32 = jnp.float32
BF16 = jnp.bfloat16

D_MODEL = 1024
CHUNK = 64
CHUNK_SHIFT = 6
N_HEADS = 16
HEAD_DIM = 64
N_KV_HEADS = 4
GQ = N_HEADS // N_KV_HEADS
IDX_HEADS = 8
IDX_DIM = 64
TOPK_MAX = 256
IDX_SCALE = (IDX_HEADS * IDX_DIM) ** -0.5
ATTN_SCALE = HEAD_DIM ** -0.5
NUM_BUCKETS = 32
MAX_DISTANCE = 128
D_INNER = 2 * D_MODEL
SSM_HEAD_DIM = 64
SSM_HEADS = D_INNER // SSM_HEAD_DIM
SSM_GROUPS = 8
HEADS_PER_GROUP = SSM_HEADS // SSM_GROUPS
GROUP_W = HEADS_PER_GROUP * SSM_HEAD_DIM
D_STATE = 128
D_CONV = 4
CONV_DIM = D_INNER + 2 * SSM_GROUPS * D_STATE
D_FF = 4 * D_MODEL
EPS = 1e-6

Q_OFF = 0
K_OFF = N_HEADS * HEAD_DIM
V_OFF = K_OFF + N_KV_HEADS * HEAD_DIM
QI_OFF = V_OFF + N_KV_HEADS * HEAD_DIM
KI_OFF = QI_OFF + IDX_HEADS * IDX_DIM
WI_OFF = KI_OFF + IDX_DIM
ATTN_COLS = WI_OFF + IDX_HEADS

SSM_XBC_OFF = 0
SSM_Z_OFF = CONV_DIM
SSM_DT_OFF = CONV_DIM + D_INNER
SSM_COLS = SSM_DT_OFF + SSM_HEADS

LANE = 128
KEY_WINDOW = 512
MAX_TILE_ROWS = 128
LOG2E = 1.4426950408889634
VMEM_LIMIT = 56 * 1024 * 1024

NEG_BIG = -1e30
KEY_NEG_INF = -2139095041
INT_MIN = -2147483648


def _round_up(n, m):
    return (n + m - 1) // m * m


def _rms(x, g):
    ms = jnp.mean(x * x, axis=-1, keepdims=True)
    return x * lax.rsqrt(ms + EPS) * g


def _sigmoid(x):
    return 1.0 / (1.0 + jnp.exp(-x))


def _params(*sem):
    return pltpu.CompilerParams(dimension_semantics=sem, vmem_limit_bytes=VMEM_LIMIT)


def _norm_matmul_kernel(x_ref, g_ref, w_ref, *rest, emit_bf16):
    if emit_bf16:
        o_ref, ob_ref, xn_ref = rest
    else:
        o_ref, xn_ref = rest

    @pl.when(pl.program_id(1) == 0)
    def _():
        xn_ref[...] = _rms(x_ref[...], g_ref[...]).astype(BF16)

    acc = jnp.dot(xn_ref[...], w_ref[...], preferred_element_type=F32)
    o_ref[...] = acc
    if emit_bf16:
        ob_ref[...] = acc.astype(BF16)


def _norm_matmul(x, g, w, *, tm, tn, emit_bf16):
    m, k = x.shape
    n = w.shape[1]
    out_shape = [jax.ShapeDtypeStruct((m, n), F32)]
    out_specs = [pl.BlockSpec((tm, tn), lambda i, j: (i, j))]
    if emit_bf16:
        out_shape.append(jax.ShapeDtypeStruct((m, n), BF16))
        out_specs.append(pl.BlockSpec((tm, tn), lambda i, j: (i, j)))
    return pl.pallas_call(
        functools.partial(_norm_matmul_kernel, emit_bf16=emit_bf16),
        out_shape=out_shape,
        grid=(m // tm, n // tn),
        in_specs=[pl.BlockSpec((tm, k), lambda i, j: (i, 0)),
                  pl.BlockSpec((1, k), lambda i, j: (0, 0)),
                  pl.BlockSpec((k, tn), lambda i, j: (0, j))],
        out_specs=out_specs,
        scratch_shapes=[pltpu.VMEM((tm, k), BF16)],
        compiler_params=_params("parallel", "arbitrary"),
        name="norm_matmul",
    )(x, g, w)


def _matmul_resid_kernel(a_ref, w_ref, r_ref, o_ref):
    o_ref[...] = r_ref[...] + jnp.dot(a_ref[...], w_ref[...], preferred_element_type=F32)


def _matmul_resid(a, w, resid, *, tm):
    m, k = a.shape
    n = w.shape[1]
    return pl.pallas_call(
        _matmul_resid_kernel,
        out_shape=jax.ShapeDtypeStruct((m, n), F32),
        grid=(m // tm,),
        in_specs=[pl.BlockSpec((tm, k), lambda i: (i, 0)),
                  pl.BlockSpec((k, n), lambda i: (0, 0)),
                  pl.BlockSpec((tm, n), lambda i: (i, 0))],
        out_specs=pl.BlockSpec((tm, n), lambda i: (i, 0)),
        compiler_params=_params("parallel"),
        name="matmul_resid",
    )(a, w, resid)


def _mlp_kernel(x_ref, g_ref, wu_ref, wd_ref, gf_ref, o_ref, xn_ref, acc_ref, *, final_norm):
    f = pl.program_id(1)

    @pl.when(f == 0)
    def _():
        xn_ref[...] = _rms(x_ref[...], g_ref[...]).astype(BF16)
        acc_ref[...] = jnp.zeros_like(acc_ref)

    a = jnp.dot(xn_ref[...], wu_ref[...], preferred_element_type=F32)
    a = jnp.maximum(a, 0.0)
    a = a * a
    acc_ref[...] += jnp.dot(a.astype(BF16), wd_ref[...], preferred_element_type=F32)

    @pl.when(f == pl.num_programs(1) - 1)
    def _():
        y = x_ref[...] + acc_ref[...]
        if final_norm:
            y = _rms(y, gf_ref[...])
        o_ref[...] = y


def _mlp(x, g, wu, wd, gf, *, tm, tf, final_norm):
    m, d = x.shape
    ff = wu.shape[1]
    return pl.pallas_call(
        functools.partial(_mlp_kernel, final_norm=final_norm),
        out_shape=jax.ShapeDtypeStruct((m, d), F32),
        grid=(m // tm, ff // tf),
        in_specs=[pl.BlockSpec((tm, d), lambda i, f: (i, 0)),
                  pl.BlockSpec((1, d), lambda i, f: (0, 0)),
                  pl.BlockSpec((d, tf), lambda i, f: (0, f)),
                  pl.BlockSpec((tf, d), lambda i, f: (f, 0)),
                  pl.BlockSpec((1, d), lambda i, f: (0, 0))],
        out_specs=pl.BlockSpec((tm, d), lambda i, f: (i, 0)),
        scratch_shapes=[pltpu.VMEM((tm, d), BF16), pltpu.VMEM((tm, d), F32)],
        compiler_params=_params("parallel", "arbitrary"),
        name="mlp",
    )(x, g, wu, wd, gf)


def _attn_kernel(q_ref, qi_ref, wi_ref, k_ref, v_ref, ki_ref, nb_ref, o_ref,
                 skey_ref, qs_ref, qis_ref, wcol_ref, madd_ref, s_ref, m_ref, acc_ref,
                 *, tq, past, front_pad, topk):
    w = KEY_WINDOW
    rows = GQ * tq
    rb = madd_ref.shape[0]
    stack_idx = qis_ref.shape[0] == IDX_HEADS * tq
    i = pl.program_id(1)
    q0 = past + i * tq
    lv = q0 + tq
    nw = jnp.right_shift(lv + (w - 1), 9)
    qchunk = jnp.right_shift(q0 + lax.broadcasted_iota(jnp.int32, (tq, 1), 0), CHUNK_SHIFT)
    col = lax.broadcasted_iota(jnp.int32, (1, w), 1)
    nt = (((1,), (1,)), ((), ()))

    def win_row(j):
        return pl.multiple_of(lv + front_pad - w * (j + 1), tq)

    for h in range(N_HEADS):
        g, jj = divmod(h, GQ)
        qs_ref[g, jj * tq:(jj + 1) * tq, :] = q_ref[:, h * HEAD_DIM:(h + 1) * HEAD_DIM]

    wi = wi_ref[...]
    if stack_idx:
        for h in range(IDX_HEADS):
            qis_ref[h * tq:(h + 1) * tq, :] = qi_ref[:, h * IDX_DIM:(h + 1) * IDX_DIM]
            wcol_ref[h * tq:(h + 1) * tq, :] = wi[:, h:h + 1]

    def score_body(j, carry):
        kib = ki_ref[pl.ds(win_row(j), w), :]
        if stack_idx:
            t = jnp.maximum(lax.dot_general(qis_ref[...], kib, nt, preferred_element_type=F32), 0.0)
            t = t * wcol_ref[...]
            sc = t[0:tq]
            for h in range(1, IDX_HEADS):
                sc = sc + t[h * tq:(h + 1) * tq]
        else:
            sc = jnp.zeros((tq, w), F32)
            for h in range(IDX_HEADS):
                s = lax.dot_general(qi_ref[:, h * IDX_DIM:(h + 1) * IDX_DIM], kib, nt,
                                    preferred_element_type=F32)
                sc = sc + jnp.maximum(s, 0.0) * wi[:, h:h + 1]
        sc = sc * IDX_SCALE
        kpos = (lv - w * (j + 1)) + col
        vis = (kpos >= 0) & (jnp.right_shift(kpos, CHUNK_SHIFT) <= qchunk)
        bits = pltpu.bitcast(sc, jnp.int32)
        key = bits ^ (jnp.right_shift(bits, 31) & 0x7FFFFFFF)
        key = jnp.where(sc == 0.0, 0, key)
        skey_ref[j] = jnp.where(vis, key, KEY_NEG_INF)
        return carry

    lax.fori_loop(0, nw, score_body, 0)

    def fold(c):
        return (c[:, 0:LANE] + c[:, LANE:2 * LANE]) + (c[:, 2 * LANE:3 * LANE] + c[:, 3 * LANE:4 * LANE])

    def count_ge(cand):
        def body(j, acc):
            return acc + fold(jnp.where(skey_ref[j] >= cand, 1.0, 0.0))
        acc = lax.fori_loop(0, nw, body, jnp.zeros((tq, LANE), F32))
        return jnp.sum(acc, axis=1, keepdims=True)

    def search_body(it, prefix):
        cand = prefix + lax.shift_left(jnp.int32(1), 31 - it)
        return jnp.where(count_ge(cand) >= float(topk), cand, prefix)

    tstar = lax.fori_loop(0, 32, search_body, jnp.full((tq, 1), INT_MIN, jnp.int32))

    def tie_count_body(j, carry):
        ag, ae = carry
        kt = skey_ref[j]
        return (ag + fold(jnp.where(kt > tstar, 1.0, 0.0)),
                ae + fold(jnp.where(kt == tstar, 1.0, 0.0)))

    ag, ae = lax.fori_loop(0, nw, tie_count_body,
                           (jnp.zeros((tq, LANE), F32), jnp.zeros((tq, LANE), F32)))
    need = float(topk) - jnp.sum(ag, axis=1, keepdims=True)
    n_eq = jnp.sum(ae, axis=1, keepdims=True)
    bad = (n_eq > need) & (tstar > KEY_NEG_INF)
    n_bad = jnp.sum(jnp.where(bad, 1.0, 0.0))

    @pl.when(n_bad > 0.0)
    def _():
        before = jnp.where(lax.broadcasted_iota(jnp.int32, (w, w), 0)
                           < lax.broadcasted_iota(jnp.int32, (w, w), 1), 1.0, 0.0).astype(BF16)

        def tie_body(t, seen):
            j = nw - 1 - t
            kt = skey_ref[j]
            eq = kt == tstar
            eqf = jnp.where(eq, 1.0, 0.0)
            rank = seen + jnp.dot(eqf.astype(BF16), before, preferred_element_type=F32)
            skey_ref[j] = jnp.where(eq & bad & (rank >= need), KEY_NEG_INF, kt)
            return seen + jnp.sum(eqf, axis=1, keepdims=True)

        lax.fori_loop(0, nw, tie_body, jnp.zeros((tq, 1), F32))

    teff = jnp.maximum(tstar, KEY_NEG_INF + 1)

    m_ref[...] = jnp.full(m_ref.shape, NEG_BIG, F32)
    acc_ref[...] = jnp.zeros(acc_ref.shape, F32)

    def attend(j, near):
        row = win_row(j)
        tile = jnp.where(skey_ref[j] >= teff, 0.0, NEG_BIG)
        for r in range(rb // tq):
            madd_ref[r * tq:(r + 1) * tq, :] = tile
        tiles = [(g, r0) for g in range(N_KV_HEADS) for r0 in range(0, rows, rb)]

        for t, (g, r0) in enumerate(tiles):
            kb = k_ref[g, pl.ds(row, w), :]
            s = lax.dot_general(qs_ref[g, r0:r0 + rb, :], kb, nt, preferred_element_type=F32)
            s = s + madd_ref[...]
            if near:
                s = s + nb_ref[g, r0:r0 + rb, :]
            s_ref[t] = s
        for t, (g, r0) in enumerate(tiles):
            rs = slice(r0, r0 + rb)
            s = s_ref[t]
            m_old = m_ref[g, rs, :]
            m_new = jnp.maximum(m_old, jnp.max(s, axis=1, keepdims=True))
            p = jnp.exp2(s - jnp.concatenate([m_new] * (w // LANE), axis=1))
            vb = v_ref[g, pl.ds(row, w), :]
            acc_ref[g, rs, :] = (jnp.exp2(m_old - m_new) * acc_ref[g, rs, :]
                                 + jnp.dot(p.astype(BF16), vb, preferred_element_type=F32))
            m_ref[g, rs, :] = m_new

    attend(0, True)

    def far_body(j, carry):
        attend(j, False)
        return carry

    lax.fori_loop(1, nw, far_body, 0)

    for h in range(N_HEADS):
        g, jj = divmod(h, GQ)
        a = acc_ref[g, jj * tq:(jj + 1) * tq, :]
        o_ref[:, h * HEAD_DIM:(h + 1) * HEAD_DIM] = (
            a[:, 0:HEAD_DIM] / a[:, HEAD_DIM:HEAD_DIM + 1]).astype(o_ref.dtype)


def _rel_bucket(rel):
    half = NUM_BUCKETS // 2
    max_exact = half // 2
    base = jnp.where(rel > 0, half, 0)
    n = jnp.abs(rel)
    nf = jnp.maximum(n, 1).astype(jnp.float32)
    large = max_exact + (jnp.log(nf / max_exact) / math.log(MAX_DISTANCE / max_exact)
                         * (half - max_exact)).astype(jnp.int32)
    large = jnp.minimum(large, half - 1)
    return base + jnp.where(n < max_exact, n, large)


def _near_bias(rel_bias, tq):
    r = jnp.arange(tq, dtype=jnp.int32)[:, None]
    c = jnp.arange(KEY_WINDOW, dtype=jnp.int32)[None, :]
    rel = (tq - KEY_WINDOW + c) - r
    tab = rel_bias.astype(F32)
    far = _rel_bucket(jnp.full((1,), -(1 << 20), jnp.int32))
    far_row = jnp.dot(jax.nn.one_hot(far, NUM_BUCKETS, dtype=F32), tab, precision=lax.Precision.HIGHEST)
    pick = jax.nn.one_hot(_rel_bucket(rel), NUM_BUCKETS, dtype=F32)
    nb = jnp.einsum('qcb,bh->hqc', pick, (tab - far_row) * LOG2E, precision=lax.Precision.HIGHEST)
    return nb.reshape(N_KV_HEADS, GQ * tq, KEY_WINDOW)


def _attention(q, qi, wi, k, v, ki, nb, *, tq, past, front_pad, topk):
    b, t, _ = q.shape
    lp = k.shape[2]
    nw_max = (past + t + KEY_WINDOW - 1) // KEY_WINDOW
    rows = GQ * tq
    rb = min(rows, MAX_TILE_ROWS)
    stack_idx = IDX_HEADS * tq <= MAX_TILE_ROWS
    kern = functools.partial(_attn_kernel, tq=tq, past=past, front_pad=front_pad, topk=topk)
    return pl.pallas_call(
        kern,
        out_shape=jax.ShapeDtypeStruct((b, t, N_HEADS * HEAD_DIM), BF16),
        grid=(b, t // tq),
        in_specs=[pl.BlockSpec((None, tq, N_HEADS * HEAD_DIM), lambda bi, i: (bi, i, 0)),
                  pl.BlockSpec((None, tq, IDX_HEADS * IDX_DIM), lambda bi, i: (bi, i, 0)),
                  pl.BlockSpec((None, tq, IDX_HEADS), lambda bi, i: (bi, i, 0)),
                  pl.BlockSpec((None, N_KV_HEADS, lp, HEAD_DIM), lambda bi, i: (bi, 0, 0, 0)),
                  pl.BlockSpec((None, N_KV_HEADS, lp, LANE), lambda bi, i: (bi, 0, 0, 0)),
                  pl.BlockSpec((None, lp, IDX_DIM), lambda bi, i: (bi, 0, 0)),
                  pl.BlockSpec((N_KV_HEADS, rows, KEY_WINDOW), lambda bi, i: (0, 0, 0))],
        out_specs=pl.BlockSpec((None, tq, N_HEADS * HEAD_DIM), lambda bi, i: (bi, i, 0)),
        scratch_shapes=[pltpu.VMEM((nw_max, tq, KEY_WINDOW), jnp.int32),
                        pltpu.VMEM((N_KV_HEADS, rows, HEAD_DIM), BF16),
                        pltpu.VMEM((IDX_HEADS * tq if stack_idx else 16, IDX_DIM), BF16),
                        pltpu.VMEM((IDX_HEADS * tq if stack_idx else 8, 1), F32),
                        pltpu.VMEM((rb, KEY_WINDOW), F32),
                        pltpu.VMEM((N_KV_HEADS * rows // rb, rb, KEY_WINDOW), F32),
                        pltpu.VMEM((N_KV_HEADS, rows, LANE), F32),
                        pltpu.VMEM((N_KV_HEADS, rows, LANE), F32)],
        compiler_params=_params("parallel", "arbitrary"),
        name="dsa_attention",
    )(q, qi, wi, k, v, ki, nb)


def _dsa_mixer(proj, projb, past_k, past_v, past_ik, rel_bias, b, t):
    past = past_k.shape[1]
    n_keys = past + t
    topk = min(TOPK_MAX, n_keys // 4)
    tq = 128 if t % 128 == 0 else t
    assert KEY_WINDOW % tq == 0 and past % KEY_WINDOW == 0 and tq % 16 == 0 and n_keys >= KEY_WINDOW
    front_pad = KEY_WINDOW - tq

    k_new = proj[:, K_OFF:V_OFF].reshape(b, t, N_KV_HEADS, HEAD_DIM)
    v_new = proj[:, V_OFF:QI_OFF].reshape(b, t, N_KV_HEADS, HEAD_DIM)
    ki_new = proj[:, KI_OFF:WI_OFF].reshape(b, t, IDX_DIM)
    wi = proj[:, WI_OFF:ATTN_COLS].reshape(b, t, IDX_HEADS)
    q = projb[:, Q_OFF:K_OFF].reshape(b, t, N_HEADS * HEAD_DIM)
    qi = projb[:, QI_OFF:KI_OFF].reshape(b, t, IDX_HEADS * IDX_DIM)

    def cat(past_x, new_x):
        x = jnp.concatenate([past_x.astype(BF16), new_x.astype(BF16)], axis=1)
        return jnp.pad(x, ((0, 0), (front_pad, 0)) + ((0, 0),) * (x.ndim - 2))

    k_all = jnp.transpose(cat(past_k, k_new), (0, 2, 1, 3))
    v_all = jnp.transpose(cat(past_v, v_new), (0, 2, 1, 3))
    ones_col = (jnp.arange(LANE - HEAD_DIM) == 0).astype(BF16)
    v_all = jnp.concatenate(
        [v_all, jnp.broadcast_to(ones_col, v_all.shape[:3] + (LANE - HEAD_DIM,))], axis=3)
    ki_all = cat(past_ik, ki_new)
    out = _attention(q, qi, wi, k_all, v_all, ki_all, _near_bias(rel_bias, tq),
                     tq=tq, past=past, front_pad=front_pad, topk=topk)
    return out.reshape(b * t, N_HEADS * HEAD_DIM), k_new, v_new, ki_new


def _ssd_kernel(z_ref, xbc_ref, dt_ref, cst_ref, st0_ref, cw_ref, cb_ref, dtb_ref, alog_ref,
                dexp_ref, nw_ref, expand_ref, y_ref, stout_ref,
                xpad_ref, act_ref, yd_ref, st_ref, *, q):
    c = pl.program_id(1)

    @pl.when(c == 0)
    def _():
        xpad_ref[0:8, :] = cst_ref[...]
        st_ref[...] = st0_ref[...]

    cc = 512
    for c0 in range(0, CONV_DIM, cc):
        cs = slice(c0, c0 + cc)
        xpad_ref[8:8 + q, cs] = xbc_ref[:, cs]
        conv = cb_ref[:, cs] + xpad_ref[5:5 + q, cs] * cw_ref[0:1, cs]
        conv = conv + xpad_ref[6:6 + q, cs] * cw_ref[1:2, cs]
        conv = conv + xpad_ref[7:7 + q, cs] * cw_ref[2:3, cs]
        conv = conv + xpad_ref[8:8 + q, cs] * cw_ref[3:4, cs]
        act_ref[:, cs] = conv * _sigmoid(conv)
        xpad_ref[0:8, cs] = xpad_ref[q:q + 8, cs]

    hi = lax.Precision.HIGHEST
    x_dt = dt_ref[:, 0:SSM_HEADS] + dtb_ref[...]
    dt = jnp.maximum(x_dt, 0.0) + jnp.log1p(jnp.exp(-jnp.abs(x_dt)))
    a = dt * (-jnp.exp(alog_ref[...]))
    rr = lax.broadcasted_iota(jnp.int32, (q, q), 0)
    cl = lax.broadcasted_iota(jnp.int32, (q, q), 1)
    causal = rr >= cl
    tri = jnp.where(causal, 1.0, 0.0)
    a_cs = jnp.dot(tri, a, precision=hi, preferred_element_type=F32)
    eye = jnp.where(lax.broadcasted_iota(jnp.int32, (SSM_HEADS, SSM_HEADS), 0)
                    == lax.broadcasted_iota(jnp.int32, (SSM_HEADS, SSM_HEADS), 1), 1.0, 0.0)
    a_cs_t = lax.dot_general(eye, a_cs, (((1,), (1,)), ((), ())), precision=hi,
                             preferred_element_type=F32)
    a_end = a_cs[q - 1:q, :]
    expand = expand_ref[...]
    dt_x = jnp.dot(dt, expand, precision=hi, preferred_element_type=F32)
    e_x = jnp.dot(jnp.exp(a_cs), expand, precision=hi, preferred_element_type=F32)
    dte_x = jnp.dot(jnp.exp(a_end - a_cs), expand, precision=hi, preferred_element_type=F32)
    dec_x = jnp.dot(jnp.exp(a_cs[q - 8:q, :]), expand, precision=hi,
                    preferred_element_type=F32)[7:8, :]

    for g in range(SSM_GROUPS):
        gs = slice(g * GROUP_W, (g + 1) * GROUP_W)
        bg = act_ref[:, D_INNER + g * D_STATE:D_INNER + (g + 1) * D_STATE].astype(BF16)
        cg = act_ref[:, D_INNER + (SSM_GROUPS + g) * D_STATE:
                     D_INNER + (SSM_GROUPS + g + 1) * D_STATE].astype(BF16)
        cb = lax.dot_general(cg, bg, (((1,), (1,)), ((), ())), preferred_element_type=F32)
        xs_g = act_ref[:, gs]
        xd_g = xs_g * dt_x[:, gs]
        st_g = st_ref[g]
        y_off = jnp.dot(cg, st_g.astype(BF16), preferred_element_type=F32) * e_x[:, gs]
        for jj in range(HEADS_PER_GROUP):
            h = g * HEADS_PER_GROUP + jj
            seg = a_cs[:, h:h + 1] - a_cs_t[h:h + 1, :]
            decay = jnp.where(causal, jnp.exp(jnp.where(causal, seg, 0.0)), 0.0)
            hs = slice(jj * SSM_HEAD_DIM, (jj + 1) * SSM_HEAD_DIM)
            yd_ref[:, hs] = jnp.dot((cb * decay).astype(BF16), xd_g[:, hs].astype(BF16),
                                    preferred_element_type=F32)
        st_ref[g] = st_g * dec_x[:, gs] + lax.dot_general(
            bg, (xd_g * dte_x[:, gs]).astype(BF16), (((0,), (0,)), ((), ())),
            preferred_element_type=F32)
        y = yd_ref[...] + y_off + xs_g * dexp_ref[:, gs]
        zg = z_ref[:, gs]
        yg = y * (zg * _sigmoid(zg))
        ms = jnp.mean(yg * yg, axis=-1, keepdims=True)
        y_ref[:, gs] = (yg * lax.rsqrt(ms + EPS) * nw_ref[:, gs]).astype(y_ref.dtype)

    @pl.when(c == pl.num_programs(1) - 1)
    def _():
        stout_ref[...] = st_ref[...]


def _ssd(proj, cst, st0, cw, cb, dtb, alog, dexp, nw, expand, *, b, t, q):
    nc = t // q
    return pl.pallas_call(
        functools.partial(_ssd_kernel, q=q),
        out_shape=[jax.ShapeDtypeStruct((b * t, D_INNER), BF16),
                   jax.ShapeDtypeStruct((b, SSM_GROUPS, D_STATE, GROUP_W), F32)],
        grid=(b, nc),
        in_specs=[pl.BlockSpec((q, D_INNER), lambda bi, c: (bi * nc + c, SSM_Z_OFF // D_INNER)),
                  pl.BlockSpec((q, CONV_DIM), lambda bi, c: (bi * nc + c, SSM_XBC_OFF // CONV_DIM)),
                  pl.BlockSpec((q, LANE), lambda bi, c: (bi * nc + c, SSM_DT_OFF // LANE)),
                  pl.BlockSpec((None, 8, CONV_DIM), lambda bi, c: (bi, 0, 0)),
                  pl.BlockSpec((None, SSM_GROUPS, D_STATE, GROUP_W), lambda bi, c: (bi, 0, 0, 0)),
                  pl.BlockSpec((D_CONV, CONV_DIM), lambda bi, c: (0, 0)),
                  pl.BlockSpec((1, CONV_DIM), lambda bi, c: (0, 0)),
                  pl.BlockSpec((1, SSM_HEADS), lambda bi, c: (0, 0)),
                  pl.BlockSpec((1, SSM_HEADS), lambda bi, c: (0, 0)),
                  pl.BlockSpec((1, D_INNER), lambda bi, c: (0, 0)),
                  pl.BlockSpec((1, D_INNER), lambda bi, c: (0, 0)),
                  pl.BlockSpec((SSM_HEADS, D_INNER), lambda bi, c: (0, 0))],
        out_specs=[pl.BlockSpec((q, D_INNER), lambda bi, c: (bi * nc + c, 0)),
                   pl.BlockSpec((None, SSM_GROUPS, D_STATE, GROUP_W), lambda bi, c: (bi, 0, 0, 0))],
        scratch_shapes=[pltpu.VMEM((q + 8, CONV_DIM), F32),
                        pltpu.VMEM((q, CONV_DIM), F32),
                        pltpu.VMEM((q, GROUP_W), F32),
                        pltpu.VMEM((SSM_GROUPS, D_STATE, GROUP_W), F32)],
        compiler_params=_params("parallel", "arbitrary"),
        name="ssd_mixer",
    )(proj, proj, proj, cst, st0, cw, cb, dtb, alog, dexp, nw, expand)


def _ssd_mixer(proj, conv_state, ssm_state, conv_w, conv_b, dt_bias, a_log, d_skip, norm_w, b, t):
    q = 128 if t % 128 == 0 else t
    assert q % 8 == 0 and t >= D_CONV - 1
    cst = jnp.pad(conv_state.astype(F32), ((0, 0), (8 - (D_CONV - 1), 0), (0, 0)))
    st0 = ssm_state.astype(F32).reshape(b, SSM_GROUPS, HEADS_PER_GROUP, SSM_HEAD_DIM, D_STATE)
    st0 = jnp.transpose(st0, (0, 1, 4, 2, 3)).reshape(b, SSM_GROUPS, D_STATE, GROUP_W)
    expand = jnp.repeat(jnp.eye(SSM_HEADS, dtype=F32), SSM_HEAD_DIM, axis=1)
    dexp = jnp.repeat(d_skip.astype(F32), SSM_HEAD_DIM)[None, :]
    y, st = _ssd(proj, cst, st0, conv_w.astype(F32), conv_b.astype(F32)[None, :],
                 dt_bias.astype(F32)[None, :], a_log.astype(F32)[None, :], dexp,
                 norm_w.astype(F32)[None, :], expand, b=b, t=t, q=q)
    xbc = proj[:, SSM_XBC_OFF:SSM_XBC_OFF + CONV_DIM].reshape(b, t, CONV_DIM)
    new_conv = xbc[:, t - (D_CONV - 1):]
    st = st.reshape(b, SSM_GROUPS, D_STATE, HEADS_PER_GROUP, SSM_HEAD_DIM)
    new_ssm = jnp.transpose(st, (0, 1, 3, 4, 2)).reshape(b, SSM_HEADS, SSM_HEAD_DIM, D_STATE)
    return y, new_conv, new_ssm.astype(ssm_state.dtype)


def _tile_m(m):
    for tm in (1024, 512, 256, 128, 64, 32, 16, 8):
        if m % tm == 0:
            return tm
    raise ValueError(m)


def _pad_cols(w, n):
    return jnp.pad(w, ((0, 0), (0, n - w.shape[1])))


def _prep_weights(norm_mix, norm_ffn, norm_final, attn_w_in, attn_w_o, ssm_w_in, ssm_w_out,
                  mlp_w_up, mlp_w_down):
    col_scale = jnp.where(jnp.arange(ATTN_COLS) < K_OFF, ATTN_SCALE * LOG2E, 1.0).astype(F32)
    return dict(
        attn_in=_pad_cols(attn_w_in[0] * col_scale, _round_up(ATTN_COLS, LANE)).astype(BF16),
        attn_o=attn_w_o[0].astype(BF16),
        ssm_in=_pad_cols(jnp.concatenate([ssm_w_in[0][:, D_INNER:D_INNER + CONV_DIM],
                                          ssm_w_in[0][:, :D_INNER],
                                          ssm_w_in[0][:, D_INNER + CONV_DIM:]], axis=1),
                         _round_up(SSM_COLS, LANE)).astype(BF16),
        ssm_out=ssm_w_out[0].astype(BF16),
        up=[mlp_w_up[i].astype(BF16) for i in range(2)],
        down=[mlp_w_down[i].astype(BF16) for i in range(2)],
        g_mix=[norm_mix[i].astype(F32)[None, :] for i in range(2)],
        g_ffn=[norm_ffn[i].astype(F32)[None, :] for i in range(2)],
        g_final=norm_final.astype(F32)[None, :],
    )


def _trunk(x, past_k, past_v, past_ik, conv_st, ssm_st, rel_bias, wts, ssm_conv_w, ssm_conv_b,
           ssm_dt_bias, ssm_a_log, ssm_d, ssm_norm):
    b, t, d = x.shape
    m = b * t
    tm = _tile_m(m)
    x2 = x.reshape(m, d).astype(F32)

    n_attn = wts["attn_in"].shape[1]
    proj, projb = _norm_matmul(x2, wts["g_mix"][0], wts["attn_in"], tm=min(tm, 512), tn=n_attn,
                               emit_bf16=True)
    ao, k_new, v_new, ki_new = _dsa_mixer(proj, projb, past_k, past_v, past_ik, rel_bias, b, t)
    x2 = _matmul_resid(ao, wts["attn_o"], x2, tm=min(tm, 512))
    x2 = _mlp(x2, wts["g_ffn"][0], wts["up"][0], wts["down"][0], wts["g_final"],
              tm=tm, tf=512, final_norm=False)

    n_ssm = wts["ssm_in"].shape[1]
    (proj,) = _norm_matmul(x2, wts["g_mix"][1], wts["ssm_in"], tm=tm, tn=n_ssm // 7,
                           emit_bf16=False)
    y, new_conv, new_ssm = _ssd_mixer(proj, conv_st, ssm_st, ssm_conv_w, ssm_conv_b, ssm_dt_bias,
                                      ssm_a_log, ssm_d, ssm_norm, b, t)
    x2 = _matmul_resid(y, wts["ssm_out"], x2, tm=min(tm, 512))
    x2 = _mlp(x2, wts["g_ffn"][1], wts["up"][1], wts["down"][1], wts["g_final"],
              tm=tm, tf=512, final_norm=True)

    return (x2.reshape(b, t, d).astype(x.dtype), k_new[None].astype(x.dtype), v_new[None].astype(x.dtype),
            ki_new[None].astype(x.dtype), new_conv[None].astype(x.dtype), new_ssm[None])


def kernel(x_prompt, x_sample, cache_k, cache_v, cache_idx_k, state_conv, state_ssm, rel_bias, norm_mix, norm_ffn, norm_final, attn_w_in, attn_w_o, ssm_w_in, ssm_conv_w, ssm_conv_b, ssm_dt_bias, ssm_a_log, ssm_d, ssm_norm, ssm_w_out, mlp_w_up, mlp_w_down):
    wts = _prep_weights(norm_mix, norm_ffn, norm_final, attn_w_in, attn_w_o, ssm_w_in, ssm_w_out,
                        mlp_w_up, mlp_w_down)
    bp = x_prompt.shape[0]
    dtp = x_prompt.dtype
    empty_k = jnp.zeros((bp, 0, N_KV_HEADS, HEAD_DIM), dtp)
    empty_ik = jnp.zeros((bp, 0, IDX_DIM), dtp)
    zero_conv = jnp.zeros((bp, D_CONV - 1, CONV_DIM), dtp)
    zero_ssm = jnp.zeros((bp, SSM_HEADS, SSM_HEAD_DIM, D_STATE), dtp)
    args = (rel_bias, wts, ssm_conv_w[0], ssm_conv_b[0], ssm_dt_bias[0], ssm_a_log[0], ssm_d[0],
            ssm_norm[0])
    yp, kp, vp, ikp, cp, sp = _trunk(x_prompt, empty_k, empty_k, empty_ik, zero_conv, zero_ssm, *args)
    ys, ks, vs, iks, cs, ss = _trunk(x_sample, cache_k[0], cache_v[0], cache_idx_k[0],
                                     state_conv[0], state_ssm[0], *args)
    return (yp, ys, kp, vp, ikp, cp, sp, ks, vs, iks, cs, ss)
```

```python
import functools
import math

import jax
import jax.numpy as jnp
from jax import lax
from jax.experimental import pallas as pl
from jax.experimental.pallas import tpu as pltpu

F32 = jnp.float32
BF16 = jnp.bfloat16

D_MODEL = 1024
CHUNK = 64
CHUNK_SHIFT = 6
N_HEADS = 16
HEAD_DIM = 64
N_KV_HEADS = 4
GQ = N_HEADS // N_KV_HEADS
IDX_HEADS = 8
IDX_DIM = 64
TOPK_MAX = 256
IDX_SCALE = (IDX_HEADS * IDX_DIM) ** -0.5
ATTN_SCALE = HEAD_DIM ** -0.5
NUM_BUCKETS = 32
MAX_DISTANCE = 128
D_INNER = 2 * D_MODEL
SSM_HEAD_DIM = 64
SSM_HEADS = D_INNER // SSM_HEAD_DIM
SSM_GROUPS = 8
HEADS_PER_GROUP = SSM_HEADS // SSM_GROUPS
GROUP_W = HEADS_PER_GROUP * SSM_HEAD_DIM
D_STATE = 128
D_CONV = 4
CONV_DIM = D_INNER + 2 * SSM_GROUPS * D_STATE
D_FF = 4 * D_MODEL
EPS = 1e-6

Q_OFF = 0
K_OFF = N_HEADS * HEAD_DIM
V_OFF = K_OFF + N_KV_HEADS * HEAD_DIM
QI_OFF = V_OFF + N_KV_HEADS * HEAD_DIM
KI_OFF = QI_OFF + IDX_HEADS * IDX_DIM
WI_OFF = KI_OFF + IDX_DIM
ATTN_COLS = WI_OFF + IDX_HEADS

SSM_Z_OFF = 0
SSM_XBC_OFF = D_INNER
SSM_DT_OFF = D_INNER + CONV_DIM
SSM_COLS = SSM_DT_OFF + SSM_HEADS

LANE = 128
KEY_WINDOW = 512
MAX_TILE_ROWS = 128
LOG2E = 1.4426950408889634
VMEM_LIMIT = 56 * 1024 * 1024

NEG_BIG = -1e30
KEY_NEG_INF = -2139095041
INT_MIN = -2147483648


def _round_up(n, m):
    return (n + m - 1) // m * m


def _rms(x, g):
    ms = jnp.mean(x * x, axis=-1, keepdims=True)
    return x * lax.rsqrt(ms + EPS) * g


def _sigmoid(x):
    return 1.0 / (1.0 + jnp.exp(-x))


def _params(*sem):
    return pltpu.CompilerParams(dimension_semantics=sem, vmem_limit_bytes=VMEM_LIMIT)


def _norm_matmul_kernel(x_ref, g_ref, w_ref, o_ref, xn_ref):
    @pl.when(pl.program_id(1) == 0)
    def _():
        xn_ref[...] = _rms(x_ref[...], g_ref[...]).astype(BF16)

    o_ref[...] = jnp.dot(xn_ref[...], w_ref[...], preferred_element_type=F32)


def _norm_matmul(x, g, w, *, tm, tn):
    m, k = x.shape
    n = w.shape[1]
    return pl.pallas_call(
        _norm_matmul_kernel,
        out_shape=jax.ShapeDtypeStruct((m, n), F32),
        grid=(m // tm, n // tn),
        in_specs=[pl.BlockSpec((tm, k), lambda i, j: (i, 0)),
                  pl.BlockSpec((1, k), lambda i, j: (0, 0)),
                  pl.BlockSpec((k, tn), lambda i, j: (0, j))],
        out_specs=pl.BlockSpec((tm, tn), lambda i, j: (i, j)),
        scratch_shapes=[pltpu.VMEM((tm, k), BF16)],
        compiler_params=_params("parallel", "arbitrary"),
        name="norm_matmul",
    )(x, g, w)


def _attn_proj_kernel(x_ref, g_ref, w_ref, pb_ref, k_ref, v_ref, ki_ref, kiwi_ref):
    xn = _rms(x_ref[...], g_ref[...]).astype(BF16)
    acc = jnp.dot(xn, w_ref[...], preferred_element_type=F32)
    pb_ref[...] = acc.astype(BF16)
    k_ref[...] = acc[:, K_OFF:V_OFF]
    v_ref[...] = acc[:, V_OFF:QI_OFF]
    ki_ref[...] = acc[:, KI_OFF:WI_OFF]
    kiwi_ref[...] = acc[:, KI_OFF:KI_OFF + LANE]


def _attn_proj(x, g, w, *, tm):
    m, k = x.shape
    n = w.shape[1]
    kvw = N_KV_HEADS * HEAD_DIM
    row = lambda i: (i, 0)
    return pl.pallas_call(
        _attn_proj_kernel,
        out_shape=[jax.ShapeDtypeStruct((m, n), BF16),
                   jax.ShapeDtypeStruct((m, kvw), F32),
                   jax.ShapeDtypeStruct((m, kvw), F32),
                   jax.ShapeDtypeStruct((m, IDX_DIM), F32),
                   jax.ShapeDtypeStruct((m, LANE), F32)],
        grid=(m // tm,),
        in_specs=[pl.BlockSpec((tm, k), row),
                  pl.BlockSpec((1, k), lambda i: (0, 0)),
                  pl.BlockSpec((k, n), lambda i: (0, 0))],
        out_specs=[pl.BlockSpec((tm, n), row), pl.BlockSpec((tm, kvw), row), pl.BlockSpec((tm, kvw), row),
                   pl.BlockSpec((tm, IDX_DIM), row), pl.BlockSpec((tm, LANE), row)],
        compiler_params=_params("parallel"),
        name="attn_proj",
    )(x, g, w)


def _matmul_resid_kernel(a_ref, w_ref, r_ref, o_ref):
    o_ref[...] = r_ref[...] + jnp.dot(a_ref[...], w_ref[...], preferred_element_type=F32)


def _matmul_resid(a, w, resid, *, tm):
    m, k = a.shape
    n = w.shape[1]
    return pl.pallas_call(
        _matmul_resid_kernel,
        out_shape=jax.ShapeDtypeStruct((m, n), F32),
        grid=(m // tm,),
        in_specs=[pl.BlockSpec((tm, k), lambda i: (i, 0)),
                  pl.BlockSpec((k, n), lambda i: (0, 0)),
                  pl.BlockSpec((tm, n), lambda i: (i, 0))],
        out_specs=pl.BlockSpec((tm, n), lambda i: (i, 0)),
        compiler_params=_params("parallel"),
        name="matmul_resid",
    )(a, w, resid)


def _mlp_kernel(x_ref, g_ref, wu_ref, wd_ref, gf_ref, o_ref, xn_ref, acc_ref, *, final_norm):
    f = pl.program_id(1)

    @pl.when(f == 0)
    def _():
        xn_ref[...] = _rms(x_ref[...], g_ref[...]).astype(BF16)
        acc_ref[...] = jnp.zeros_like(acc_ref)

    a = jnp.dot(xn_ref[...], wu_ref[...], preferred_element_type=F32)
    a = jnp.maximum(a, 0.0)
    a = a * a
    acc_ref[...] += jnp.dot(a.astype(BF16), wd_ref[...], preferred_element_type=F32)

    @pl.when(f == pl.num_programs(1) - 1)
    def _():
        y = x_ref[...] + acc_ref[...]
        if final_norm:
            y = _rms(y, gf_ref[...])
        o_ref[...] = y


def _mlp(x, g, wu, wd, gf, *, tm, tf, final_norm):
    m, d = x.shape
    ff = wu.shape[1]
    return pl.pallas_call(
        functools.partial(_mlp_kernel, final_norm=final_norm),
        out_shape=jax.ShapeDtypeStruct((m, d), F32),
        grid=(m // tm, ff // tf),
        in_specs=[pl.BlockSpec((tm, d), lambda i, f: (i, 0)),
                  pl.BlockSpec((1, d), lambda i, f: (0, 0)),
                  pl.BlockSpec((d, tf), lambda i, f: (0, f)),
                  pl.BlockSpec((tf, d), lambda i, f: (f, 0)),
                  pl.BlockSpec((1, d), lambda i, f: (0, 0))],
        out_specs=pl.BlockSpec((tm, d), lambda i, f: (i, 0)),
        scratch_shapes=[pltpu.VMEM((tm, d), BF16), pltpu.VMEM((tm, d), F32)],
        compiler_params=_params("parallel", "arbitrary"),
        name="mlp",
    )(x, g, wu, wd, gf)


def _attn_kernel(*refs, tq, t, past, front_pad, topk):
    if past:
        (q_ref, qi_ref, kiwi_ref, kn_ref, vn_ref, kin_ref, pk_ref, pv_ref, pki_ref, nb_ref, o_ref,
         k_ref, v_ref, ki_ref, skey_ref, qs_ref, qis_ref, wcol_ref, madd_ref, s_ref, m_ref, acc_ref) = refs
    else:
        (q_ref, qi_ref, kiwi_ref, kn_ref, vn_ref, kin_ref, nb_ref, o_ref,
         k_ref, v_ref, ki_ref, skey_ref, qs_ref, qis_ref, wcol_ref, madd_ref, s_ref, m_ref, acc_ref) = refs
    w = KEY_WINDOW
    rows = GQ * tq
    rb = madd_ref.shape[0]
    stack_idx = qis_ref.shape[0] == IDX_HEADS * tq
    i = pl.program_id(1)
    q0 = past + i * tq
    lv = q0 + tq
    nw = jnp.right_shift(lv + (w - 1), 9)
    qchunk = jnp.right_shift(q0 + lax.broadcasted_iota(jnp.int32, (tq, 1), 0), CHUNK_SHIFT)
    col = lax.broadcasted_iota(jnp.int32, (1, w), 1)
    nt = (((1,), (1,)), ((), ()))

    def win_row(j):
        return pl.multiple_of(lv + front_pad - w * (j + 1), tq)

    @pl.when(i == 0)
    def _():
        ones_col = jnp.where(lax.broadcasted_iota(jnp.int32, (1, LANE - HEAD_DIM), 1) == 0, 1.0, 0.0)

        def put(r0, n, kc, vc, kic):
            ones = jnp.broadcast_to(ones_col, (n, LANE - HEAD_DIM)).astype(BF16)
            for g in range(N_KV_HEADS):
                gs = slice(g * HEAD_DIM, (g + 1) * HEAD_DIM)
                k_ref[g, r0:r0 + n, :] = kc[:, gs]
                v_ref[g, r0:r0 + n, :] = jnp.concatenate([vc[:, gs], ones], axis=1)
            ki_ref[r0:r0 + n, :] = kic

        kvw = N_KV_HEADS * HEAD_DIM
        if front_pad:
            put(0, front_pad, jnp.zeros((front_pad, kvw), BF16), jnp.zeros((front_pad, kvw), BF16),
                jnp.zeros((front_pad, IDX_DIM), BF16))
        step = min(t, w)
        for c0 in range(0, past, w):
            put(front_pad + c0, w, pk_ref[c0:c0 + w, :].astype(BF16), pv_ref[c0:c0 + w, :].astype(BF16),
                pki_ref[c0:c0 + w, :].astype(BF16))
        for c0 in range(0, t, step):
            put(front_pad + past + c0, step, kn_ref[c0:c0 + step, :], vn_ref[c0:c0 + step, :],
                kin_ref[c0:c0 + step, 0:IDX_DIM])

    for h in range(N_HEADS):
        g, jj = divmod(h, GQ)
        qs_ref[g, jj * tq:(jj + 1) * tq, :] = q_ref[:, h * HEAD_DIM:(h + 1) * HEAD_DIM]

    wi = kiwi_ref[:, IDX_DIM:IDX_DIM + IDX_HEADS]
    if stack_idx:
        for h in range(IDX_HEADS):
            qis_ref[h * tq:(h + 1) * tq, :] = qi_ref[:, h * IDX_DIM:(h + 1) * IDX_DIM]
            wcol_ref[h * tq:(h + 1) * tq, :] = wi[:, h:h + 1]

    def score_body(j, carry):
        kib = ki_ref[pl.ds(win_row(j), w), :]
        if stack_idx:
            t = jnp.maximum(lax.dot_general(qis_ref[...], kib, nt, preferred_element_type=F32), 0.0)
            t = t * wcol_ref[...]
            sc = t[0:tq]
            for h in range(1, IDX_HEADS):
                sc = sc + t[h * tq:(h + 1) * tq]
        else:
            sc = jnp.zeros((tq, w), F32)
            for h in range(IDX_HEADS):
                s = lax.dot_general(qi_ref[:, h * IDX_DIM:(h + 1) * IDX_DIM], kib, nt,
                                    preferred_element_type=F32)
                sc = sc + jnp.maximum(s, 0.0) * wi[:, h:h + 1]
        sc = sc * IDX_SCALE
        kpos = (lv - w * (j + 1)) + col
        vis = (kpos >= 0) & (jnp.right_shift(kpos, CHUNK_SHIFT) <= qchunk)
        bits = pltpu.bitcast(sc, jnp.int32)
        key = bits ^ (jnp.right_shift(bits, 31) & 0x7FFFFFFF)
        key = jnp.where(sc == 0.0, 0, key)
        skey_ref[j] = jnp.where(vis, key, KEY_NEG_INF)
        return carry

    lax.fori_loop(0, nw, score_body, 0)

    def fold(c):
        return (c[:, 0:LANE] + c[:, LANE:2 * LANE]) + (c[:, 2 * LANE:3 * LANE] + c[:, 3 * LANE:4 * LANE])

    def count_ge(cand):
        def body(j, acc):
            return acc + fold(jnp.where(skey_ref[j] >= cand, 1.0, 0.0))
        acc = lax.fori_loop(0, nw, body, jnp.zeros((tq, LANE), F32))
        return jnp.sum(acc, axis=1, keepdims=True)

    def search_body(it, prefix):
        cand = prefix + lax.shift_left(jnp.int32(1), 31 - it)
        return jnp.where(count_ge(cand) >= float(topk), cand, prefix)

    tstar = lax.fori_loop(0, 32, search_body, jnp.full((tq, 1), INT_MIN, jnp.int32))

    def tie_count_body(j, carry):
        ag, ae = carry
        kt = skey_ref[j]
        return (ag + fold(jnp.where(kt > tstar, 1.0, 0.0)),
                ae + fold(jnp.where(kt == tstar, 1.0, 0.0)))

    ag, ae = lax.fori_loop(0, nw, tie_count_body,
                           (jnp.zeros((tq, LANE), F32), jnp.zeros((tq, LANE), F32)))
    need = float(topk) - jnp.sum(ag, axis=1, keepdims=True)
    n_eq = jnp.sum(ae, axis=1, keepdims=True)
    bad = (n_eq > need) & (tstar > KEY_NEG_INF)
    n_bad = jnp.sum(jnp.where(bad, 1.0, 0.0))

    @pl.when(n_bad > 0.0)
    def _():
        before = jnp.where(lax.broadcasted_iota(jnp.int32, (w, w), 0)
                           < lax.broadcasted_iota(jnp.int32, (w, w), 1), 1.0, 0.0).astype(BF16)

        def tie_body(t, seen):
            j = nw - 1 - t
            kt = skey_ref[j]
            eq = kt == tstar
            eqf = jnp.where(eq, 1.0, 0.0)
            rank = seen + jnp.dot(eqf.astype(BF16), before, preferred_element_type=F32)
            skey_ref[j] = jnp.where(eq & bad & (rank >= need), KEY_NEG_INF, kt)
            return seen + jnp.sum(eqf, axis=1, keepdims=True)

        lax.fori_loop(0, nw, tie_body, jnp.zeros((tq, 1), F32))

    teff = jnp.maximum(tstar, KEY_NEG_INF + 1)

    m_ref[...] = jnp.full(m_ref.shape, NEG_BIG, F32)
    acc_ref[...] = jnp.zeros(acc_ref.shape, F32)

    def attend(j, near):
        row = win_row(j)
        tile = jnp.where(skey_ref[j] >= teff, 0.0, NEG_BIG)
        for r in range(rb // tq):
            madd_ref[r * tq:(r + 1) * tq, :] = tile
        tiles = [(g, r0) for g in range(N_KV_HEADS) for r0 in range(0, rows, rb)]

        for t, (g, r0) in enumerate(tiles):
            kb = k_ref[g, pl.ds(row, w), :]
            s = lax.dot_general(qs_ref[g, r0:r0 + rb, :], kb, nt, preferred_element_type=F32)
            s = s + madd_ref[...]
            if near:
                s = s + nb_ref[g, r0:r0 + rb, :]
            s_ref[t] = s
        for t, (g, r0) in enumerate(tiles):
            rs = slice(r0, r0 + rb)
            s = s_ref[t]
            m_old = m_ref[g, rs, :]
            m_new = jnp.maximum(m_old, jnp.max(s, axis=1, keepdims=True))
            p = jnp.exp2(s - jnp.concatenate([m_new] * (w // LANE), axis=1))
            vb = v_ref[g, pl.ds(row, w), :]
            acc_ref[g, rs, :] = (jnp.exp2(m_old - m_new) * acc_ref[g, rs, :]
                                 + jnp.dot(p.astype(BF16), vb, preferred_element_type=F32))
            m_ref[g, rs, :] = m_new

    attend(0, True)

    def far_body(j, carry):
        attend(j, False)
        return carry

    lax.fori_loop(1, nw, far_body, 0)

    for h in range(N_HEADS):
        g, jj = divmod(h, GQ)
        a = acc_ref[g, jj * tq:(jj + 1) * tq, :]
        o_ref[:, h * HEAD_DIM:(h + 1) * HEAD_DIM] = (
            a[:, 0:HEAD_DIM] / a[:, HEAD_DIM:HEAD_DIM + 1]).astype(o_ref.dtype)


def _rel_bucket(rel):
    half = NUM_BUCKETS // 2
    max_exact = half // 2
    base = jnp.where(rel > 0, half, 0)
    n = jnp.abs(rel)
    nf = jnp.maximum(n, 1).astype(jnp.float32)
    large = max_exact + (jnp.log(nf / max_exact) / math.log(MAX_DISTANCE / max_exact)
                         * (half - max_exact)).astype(jnp.int32)
    large = jnp.minimum(large, half - 1)
    return base + jnp.where(n < max_exact, n, large)


def _near_bias(rel_bias, tq):
    r = jnp.arange(tq, dtype=jnp.int32)[:, None]
    c = jnp.arange(KEY_WINDOW, dtype=jnp.int32)[None, :]
    rel = (tq - KEY_WINDOW + c) - r
    tab = rel_bias.astype(F32)
    far = _rel_bucket(jnp.full((1,), -(1 << 20), jnp.int32))
    far_row = jnp.dot(jax.nn.one_hot(far, NUM_BUCKETS, dtype=F32), tab, precision=lax.Precision.HIGHEST)
    pick = jax.nn.one_hot(_rel_bucket(rel), NUM_BUCKETS, dtype=F32)
    nb = jnp.einsum('qcb,bh->hqc', pick, (tab - far_row) * LOG2E, precision=lax.Precision.HIGHEST)
    return nb.reshape(N_KV_HEADS, GQ * tq, KEY_WINDOW)


def _dsa_mixer(projb, kiwi, past_k, past_v, past_ik, rel_bias, b, t):
    past = past_k.shape[1]
    n_keys = past + t
    topk = min(TOPK_MAX, n_keys // 4)
    tq = 128 if t % 128 == 0 else t
    assert KEY_WINDOW % tq == 0 and past % KEY_WINDOW == 0 and tq % 16 == 0 and n_keys >= KEY_WINDOW
    assert t % min(t, KEY_WINDOW) == 0
    front_pad = KEY_WINDOW - tq
    lp = front_pad + n_keys
    nq = t // tq
    nw_max = (n_keys + KEY_WINDOW - 1) // KEY_WINDOW
    rows = GQ * tq
    rb = min(rows, MAX_TILE_ROWS)
    stack_idx = IDX_HEADS * tq <= MAX_TILE_ROWS
    kvw = N_KV_HEADS * HEAD_DIM
    qw = N_HEADS * HEAD_DIM
    qiw = IDX_HEADS * IDX_DIM

    qrow = lambda bi, i: (bi * nq + i, 0)
    in_specs = [pl.BlockSpec((tq, qw), qrow),
                pl.BlockSpec((tq, qiw), lambda bi, i: (bi * nq + i, QI_OFF // qiw)),
                pl.BlockSpec((tq, LANE), qrow),
                pl.BlockSpec((t, kvw), lambda bi, i: (bi, K_OFF // kvw)),
                pl.BlockSpec((t, kvw), lambda bi, i: (bi, V_OFF // kvw)),
                pl.BlockSpec((t, LANE), lambda bi, i: (bi, KI_OFF // LANE))]
    args = [projb, projb, kiwi, projb, projb, projb]
    if past:
        in_specs += [pl.BlockSpec((None, past, kvw), lambda bi, i: (bi, 0, 0)),
                     pl.BlockSpec((None, past, kvw), lambda bi, i: (bi, 0, 0)),
                     pl.BlockSpec((None, past, IDX_DIM), lambda bi, i: (bi, 0, 0))]
        args += [past_k.reshape(b, past, kvw), past_v.reshape(b, past, kvw), past_ik]
    in_specs.append(pl.BlockSpec((N_KV_HEADS, rows, KEY_WINDOW), lambda bi, i: (0, 0, 0)))
    args.append(_near_bias(rel_bias, tq))

    kern = functools.partial(_attn_kernel, tq=tq, t=t, past=past, front_pad=front_pad, topk=topk)
    return pl.pallas_call(
        kern,
        out_shape=jax.ShapeDtypeStruct((b * t, qw), BF16),
        grid=(b, nq),
        in_specs=in_specs,
        out_specs=pl.BlockSpec((tq, qw), qrow),
        scratch_shapes=[pltpu.VMEM((N_KV_HEADS, lp, HEAD_DIM), BF16),
                        pltpu.VMEM((N_KV_HEADS, lp, LANE), BF16),
                        pltpu.VMEM((lp, IDX_DIM), BF16),
                        pltpu.VMEM((nw_max, tq, KEY_WINDOW), jnp.int32),
                        pltpu.VMEM((N_KV_HEADS, rows, HEAD_DIM), BF16),
                        pltpu.VMEM((IDX_HEADS * tq if stack_idx else 16, IDX_DIM), BF16),
                        pltpu.VMEM((IDX_HEADS * tq if stack_idx else 8, 1), F32),
                        pltpu.VMEM((rb, KEY_WINDOW), F32),
                        pltpu.VMEM((N_KV_HEADS * rows // rb, rb, KEY_WINDOW), F32),
                        pltpu.VMEM((N_KV_HEADS, rows, LANE), F32),
                        pltpu.VMEM((N_KV_HEADS, rows, LANE), F32)],
        compiler_params=_params("arbitrary", "arbitrary"),
        name="dsa_attention",
    )(*args)


def _ssd_kernel(z_ref, xlo_ref, xhi_ref, dt_ref, cst_ref, st0_ref, cw_ref, cb_ref, dtb_ref, alog_ref,
                dexp_ref, nw_ref, expand_ref, y_ref, ncv_ref, stout_ref,
                xpad_ref, act_ref, yd_ref, st_ref, *, q):
    c = pl.program_id(1)
    half = CONV_DIM // 2

    @pl.when(c == 0)
    def _():
        xpad_ref[8 - (D_CONV - 1):8, :] = cst_ref[...]
        for h in range(SSM_HEADS):
            g, jj = divmod(h, HEADS_PER_GROUP)
            st_ref[g, :, jj * SSM_HEAD_DIM:(jj + 1) * SSM_HEAD_DIM] = st0_ref[h].T

    cc = 512
    for c0 in range(0, CONV_DIM, cc):
        cs = slice(c0, c0 + cc)
        src = xlo_ref[:, c0:c0 + cc] if c0 < half else xhi_ref[:, c0 - half:c0 - half + cc]
        xpad_ref[8:8 + q, cs] = src
        conv = cb_ref[:, cs] + xpad_ref[5:5 + q, cs] * cw_ref[0:1, cs]
        conv = conv + xpad_ref[6:6 + q, cs] * cw_ref[1:2, cs]
        conv = conv + xpad_ref[7:7 + q, cs] * cw_ref[2:3, cs]
        conv = conv + xpad_ref[8:8 + q, cs] * cw_ref[3:4, cs]
        act_ref[:, cs] = conv * _sigmoid(conv)
        xpad_ref[0:8, cs] = xpad_ref[q:q + 8, cs]

    hi = lax.Precision.HIGHEST
    x_dt = dt_ref[:, 0:SSM_HEADS] + dtb_ref[...]
    dt = jnp.maximum(x_dt, 0.0) + jnp.log1p(jnp.exp(-jnp.abs(x_dt)))
    a = dt * (-jnp.exp(alog_ref[...]))
    rr = lax.broadcasted_iota(jnp.int32, (q, q), 0)
    cl = lax.broadcasted_iota(jnp.int32, (q, q), 1)
    causal = rr >= cl
    tri = jnp.where(causal, 1.0, 0.0)
    a_cs = jnp.dot(tri, a, precision=hi, preferred_element_type=F32)
    eye = jnp.where(lax.broadcasted_iota(jnp.int32, (SSM_HEADS, SSM_HEADS), 0)
                    == lax.broadcasted_iota(jnp.int32, (SSM_HEADS, SSM_HEADS), 1), 1.0, 0.0)
    a_cs_t = lax.dot_general(eye, a_cs, (((1,), (1,)), ((), ())), precision=hi,
                             preferred_element_type=F32)
    a_end = a_cs[q - 1:q, :]
    per_head = jnp.concatenate([dt, jnp.exp(a_cs), jnp.exp(a_end - a_cs)], axis=0)
    top = per_head.astype(BF16)
    rest = (per_head - top.astype(F32)).astype(BF16)
    spread = jnp.dot(jnp.concatenate([top, rest], axis=0), expand_ref[...],
                     preferred_element_type=F32)
    dt_x = spread[0:q] + spread[3 * q:4 * q]
    e_x = spread[q:2 * q] + spread[4 * q:5 * q]
    dte_x = spread[2 * q:3 * q] + spread[5 * q:6 * q]
    dec_x = e_x[q - 1:q, :]

    for g in range(SSM_GROUPS):
        gs = slice(g * GROUP_W, (g + 1) * GROUP_W)
        bg = act_ref[:, D_INNER + g * D_STATE:D_INNER + (g + 1) * D_STATE].astype(BF16)
        cg = act_ref[:, D_INNER + (SSM_GROUPS + g) * D_STATE:
                     D_INNER + (SSM_GROUPS + g + 1) * D_STATE].astype(BF16)
        cb = lax.dot_general(cg, bg, (((1,), (1,)), ((), ())), preferred_element_type=F32)
        xs_g = act_ref[:, gs]
        xd_g = xs_g * dt_x[:, gs]
        st_g = st_ref[g]
        y_off = jnp.dot(cg, st_g.astype(BF16), preferred_element_type=F32) * e_x[:, gs]
        for jj in range(HEADS_PER_GROUP):
            h = g * HEADS_PER_GROUP + jj
            seg = a_cs[:, h:h + 1] - a_cs_t[h:h + 1, :]
            decay = jnp.where(causal, jnp.exp(jnp.where(causal, seg, 0.0)), 0.0)
            hs = slice(jj * SSM_HEAD_DIM, (jj + 1) * SSM_HEAD_DIM)
            yd_ref[:, hs] = jnp.dot((cb * decay).astype(BF16), xd_g[:, hs].astype(BF16),
                                    preferred_element_type=F32)
        st_ref[g] = st_g * dec_x[:, gs] + lax.dot_general(
            bg, (xd_g * dte_x[:, gs]).astype(BF16), (((0,), (0,)), ((), ())),
            preferred_element_type=F32)
        y = yd_ref[...] + y_off + xs_g * dexp_ref[:, gs]
        zg = z_ref[:, gs]
        yg = y * (zg * _sigmoid(zg))
        ms = jnp.mean(yg * yg, axis=-1, keepdims=True)
        y_ref[:, gs] = (yg * lax.rsqrt(ms + EPS) * nw_ref[:, gs]).astype(y_ref.dtype)

    @pl.when(c == pl.num_programs(1) - 1)
    def _():
        ncv_ref[...] = xpad_ref[8 - (D_CONV - 1):8, :]
        for h in range(SSM_HEADS):
            g, jj = divmod(h, HEADS_PER_GROUP)
            stout_ref[h] = st_ref[g, :, jj * SSM_HEAD_DIM:(jj + 1) * SSM_HEAD_DIM].T


def _ssd_mixer(proj, conv_state, ssm_state, conv_w, conv_b, dt_bias, a_log, d_skip, norm_w, b, t):
    q = 128 if t % 128 == 0 else t
    assert q % 8 == 0 and t >= D_CONV - 1 and CONV_DIM == 2 * D_INNER
    nc = t // q
    expand = jnp.repeat(jnp.eye(SSM_HEADS, dtype=BF16), SSM_HEAD_DIM, axis=1)
    dexp = jnp.repeat(d_skip.astype(F32), SSM_HEAD_DIM)[None, :]
    row = lambda k: (lambda bi, c: (bi * nc + c, k))
    const2 = lambda bi, c: (0, 0)
    state_spec = pl.BlockSpec((None, SSM_HEADS, SSM_HEAD_DIM, D_STATE), lambda bi, c: (bi, 0, 0, 0))
    conv_spec = pl.BlockSpec((None, D_CONV - 1, CONV_DIM), lambda bi, c: (bi, 0, 0))
    y, new_conv, new_ssm = pl.pallas_call(
        functools.partial(_ssd_kernel, q=q),
        out_shape=[jax.ShapeDtypeStruct((b * t, D_INNER), BF16),
                   jax.ShapeDtypeStruct((b, D_CONV - 1, CONV_DIM), F32),
                   jax.ShapeDtypeStruct((b, SSM_HEADS, SSM_HEAD_DIM, D_STATE), F32)],
        grid=(b, nc),
        in_specs=[pl.BlockSpec((q, D_INNER), row(SSM_Z_OFF // D_INNER)),
                  pl.BlockSpec((q, D_INNER), row(SSM_XBC_OFF // D_INNER)),
                  pl.BlockSpec((q, D_INNER), row(SSM_XBC_OFF // D_INNER + 1)),
                  pl.BlockSpec((q, LANE), row(SSM_DT_OFF // LANE)),
                  conv_spec,
                  state_spec,
                  pl.BlockSpec((D_CONV, CONV_DIM), const2),
                  pl.BlockSpec((1, CONV_DIM), const2),
                  pl.BlockSpec((1, SSM_HEADS), const2),
                  pl.BlockSpec((1, SSM_HEADS), const2),
                  pl.BlockSpec((1, D_INNER), const2),
                  pl.BlockSpec((1, D_INNER), const2),
                  pl.BlockSpec((SSM_HEADS, D_INNER), const2)],
        out_specs=[pl.BlockSpec((q, D_INNER), row(0)), conv_spec, state_spec],
        scratch_shapes=[pltpu.VMEM((q + 8, CONV_DIM), F32),
                        pltpu.VMEM((q, CONV_DIM), F32),
                        pltpu.VMEM((q, GROUP_W), F32),
                        pltpu.VMEM((SSM_GROUPS, D_STATE, GROUP_W), F32)],
        compiler_params=_params("arbitrary", "arbitrary"),
        name="ssd_mixer",
    )(proj, proj, proj, proj, conv_state.astype(F32), ssm_state.astype(F32), conv_w.astype(F32),
      conv_b.astype(F32)[None, :], dt_bias.astype(F32)[None, :], a_log.astype(F32)[None, :], dexp,
      norm_w.astype(F32)[None, :], expand)
    return y, new_conv, new_ssm.astype(ssm_state.dtype)


def _tile_m(m):
    for tm in (1024, 512, 256, 128, 64, 32, 16, 8):
        if m % tm == 0:
            return tm
    raise ValueError(m)


def _pad_cols(w, n):
    return jnp.pad(w, ((0, 0), (0, n - w.shape[1])))


def _prep_weights(norm_mix, norm_ffn, norm_final, attn_w_in, attn_w_o, ssm_w_in, ssm_w_out,
                  mlp_w_up, mlp_w_down):
    col_scale = jnp.where(jnp.arange(ATTN_COLS) < K_OFF, ATTN_SCALE * LOG2E, 1.0).astype(F32)
    return dict(
        attn_in=_pad_cols(attn_w_in[0] * col_scale, _round_up(ATTN_COLS, LANE)).astype(BF16),
        attn_o=attn_w_o[0].astype(BF16),
        ssm_in=_pad_cols(ssm_w_in[0], _round_up(SSM_COLS, LANE)).astype(BF16),
        ssm_out=ssm_w_out[0].astype(BF16),
        up=[mlp_w_up[i].astype(BF16) for i in range(2)],
        down=[mlp_w_down[i].astype(BF16) for i in range(2)],
        g_mix=[norm_mix[i].astype(F32)[None, :] for i in range(2)],
        g_ffn=[norm_ffn[i].astype(F32)[None, :] for i in range(2)],
        g_final=norm_final.astype(F32)[None, :],
    )


def _trunk(x, past_k, past_v, past_ik, conv_st, ssm_st, rel_bias, wts, ssm_conv_w, ssm_conv_b,
           ssm_dt_bias, ssm_a_log, ssm_d, ssm_norm):
    b, t, d = x.shape
    m = b * t
    tm = _tile_m(m)
    x2 = x.reshape(m, d).astype(F32)

    projb, k_new, v_new, ki_new, kiwi = _attn_proj(x2, wts["g_mix"][0], wts["attn_in"], tm=min(tm, 512))
    ao = _dsa_mixer(projb, kiwi, past_k, past_v, past_ik, rel_bias, b, t)
    x2 = _matmul_resid(ao, wts["attn_o"], x2, tm=min(tm, 512))
    x2 = _mlp(x2, wts["g_ffn"][0], wts["up"][0], wts["down"][0], wts["g_final"],
              tm=tm, tf=512, final_norm=False)

    n_ssm = wts["ssm_in"].shape[1]
    proj = _norm_matmul(x2, wts["g_mix"][1], wts["ssm_in"], tm=tm, tn=n_ssm // 7)
    y, new_conv, new_ssm = _ssd_mixer(proj, conv_st, ssm_st, ssm_conv_w, ssm_conv_b, ssm_dt_bias,
                                      ssm_a_log, ssm_d, ssm_norm, b, t)
    x2 = _matmul_resid(y, wts["ssm_out"], x2, tm=min(tm, 512))
    x2 = _mlp(x2, wts["g_ffn"][1], wts["up"][1], wts["down"][1], wts["g_final"],
              tm=tm, tf=512, final_norm=True)

    dt = x.dtype
    return (x2.reshape(b, t, d).astype(dt),
            k_new.reshape(1, b, t, N_KV_HEADS, HEAD_DIM).astype(dt),
            v_new.reshape(1, b, t, N_KV_HEADS, HEAD_DIM).astype(dt),
            ki_new.reshape(1, b, t, IDX_DIM).astype(dt), new_conv[None].astype(dt), new_ssm[None])


def kernel(x_prompt, x_sample, cache_k, cache_v, cache_idx_k, state_conv, state_ssm, rel_bias, norm_mix, norm_ffn, norm_final, attn_w_in, attn_w_o, ssm_w_in, ssm_conv_w, ssm_conv_b, ssm_dt_bias, ssm_a_log, ssm_d, ssm_norm, ssm_w_out, mlp_w_up, mlp_w_down):
    wts = _prep_weights(norm_mix, norm_ffn, norm_final, attn_w_in, attn_w_o, ssm_w_in, ssm_w_out,
                        mlp_w_up, mlp_w_down)
    bp = x_prompt.shape[0]
    dtp = x_prompt.dtype
    empty_k = jnp.zeros((bp, 0, N_KV_HEADS, HEAD_DIM), dtp)
    empty_ik = jnp.zeros((bp, 0, IDX_DIM), dtp)
    zero_conv = jnp.zeros((bp, D_CONV - 1, CONV_DIM), dtp)
    zero_ssm = jnp.zeros((bp, SSM_HEADS, SSM_HEAD_DIM, D_STATE), dtp)
    args = (rel_bias, wts, ssm_conv_w[0], ssm_conv_b[0], ssm_dt_bias[0], ssm_a_log[0], ssm_d[0],
            ssm_norm[0])
    yp, kp, vp, ikp, cp, sp = _trunk(x_prompt, empty_k, empty_k, empty_ik, zero_conv, zero_ssm, *args)
    ys, ks, vs, iks, cs, ss = _trunk(x_sample, cache_k[0], cache_v[0], cache_idx_k[0],
                                     state_conv[0], state_ssm[0], *args)
    return (yp, ys, kp, vp, ikp, cp, sp, ks, vs, iks, cs, ss)
```

```python
import functools
import math

import jax
import jax.numpy as jnp
from jax import lax
from jax.experimental import pallas as pl
from jax.experimental.pallas import tpu as pltpu

F32 = jnp.float32
BF16 = jnp.bfloat16

D_MODEL = 1024
CHUNK = 64
CHUNK_SHIFT = 6
N_HEADS = 16
HEAD_DIM = 64
N_KV_HEADS = 4
GQ = N_HEADS // N_KV_HEADS
IDX_HEADS = 8
IDX_DIM = 64
TOPK_MAX = 256
IDX_SCALE = (IDX_HEADS * IDX_DIM) ** -0.5
ATTN_SCALE = HEAD_DIM ** -0.5
NUM_BUCKETS = 32
MAX_DISTANCE = 128
D_INNER = 2 * D_MODEL
SSM_HEAD_DIM = 64
SSM_HEADS = D_INNER // SSM_HEAD_DIM
SSM_GROUPS = 8
HEADS_PER_GROUP = SSM_HEADS // SSM_GROUPS
GROUP_W = HEADS_PER_GROUP * SSM_HEAD_DIM
D_STATE = 128
D_CONV = 4
CONV_DIM = D_INNER + 2 * SSM_GROUPS * D_STATE
D_FF = 4 * D_MODEL
EPS = 1e-6

Q_OFF = 0
K_OFF = N_HEADS * HEAD_DIM
V_OFF = K_OFF + N_KV_HEADS * HEAD_DIM
QI_OFF = V_OFF + N_KV_HEADS * HEAD_DIM
KI_OFF = QI_OFF + IDX_HEADS * IDX_DIM
WI_OFF = KI_OFF + IDX_DIM
ATTN_COLS = WI_OFF + IDX_HEADS

SSM_Z_OFF = 0
SSM_XBC_OFF = D_INNER
SSM_DT_OFF = D_INNER + CONV_DIM
SSM_COLS = SSM_DT_OFF + SSM_HEADS

LANE = 128
KEY_WINDOW = 512
MAX_TILE_ROWS = 128
LOG2E = 1.4426950408889634
VMEM_LIMIT = 56 * 1024 * 1024

NEG_BIG = -1e30
KEY_NEG_INF = -2139095041
INT_MIN = -2147483648


def _round_up(n, m):
    return (n + m - 1) // m * m


def _rms(x, g):
    ms = jnp.mean(x * x, axis=-1, keepdims=True)
    return x * lax.rsqrt(ms + EPS) * g


def _sigmoid(x):
    return 1.0 / (1.0 + jnp.exp(-x))


def _params(*sem):
    return pltpu.CompilerParams(dimension_semantics=sem, vmem_limit_bytes=VMEM_LIMIT)


def _norm_matmul_kernel(x_ref, g_ref, w_ref, o_ref, xn_ref):
    @pl.when(pl.program_id(1) == 0)
    def _():
        xn_ref[...] = _rms(x_ref[...], g_ref[...]).astype(BF16)

    o_ref[...] = jnp.dot(xn_ref[...], w_ref[...], preferred_element_type=F32)


def _norm_matmul(x, g, w, *, tm, tn):
    m, k = x.shape
    n = w.shape[1]
    return pl.pallas_call(
        _norm_matmul_kernel,
        out_shape=jax.ShapeDtypeStruct((m, n), F32),
        grid=(m // tm, n // tn),
        in_specs=[pl.BlockSpec((tm, k), lambda i, j: (i, 0)),
                  pl.BlockSpec((1, k), lambda i, j: (0, 0)),
                  pl.BlockSpec((k, tn), lambda i, j: (0, j))],
        out_specs=pl.BlockSpec((tm, tn), lambda i, j: (i, j)),
        scratch_shapes=[pltpu.VMEM((tm, k), BF16)],
        compiler_params=_params("parallel", "arbitrary"),
        name="norm_matmul",
    )(x, g, w)


def _attn_proj_kernel(x_ref, g_ref, w_ref, pb_ref, k_ref, v_ref, ki_ref, kiwi_ref):
    xn = _rms(x_ref[...], g_ref[...]).astype(BF16)
    acc = jnp.dot(xn, w_ref[...], preferred_element_type=F32)
    pb_ref[...] = acc.astype(BF16)
    k_ref[...] = acc[:, K_OFF:V_OFF]
    v_ref[...] = acc[:, V_OFF:QI_OFF]
    ki_ref[...] = acc[:, KI_OFF:WI_OFF]
    kiwi_ref[...] = acc[:, KI_OFF:KI_OFF + LANE]


def _attn_proj(x, g, w, *, tm):
    m, k = x.shape
    n = w.shape[1]
    kvw = N_KV_HEADS * HEAD_DIM
    row = lambda i: (i, 0)
    return pl.pallas_call(
        _attn_proj_kernel,
        out_shape=[jax.ShapeDtypeStruct((m, n), BF16),
                   jax.ShapeDtypeStruct((m, kvw), F32),
                   jax.ShapeDtypeStruct((m, kvw), F32),
                   jax.ShapeDtypeStruct((m, IDX_DIM), F32),
                   jax.ShapeDtypeStruct((m, LANE), F32)],
        grid=(m // tm,),
        in_specs=[pl.BlockSpec((tm, k), row),
                  pl.BlockSpec((1, k), lambda i: (0, 0)),
                  pl.BlockSpec((k, n), lambda i: (0, 0))],
        out_specs=[pl.BlockSpec((tm, n), row), pl.BlockSpec((tm, kvw), row), pl.BlockSpec((tm, kvw), row),
                   pl.BlockSpec((tm, IDX_DIM), row), pl.BlockSpec((tm, LANE), row)],
        compiler_params=_params("parallel"),
        name="attn_proj",
    )(x, g, w)


def _matmul_resid_kernel(a_ref, w_ref, r_ref, o_ref):
    o_ref[...] = r_ref[...] + jnp.dot(a_ref[...], w_ref[...], preferred_element_type=F32)


def _matmul_resid(a, w, resid, *, tm):
    m, k = a.shape
    n = w.shape[1]
    return pl.pallas_call(
        _matmul_resid_kernel,
        out_shape=jax.ShapeDtypeStruct((m, n), F32),
        grid=(m // tm,),
        in_specs=[pl.BlockSpec((tm, k), lambda i: (i, 0)),
                  pl.BlockSpec((k, n), lambda i: (0, 0)),
                  pl.BlockSpec((tm, n), lambda i: (i, 0))],
        out_specs=pl.BlockSpec((tm, n), lambda i: (i, 0)),
        compiler_params=_params("parallel"),
        name="matmul_resid",
    )(a, w, resid)


def _mlp_kernel(x_ref, g_ref, wu_ref, wd_ref, gf_ref, o_ref, xn_ref, acc_ref, *, final_norm):
    f = pl.program_id(1)

    @pl.when(f == 0)
    def _():
        xn_ref[...] = _rms(x_ref[...], g_ref[...]).astype(BF16)
        acc_ref[...] = jnp.zeros_like(acc_ref)

    a = jnp.dot(xn_ref[...], wu_ref[...], preferred_element_type=F32)
    a = jnp.maximum(a, 0.0)
    a = a * a
    acc_ref[...] += jnp.dot(a.astype(BF16), wd_ref[...], preferred_element_type=F32)

    @pl.when(f == pl.num_programs(1) - 1)
    def _():
        y = x_ref[...] + acc_ref[...]
        if final_norm:
            y = _rms(y, gf_ref[...])
        o_ref[...] = y


def _mlp(x, g, wu, wd, gf, *, tm, tf, final_norm):
    m, d = x.shape
    ff = wu.shape[1]
    return pl.pallas_call(
        functools.partial(_mlp_kernel, final_norm=final_norm),
        out_shape=jax.ShapeDtypeStruct((m, d), F32),
        grid=(m // tm, ff // tf),
        in_specs=[pl.BlockSpec((tm, d), lambda i, f: (i, 0)),
                  pl.BlockSpec((1, d), lambda i, f: (0, 0)),
                  pl.BlockSpec((d, tf), lambda i, f: (0, f)),
                  pl.BlockSpec((tf, d), lambda i, f: (f, 0)),
                  pl.BlockSpec((1, d), lambda i, f: (0, 0))],
        out_specs=pl.BlockSpec((tm, d), lambda i, f: (i, 0)),
        scratch_shapes=[pltpu.VMEM((tm, d), BF16), pltpu.VMEM((tm, d), F32)],
        compiler_params=_params("parallel", "arbitrary"),
        name="mlp",
    )(x, g, wu, wd, gf)


def _attn_kernel(*refs, tq, t, past, front_pad, topk):
    if past:
        (q_ref, qi_ref, kiwi_ref, kn_ref, vn_ref, kin_ref, pk_ref, pv_ref, pki_ref, nb_ref, o_ref,
         k_ref, v_ref, ki_ref, skey_ref, qs_ref, qis_ref, wcol_ref, madd_ref, s_ref, m_ref, acc_ref) = refs
    else:
        (q_ref, qi_ref, kiwi_ref, kn_ref, vn_ref, kin_ref, nb_ref, o_ref,
         k_ref, v_ref, ki_ref, skey_ref, qs_ref, qis_ref, wcol_ref, madd_ref, s_ref, m_ref, acc_ref) = refs
    w = KEY_WINDOW
    rows = GQ * tq
    rb = madd_ref.shape[0]
    keys_on_rows = skey_ref.shape[1] == w
    kax = 0 if keys_on_rows else 1
    qshape = (1, tq) if keys_on_rows else (tq, 1)
    i = pl.program_id(1)
    q0 = past + i * tq
    lv = q0 + tq
    nw = jnp.right_shift(lv + (w - 1), 9)
    qchunk = jnp.right_shift(q0 + lax.broadcasted_iota(jnp.int32, qshape, 1 - kax), CHUNK_SHIFT)
    kidx = lax.broadcasted_iota(jnp.int32, (w, 1) if keys_on_rows else (1, w), kax)
    nt = (((1,), (1,)), ((), ()))

    def win_row(j):
        return pl.multiple_of(lv + front_pad - w * (j + 1), tq)

    @pl.when(i == 0)
    def _():
        ones_col = jnp.where(lax.broadcasted_iota(jnp.int32, (1, LANE - HEAD_DIM), 1) == 0, 1.0, 0.0)

        def put(r0, n, kc, vc, kic):
            ones = jnp.broadcast_to(ones_col, (n, LANE - HEAD_DIM)).astype(BF16)
            for g in range(N_KV_HEADS):
                gs = slice(g * HEAD_DIM, (g + 1) * HEAD_DIM)
                k_ref[g, r0:r0 + n, :] = kc[:, gs]
                v_ref[g, r0:r0 + n, :] = jnp.concatenate([vc[:, gs], ones], axis=1)
            ki_ref[r0:r0 + n, :] = kic

        kvw = N_KV_HEADS * HEAD_DIM
        if front_pad:
            put(0, front_pad, jnp.zeros((front_pad, kvw), BF16), jnp.zeros((front_pad, kvw), BF16),
                jnp.zeros((front_pad, IDX_DIM), BF16))
        step = min(t, w)
        for c0 in range(0, past, w):
            put(front_pad + c0, w, pk_ref[c0:c0 + w, :].astype(BF16), pv_ref[c0:c0 + w, :].astype(BF16),
                pki_ref[c0:c0 + w, :].astype(BF16))
        for c0 in range(0, t, step):
            put(front_pad + past + c0, step, kn_ref[c0:c0 + step, :], vn_ref[c0:c0 + step, :],
                kin_ref[c0:c0 + step, 0:IDX_DIM])

    for h in range(N_HEADS):
        g, jj = divmod(h, GQ)
        qs_ref[g, jj * tq:(jj + 1) * tq, :] = q_ref[:, h * HEAD_DIM:(h + 1) * HEAD_DIM]

    if keys_on_rows:
        wi_t = kiwi_ref[...].T[IDX_DIM:IDX_DIM + IDX_HEADS, :]
        for hp in range(IDX_HEADS // 2):
            for u in range(2):
                h = 2 * hp + u
                qis_ref[hp, u * tq:(u + 1) * tq, :] = qi_ref[:, h * IDX_DIM:(h + 1) * IDX_DIM]
    else:
        wi = kiwi_ref[:, IDX_DIM:IDX_DIM + IDX_HEADS]
        for h in range(IDX_HEADS):
            qis_ref[0, h * tq:(h + 1) * tq, :] = qi_ref[:, h * IDX_DIM:(h + 1) * IDX_DIM]
            wcol_ref[h * tq:(h + 1) * tq, :] = wi[:, h:h + 1]

    def score_body(j, carry):
        kib = ki_ref[pl.ds(win_row(j), w), :]
        if keys_on_rows:
            sc = jnp.zeros((w, tq), F32)
            for hp in range(IDX_HEADS // 2):
                s2 = jnp.maximum(lax.dot_general(kib, qis_ref[hp], nt, preferred_element_type=F32), 0.0)
                sc = sc + s2[:, 0:tq] * wi_t[2 * hp:2 * hp + 1, :]
                sc = sc + s2[:, tq:2 * tq] * wi_t[2 * hp + 1:2 * hp + 2, :]
        else:
            s8 = jnp.maximum(lax.dot_general(qis_ref[0], kib, nt, preferred_element_type=F32), 0.0)
            s8 = s8 * wcol_ref[...]
            sc = s8[0:tq]
            for h in range(1, IDX_HEADS):
                sc = sc + s8[h * tq:(h + 1) * tq]
        sc = sc * IDX_SCALE
        kpos = (lv - w * (j + 1)) + kidx
        vis = (kpos >= 0) & (jnp.right_shift(kpos, CHUNK_SHIFT) <= qchunk)
        bits = pltpu.bitcast(sc, jnp.int32)
        key = bits ^ (jnp.right_shift(bits, 31) & 0x7FFFFFFF)
        key = jnp.where(sc == 0.0, 0, key)
        skey_ref[j] = jnp.where(vis, key, KEY_NEG_INF)
        return carry

    lax.fori_loop(0, nw, score_body, 0)

    if keys_on_rows:
        part = (4 * 8, tq)

        def fold(c):
            return jnp.sum(c.reshape(w // part[0], part[0], tq), axis=0)
    else:
        part = (tq, LANE)

        def fold(c):
            return (c[:, 0:LANE] + c[:, LANE:2 * LANE]) + (c[:, 2 * LANE:3 * LANE] + c[:, 3 * LANE:4 * LANE])

    def total(acc):
        return jnp.sum(acc, axis=kax, keepdims=True)

    def count_ge(cand):
        def body(j, acc):
            return acc + fold(jnp.where(skey_ref[j] >= cand, 1.0, 0.0))
        return total(lax.fori_loop(0, nw, body, jnp.zeros(part, F32)))

    def search_body(it, prefix):
        cand = prefix + lax.shift_left(jnp.int32(1), 31 - it)
        return jnp.where(count_ge(cand) >= float(topk), cand, prefix)

    tstar = lax.fori_loop(0, 32, search_body, jnp.full(qshape, INT_MIN, jnp.int32))

    def tie_count_body(j, carry):
        ag, ae = carry
        kt = skey_ref[j]
        return (ag + fold(jnp.where(kt > tstar, 1.0, 0.0)),
                ae + fold(jnp.where(kt == tstar, 1.0, 0.0)))

    ag, ae = lax.fori_loop(0, nw, tie_count_body, (jnp.zeros(part, F32), jnp.zeros(part, F32)))
    need = float(topk) - total(ag)
    n_eq = total(ae)
    bad = (n_eq > need) & (tstar > KEY_NEG_INF)
    n_bad = jnp.sum(jnp.where(bad, 1.0, 0.0))

    @pl.when(n_bad > 0.0)
    def _():
        r_i = lax.broadcasted_iota(jnp.int32, (w, w), 0)
        c_i = lax.broadcasted_iota(jnp.int32, (w, w), 1)
        earlier = jnp.where(c_i < r_i if keys_on_rows else r_i < c_i, 1.0, 0.0).astype(BF16)

        def tie_body(t, seen):
            j = nw - 1 - t
            kt = skey_ref[j]
            eq = kt == tstar
            eqf = jnp.where(eq, 1.0, 0.0)
            if keys_on_rows:
                inwin = jnp.dot(earlier, eqf.astype(BF16), preferred_element_type=F32)
            else:
                inwin = jnp.dot(eqf.astype(BF16), earlier, preferred_element_type=F32)
            skey_ref[j] = jnp.where(eq & bad & (seen + inwin >= need), KEY_NEG_INF, kt)
            return seen + total(eqf)

        lax.fori_loop(0, nw, tie_body, jnp.zeros(qshape, F32))

    teff = jnp.maximum(tstar, KEY_NEG_INF + 1)

    m_ref[...] = jnp.full(m_ref.shape, NEG_BIG, F32)
    acc_ref[...] = jnp.zeros(acc_ref.shape, F32)

    def attend(j, near):
        row = win_row(j)
        tile = jnp.where(skey_ref[j] >= teff, 0.0, NEG_BIG)
        if keys_on_rows:
            tile = tile.T
        for r in range(rb // tq):
            madd_ref[r * tq:(r + 1) * tq, :] = tile
        tiles = [(g, r0) for g in range(N_KV_HEADS) for r0 in range(0, rows, rb)]

        for t, (g, r0) in enumerate(tiles):
            kb = k_ref[g, pl.ds(row, w), :]
            s = lax.dot_general(qs_ref[g, r0:r0 + rb, :], kb, nt, preferred_element_type=F32)
            s = s + madd_ref[...]
            if near:
                s = s + nb_ref[g, r0:r0 + rb, :]
            s_ref[t] = s
        for t, (g, r0) in enumerate(tiles):
            rs = slice(r0, r0 + rb)
            s = s_ref[t]
            m_old = m_ref[g, rs, :]
            m_new = jnp.maximum(m_old, jnp.max(s, axis=1, keepdims=True))
            p = jnp.exp2(s - jnp.concatenate([m_new] * (w // LANE), axis=1))
            vb = v_ref[g, pl.ds(row, w), :]
            acc_ref[g, rs, :] = (jnp.exp2(m_old - m_new) * acc_ref[g, rs, :]
                                 + jnp.dot(p.astype(BF16), vb, preferred_element_type=F32))
            m_ref[g, rs, :] = m_new

    attend(0, True)

    def far_body(j, carry):
        attend(j, False)
        return carry

    lax.fori_loop(1, nw, far_body, 0)

    for h in range(N_HEADS):
        g, jj = divmod(h, GQ)
        a = acc_ref[g, jj * tq:(jj + 1) * tq, :]
        o_ref[:, h * HEAD_DIM:(h + 1) * HEAD_DIM] = (
            a[:, 0:HEAD_DIM] / a[:, HEAD_DIM:HEAD_DIM + 1]).astype(o_ref.dtype)


def _rel_bucket(rel):
    half = NUM_BUCKETS // 2
    max_exact = half // 2
    base = jnp.where(rel > 0, half, 0)
    n = jnp.abs(rel)
    nf = jnp.maximum(n, 1).astype(jnp.float32)
    large = max_exact + (jnp.log(nf / max_exact) / math.log(MAX_DISTANCE / max_exact)
                         * (half - max_exact)).astype(jnp.int32)
    large = jnp.minimum(large, half - 1)
    return base + jnp.where(n < max_exact, n, large)


def _near_bias(rel_bias, tq):
    r = jnp.arange(tq, dtype=jnp.int32)[:, None]
    c = jnp.arange(KEY_WINDOW, dtype=jnp.int32)[None, :]
    rel = (tq - KEY_WINDOW + c) - r
    tab = rel_bias.astype(F32)
    far = _rel_bucket(jnp.full((1,), -(1 << 20), jnp.int32))
    far_row = jnp.dot(jax.nn.one_hot(far, NUM_BUCKETS, dtype=F32), tab, precision=lax.Precision.HIGHEST)
    pick = jax.nn.one_hot(_rel_bucket(rel), NUM_BUCKETS, dtype=F32)
    nb = jnp.einsum('qcb,bh->hqc', pick, (tab - far_row) * LOG2E, precision=lax.Precision.HIGHEST)
    return nb.reshape(N_KV_HEADS, GQ * tq, KEY_WINDOW)


def _dsa_mixer(projb, kiwi, past_k, past_v, past_ik, rel_bias, b, t):
    past = past_k.shape[1]
    n_keys = past + t
    topk = min(TOPK_MAX, n_keys // 4)
    tq = 128 if t % 128 == 0 else t
    assert KEY_WINDOW % tq == 0 and past % KEY_WINDOW == 0 and tq % 16 == 0 and n_keys >= KEY_WINDOW
    assert t % min(t, KEY_WINDOW) == 0
    front_pad = KEY_WINDOW - tq
    lp = front_pad + n_keys
    nq = t // tq
    nw_max = (n_keys + KEY_WINDOW - 1) // KEY_WINDOW
    rows = GQ * tq
    rb = min(rows, MAX_TILE_ROWS)
    keys_on_rows = tq % LANE == 0
    assert keys_on_rows or IDX_HEADS * tq <= MAX_TILE_ROWS
    kvw = N_KV_HEADS * HEAD_DIM
    qw = N_HEADS * HEAD_DIM
    qiw = IDX_HEADS * IDX_DIM

    qrow = lambda bi, i: (bi * nq + i, 0)
    in_specs = [pl.BlockSpec((tq, qw), qrow),
                pl.BlockSpec((tq, qiw), lambda bi, i: (bi * nq + i, QI_OFF // qiw)),
                pl.BlockSpec((tq, LANE), qrow),
                pl.BlockSpec((t, kvw), lambda bi, i: (bi, K_OFF // kvw)),
                pl.BlockSpec((t, kvw), lambda bi, i: (bi, V_OFF // kvw)),
                pl.BlockSpec((t, LANE), lambda bi, i: (bi, KI_OFF // LANE))]
    args = [projb, projb, kiwi, projb, projb, projb]
    if past:
        in_specs += [pl.BlockSpec((None, past, kvw), lambda bi, i: (bi, 0, 0)),
                     pl.BlockSpec((None, past, kvw), lambda bi, i: (bi, 0, 0)),
                     pl.BlockSpec((None, past, IDX_DIM), lambda bi, i: (bi, 0, 0))]
        args += [past_k.reshape(b, past, kvw), past_v.reshape(b, past, kvw), past_ik]
    in_specs.append(pl.BlockSpec((N_KV_HEADS, rows, KEY_WINDOW), lambda bi, i: (0, 0, 0)))
    args.append(_near_bias(rel_bias, tq))

    kern = functools.partial(_attn_kernel, tq=tq, t=t, past=past, front_pad=front_pad, topk=topk)
    return pl.pallas_call(
        kern,
        out_shape=jax.ShapeDtypeStruct((b * t, qw), BF16),
        grid=(b, nq),
        in_specs=in_specs,
        out_specs=pl.BlockSpec((tq, qw), qrow),
        scratch_shapes=[pltpu.VMEM((N_KV_HEADS, lp, HEAD_DIM), BF16),
                        pltpu.VMEM((N_KV_HEADS, lp, LANE), BF16),
                        pltpu.VMEM((lp, IDX_DIM), BF16),
                        pltpu.VMEM((nw_max, KEY_WINDOW, tq) if keys_on_rows
                                   else (nw_max, tq, KEY_WINDOW), jnp.int32),
                        pltpu.VMEM((N_KV_HEADS, rows, HEAD_DIM), BF16),
                        pltpu.VMEM((IDX_HEADS // 2, 2 * tq, IDX_DIM) if keys_on_rows
                                   else (1, IDX_HEADS * tq, IDX_DIM), BF16),
                        pltpu.VMEM((8, 1) if keys_on_rows else (IDX_HEADS * tq, 1), F32),
                        pltpu.VMEM((rb, KEY_WINDOW), F32),
                        pltpu.VMEM((N_KV_HEADS * rows // rb, rb, KEY_WINDOW), F32),
                        pltpu.VMEM((N_KV_HEADS, rows, LANE), F32),
                        pltpu.VMEM((N_KV_HEADS, rows, LANE), F32)],
        compiler_params=_params("arbitrary", "arbitrary"),
        name="dsa_attention",
    )(*args)


def _ssd_kernel(z_ref, xlo_ref, xhi_ref, dt_ref, cst_ref, st0_ref, cw_ref, cb_ref, dtb_ref, alog_ref,
                dexp_ref, nw_ref, expand_ref, y_ref, ncv_ref, stout_ref,
                xpad_ref, act_ref, yd_ref, st_ref, *, q):
    c = pl.program_id(1)
    half = CONV_DIM // 2

    @pl.when(c == 0)
    def _():
        xpad_ref[8 - (D_CONV - 1):8, :] = cst_ref[...]
        for h in range(SSM_HEADS):
            g, jj = divmod(h, HEADS_PER_GROUP)
            st_ref[g, :, jj * SSM_HEAD_DIM:(jj + 1) * SSM_HEAD_DIM] = st0_ref[h].T

    cc = 512
    for c0 in range(0, CONV_DIM, cc):
        cs = slice(c0, c0 + cc)
        src = xlo_ref[:, c0:c0 + cc] if c0 < half else xhi_ref[:, c0 - half:c0 - half + cc]
        xpad_ref[8:8 + q, cs] = src
        conv = cb_ref[:, cs] + xpad_ref[5:5 + q, cs] * cw_ref[0:1, cs]
        conv = conv + xpad_ref[6:6 + q, cs] * cw_ref[1:2, cs]
        conv = conv + xpad_ref[7:7 + q, cs] * cw_ref[2:3, cs]
        conv = conv + xpad_ref[8:8 + q, cs] * cw_ref[3:4, cs]
        act_ref[:, cs] = conv * _sigmoid(conv)
        xpad_ref[0:8, cs] = xpad_ref[q:q + 8, cs]

    hi = lax.Precision.HIGHEST
    x_dt = dt_ref[:, 0:SSM_HEADS] + dtb_ref[...]
    dt = jnp.maximum(x_dt, 0.0) + jnp.log1p(jnp.exp(-jnp.abs(x_dt)))
    a = dt * (-jnp.exp(alog_ref[...]))
    rr = lax.broadcasted_iota(jnp.int32, (q, q), 0)
    cl = lax.broadcasted_iota(jnp.int32, (q, q), 1)
    causal = rr >= cl
    tri = jnp.where(causal, 1.0, 0.0)
    a_cs = jnp.dot(tri, a, precision=hi, preferred_element_type=F32)
    eye = jnp.where(lax.broadcasted_iota(jnp.int32, (SSM_HEADS, SSM_HEADS), 0)
                    == lax.broadcasted_iota(jnp.int32, (SSM_HEADS, SSM_HEADS), 1), 1.0, 0.0)
    a_cs_t = lax.dot_general(eye, a_cs, (((1,), (1,)), ((), ())), precision=hi,
                             preferred_element_type=F32)
    a_end = a_cs[q - 1:q, :]
    per_head = jnp.concatenate([dt, jnp.exp(a_cs), jnp.exp(a_end - a_cs)], axis=0)
    top = per_head.astype(BF16)
    rest = (per_head - top.astype(F32)).astype(BF16)
    spread = jnp.dot(jnp.concatenate([top, rest], axis=0), expand_ref[...],
                     preferred_element_type=F32)
    dt_x = spread[0:q] + spread[3 * q:4 * q]
    e_x = spread[q:2 * q] + spread[4 * q:5 * q]
    dte_x = spread[2 * q:3 * q] + spread[5 * q:6 * q]
    dec_x = e_x[q - 1:q, :]

    for g in range(SSM_GROUPS):
        gs = slice(g * GROUP_W, (g + 1) * GROUP_W)
        bg = act_ref[:, D_INNER + g * D_STATE:D_INNER + (g + 1) * D_STATE].astype(BF16)
        cg = act_ref[:, D_INNER + (SSM_GROUPS + g) * D_STATE:
                     D_INNER + (SSM_GROUPS + g + 1) * D_STATE].astype(BF16)
        cb = lax.dot_general(cg, bg, (((1,), (1,)), ((), ())), preferred_element_type=F32)
        xs_g = act_ref[:, gs]
        xd_g = xs_g * dt_x[:, gs]
        st_g = st_ref[g]
        y_off = jnp.dot(cg, st_g.astype(BF16), preferred_element_type=F32) * e_x[:, gs]
        for jj in range(HEADS_PER_GROUP):
            h = g * HEADS_PER_GROUP + jj
            seg = a_cs[:, h:h + 1] - a_cs_t[h:h + 1, :]
            decay = jnp.where(causal, jnp.exp(jnp.where(causal, seg, 0.0)), 0.0)
            hs = slice(jj * SSM_HEAD_DIM, (jj + 1) * SSM_HEAD_DIM)
            yd_ref[:, hs] = jnp.dot((cb * decay).astype(BF16), xd_g[:, hs].astype(BF16),
                                    preferred_element_type=F32)
        st_ref[g] = st_g * dec_x[:, gs] + lax.dot_general(
            bg, (xd_g * dte_x[:, gs]).astype(BF16), (((0,), (0,)), ((), ())),
            preferred_element_type=F32)
        y = yd_ref[...] + y_off + xs_g * dexp_ref[:, gs]
        zg = z_ref[:, gs]
        yg = y * (zg * _sigmoid(zg))
        ms = jnp.mean(yg * yg, axis=-1, keepdims=True)
        y_ref[:, gs] = (yg * lax.rsqrt(ms + EPS) * nw_ref[:, gs]).astype(y_ref.dtype)

    @pl.when(c == pl.num_programs(1) - 1)
    def _():
        ncv_ref[...] = xpad_ref[8 - (D_CONV - 1):8, :]
        for h in range(SSM_HEADS):
            g, jj = divmod(h, HEADS_PER_GROUP)
            stout_ref[h] = st_ref[g, :, jj * SSM_HEAD_DIM:(jj + 1) * SSM_HEAD_DIM].T


def _ssd_mixer(proj, conv_state, ssm_state, conv_w, conv_b, dt_bias, a_log, d_skip, norm_w, b, t):
    q = 128 if t % 128 == 0 else t
    assert q % 8 == 0 and t >= D_CONV - 1 and CONV_DIM == 2 * D_INNER
    nc = t // q
    expand = jnp.repeat(jnp.eye(SSM_HEADS, dtype=BF16), SSM_HEAD_DIM, axis=1)
    dexp = jnp.repeat(d_skip.astype(F32), SSM_HEAD_DIM)[None, :]
    row = lambda k: (lambda bi, c: (bi * nc + c, k))
    const2 = lambda bi, c: (0, 0)
    state_spec = pl.BlockSpec((None, SSM_HEADS, SSM_HEAD_DIM, D_STATE), lambda bi, c: (bi, 0, 0, 0))
    conv_spec = pl.BlockSpec((None, D_CONV - 1, CONV_DIM), lambda bi, c: (bi, 0, 0))
    y, new_conv, new_ssm = pl.pallas_call(
        functools.partial(_ssd_kernel, q=q),
        out_shape=[jax.ShapeDtypeStruct((b * t, D_INNER), BF16),
                   jax.ShapeDtypeStruct((b, D_CONV - 1, CONV_DIM), F32),
                   jax.ShapeDtypeStruct((b, SSM_HEADS, SSM_HEAD_DIM, D_STATE), F32)],
        grid=(b, nc),
        in_specs=[pl.BlockSpec((q, D_INNER), row(SSM_Z_OFF // D_INNER)),
                  pl.BlockSpec((q, D_INNER), row(SSM_XBC_OFF // D_INNER)),
                  pl.BlockSpec((q, D_INNER), row(SSM_XBC_OFF // D_INNER + 1)),
                  pl.BlockSpec((q, LANE), row(SSM_DT_OFF // LANE)),
                  conv_spec,
                  state_spec,
                  pl.BlockSpec((D_CONV, CONV_DIM), const2),
                  pl.BlockSpec((1, CONV_DIM), const2),
                  pl.BlockSpec((1, SSM_HEADS), const2),
                  pl.BlockSpec((1, SSM_HEADS), const2),
                  pl.BlockSpec((1, D_INNER), const2),
                  pl.BlockSpec((1, D_INNER), const2),
                  pl.BlockSpec((SSM_HEADS, D_INNER), const2)],
        out_specs=[pl.BlockSpec((q, D_INNER), row(0)), conv_spec, state_spec],
        scratch_shapes=[pltpu.VMEM((q + 8, CONV_DIM), F32),
                        pltpu.VMEM((q, CONV_DIM), F32),
                        pltpu.VMEM((q, GROUP_W), F32),
                        pltpu.VMEM((SSM_GROUPS, D_STATE, GROUP_W), F32)],
        compiler_params=_params("arbitrary", "arbitrary"),
        name="ssd_mixer",
    )(proj, proj, proj, proj, conv_state.astype(F32), ssm_state.astype(F32), conv_w.astype(F32),
      conv_b.astype(F32)[None, :], dt_bias.astype(F32)[None, :], a_log.astype(F32)[None, :], dexp,
      norm_w.astype(F32)[None, :], expand)
    return y, new_conv, new_ssm.astype(ssm_state.dtype)


def _tile_m(m):
    for tm in (1024, 512, 256, 128, 64, 32, 16, 8):
        if m % tm == 0:
            return tm
    raise ValueError(m)


def _pad_cols(w, n):
    return jnp.pad(w, ((0, 0), (0, n - w.shape[1])))


def _prep_weights(norm_mix, norm_ffn, norm_final, attn_w_in, attn_w_o, ssm_w_in, ssm_w_out,
                  mlp_w_up, mlp_w_down):
    col_scale = jnp.where(jnp.arange(ATTN_COLS) < K_OFF, ATTN_SCALE * LOG2E, 1.0).astype(F32)
    return dict(
        attn_in=_pad_cols(attn_w_in[0] * col_scale, _round_up(ATTN_COLS, LANE)).astype(BF16),
        attn_o=attn_w_o[0].astype(BF16),
        ssm_in=_pad_cols(ssm_w_in[0], _round_up(SSM_COLS, LANE)).astype(BF16),
        ssm_out=ssm_w_out[0].astype(BF16),
        up=[mlp_w_up[i].astype(BF16) for i in range(2)],
        down=[mlp_w_down[i].astype(BF16) for i in range(2)],
        g_mix=[norm_mix[i].astype(F32)[None, :] for i in range(2)],
        g_ffn=[norm_ffn[i].astype(F32)[None, :] for i in range(2)],
        g_final=norm_final.astype(F32)[None, :],
    )


def _trunk(x, past_k, past_v, past_ik, conv_st, ssm_st, rel_bias, wts, ssm_conv_w, ssm_conv_b,
           ssm_dt_bias, ssm_a_log, ssm_d, ssm_norm):
    b, t, d = x.shape
    m = b * t
    tm = _tile_m(m)
    x2 = x.reshape(m, d).astype(F32)

    projb, k_new, v_new, ki_new, kiwi = _attn_proj(x2, wts["g_mix"][0], wts["attn_in"], tm=min(tm, 512))
    ao = _dsa_mixer(projb, kiwi, past_k, past_v, past_ik, rel_bias, b, t)
    x2 = _matmul_resid(ao, wts["attn_o"], x2, tm=min(tm, 512))
    x2 = _mlp(x2, wts["g_ffn"][0], wts["up"][0], wts["down"][0], wts["g_final"],
              tm=tm, tf=512, final_norm=False)

    n_ssm = wts["ssm_in"].shape[1]
    proj = _norm_matmul(x2, wts["g_mix"][1], wts["ssm_in"], tm=tm, tn=n_ssm // 7)
    y, new_conv, new_ssm = _ssd_mixer(proj, conv_st, ssm_st, ssm_conv_w, ssm_conv_b, ssm_dt_bias,
                                      ssm_a_log, ssm_d, ssm_norm, b, t)
    x2 = _matmul_resid(y, wts["ssm_out"], x2, tm=min(tm, 512))
    x2 = _mlp(x2, wts["g_ffn"][1], wts["up"][1], wts["down"][1], wts["g_final"],
              tm=tm, tf=512, final_norm=True)

    dt = x.dtype
    return (x2.reshape(b, t, d).astype(dt),
            k_new.reshape(1, b, t, N_KV_HEADS, HEAD_DIM).astype(dt),
            v_new.reshape(1, b, t, N_KV_HEADS, HEAD_DIM).astype(dt),
            ki_new.reshape(1, b, t, IDX_DIM).astype(dt), new_conv[None].astype(dt), new_ssm[None])


def kernel(x_prompt, x_sample, cache_k, cache_v, cache_idx_k, state_conv, state_ssm, rel_bias, norm_mix, norm_ffn, norm_final, attn_w_in, attn_w_o, ssm_w_in, ssm_conv_w, ssm_conv_b, ssm_dt_bias, ssm_a_log, ssm_d, ssm_norm, ssm_w_out, mlp_w_up, mlp_w_down):
    wts = _prep_weights(norm_mix, norm_ffn, norm_final, attn_w_in, attn_w_o, ssm_w_in, ssm_w_out,
                        mlp_w_up, mlp_w_down)
    bp = x_prompt.shape[0]
    dtp = x_prompt.dtype
    empty_k = jnp.zeros((bp, 0, N_KV_HEADS, HEAD_DIM), dtp)
    empty_ik = jnp.zeros((bp, 0, IDX_DIM), dtp)
    zero_conv = jnp.zeros((bp, D_CONV - 1, CONV_DIM), dtp)
    zero_ssm = jnp.zeros((bp, SSM_HEADS, SSM_HEAD_DIM, D_STATE), dtp)
    args = (rel_bias, wts, ssm_conv_w[0], ssm_conv_b[0], ssm_dt_bias[0], ssm_a_log[0], ssm_d[0],
            ssm_norm[0])
    yp, kp, vp, ikp, cp, sp = _trunk(x_prompt, empty_k, empty_k, empty_ik, zero_conv, zero_ssm, *args)
    ys, ks, vs, iks, cs, ss = _trunk(x_sample, cache_k[0], cache_v[0], cache_idx_k[0],
                                     state_conv[0], state_ssm[0], *args)
    return (yp, ys, kp, vp, ikp, cp, sp, ks, vs, iks, cs, ss)
```

```python
import functools
import math

import jax
import jax.numpy as jnp
from jax import lax
from jax.experimental import pallas as pl
from jax.experimental.pallas import tpu as pltpu

F32 = jnp.float32
BF16 = jnp.bfloat16

D_MODEL = 1024
CHUNK = 64
CHUNK_SHIFT = 6
N_HEADS = 16
HEAD_DIM = 64
N_KV_HEADS = 4
GQ = N_HEADS // N_KV_HEADS
IDX_HEADS = 8
IDX_DIM = 64
TOPK_MAX = 256
IDX_SCALE = (IDX_HEADS * IDX_DIM) ** -0.5
ATTN_SCALE = HEAD_DIM ** -0.5
NUM_BUCKETS = 32
MAX_DISTANCE = 128
D_INNER = 2 * D_MODEL
SSM_HEAD_DIM = 64
SSM_HEADS = D_INNER // SSM_HEAD_DIM
SSM_GROUPS = 8
HEADS_PER_GROUP = SSM_HEADS // SSM_GROUPS
GROUP_W = HEADS_PER_GROUP * SSM_HEAD_DIM
D_STATE = 128
D_CONV = 4
CONV_DIM = D_INNER + 2 * SSM_GROUPS * D_STATE
D_FF = 4 * D_MODEL
EPS = 1e-6

Q_OFF = 0
K_OFF = N_HEADS * HEAD_DIM
V_OFF = K_OFF + N_KV_HEADS * HEAD_DIM
QI_OFF = V_OFF + N_KV_HEADS * HEAD_DIM
KI_OFF = QI_OFF + IDX_HEADS * IDX_DIM
WI_OFF = KI_OFF + IDX_DIM
ATTN_COLS = WI_OFF + IDX_HEADS

SSM_Z_OFF = 0
SSM_XBC_OFF = D_INNER
SSM_DT_OFF = D_INNER + CONV_DIM
SSM_COLS = SSM_DT_OFF + SSM_HEADS

LANE = 128
KEY_WINDOW = 512
MAX_TILE_ROWS = 128
LOG2E = 1.4426950408889634
VMEM_LIMIT = 56 * 1024 * 1024

NEG_BIG = -1e30
KEY_NEG_INF = -2139095041
INT_MIN = -2147483648


def _round_up(n, m):
    return (n + m - 1) // m * m


def _rms(x, g):
    ms = jnp.mean(x * x, axis=-1, keepdims=True)
    return x * lax.rsqrt(ms + EPS) * g


def _sigmoid(x):
    return 1.0 / (1.0 + jnp.exp(-x))


def _params(*sem):
    return pltpu.CompilerParams(dimension_semantics=sem, vmem_limit_bytes=VMEM_LIMIT)


def _norm_matmul_kernel(x_ref, g_ref, w_ref, o_ref, xn_ref):
    @pl.when(pl.program_id(1) == 0)
    def _():
        xn_ref[...] = _rms(x_ref[...], g_ref[...]).astype(BF16)

    o_ref[...] = jnp.dot(xn_ref[...], w_ref[...], preferred_element_type=F32)


def _norm_matmul(x, g, w, *, tm, tn):
    m, k = x.shape
    n = w.shape[1]
    return pl.pallas_call(
        _norm_matmul_kernel,
        out_shape=jax.ShapeDtypeStruct((m, n), F32),
        grid=(m // tm, n // tn),
        in_specs=[pl.BlockSpec((tm, k), lambda i, j: (i, 0)),
                  pl.BlockSpec((1, k), lambda i, j: (0, 0)),
                  pl.BlockSpec((k, tn), lambda i, j: (0, j))],
        out_specs=pl.BlockSpec((tm, tn), lambda i, j: (i, j)),
        scratch_shapes=[pltpu.VMEM((tm, k), BF16)],
        compiler_params=_params("parallel", "arbitrary"),
        name="norm_matmul",
    )(x, g, w)


def _attn_proj_kernel(x_ref, g_ref, w_ref, pb_ref, k_ref, v_ref, ki_ref, kiwi_ref):
    xn = _rms(x_ref[...], g_ref[...]).astype(BF16)
    acc = jnp.dot(xn, w_ref[...], preferred_element_type=F32)
    pb_ref[...] = acc.astype(BF16)
    k_ref[...] = acc[:, K_OFF:V_OFF]
    v_ref[...] = acc[:, V_OFF:QI_OFF]
    ki_ref[...] = acc[:, KI_OFF:WI_OFF]
    kiwi_ref[...] = acc[:, KI_OFF:KI_OFF + LANE]


def _attn_proj(x, g, w, *, tm):
    m, k = x.shape
    n = w.shape[1]
    kvw = N_KV_HEADS * HEAD_DIM
    row = lambda i: (i, 0)
    return pl.pallas_call(
        _attn_proj_kernel,
        out_shape=[jax.ShapeDtypeStruct((m, n), BF16),
                   jax.ShapeDtypeStruct((m, kvw), F32),
                   jax.ShapeDtypeStruct((m, kvw), F32),
                   jax.ShapeDtypeStruct((m, IDX_DIM), F32),
                   jax.ShapeDtypeStruct((m, LANE), F32)],
        grid=(m // tm,),
        in_specs=[pl.BlockSpec((tm, k), row),
                  pl.BlockSpec((1, k), lambda i: (0, 0)),
                  pl.BlockSpec((k, n), lambda i: (0, 0))],
        out_specs=[pl.BlockSpec((tm, n), row), pl.BlockSpec((tm, kvw), row), pl.BlockSpec((tm, kvw), row),
                   pl.BlockSpec((tm, IDX_DIM), row), pl.BlockSpec((tm, LANE), row)],
        compiler_params=_params("parallel"),
        name="attn_proj",
    )(x, g, w)


def _matmul_resid_kernel(a_ref, w_ref, r_ref, o_ref):
    o_ref[...] = r_ref[...] + jnp.dot(a_ref[...], w_ref[...], preferred_element_type=F32)


def _matmul_resid(a, w, resid, *, tm):
    m, k = a.shape
    n = w.shape[1]
    return pl.pallas_call(
        _matmul_resid_kernel,
        out_shape=jax.ShapeDtypeStruct((m, n), F32),
        grid=(m // tm,),
        in_specs=[pl.BlockSpec((tm, k), lambda i: (i, 0)),
                  pl.BlockSpec((k, n), lambda i: (0, 0)),
                  pl.BlockSpec((tm, n), lambda i: (i, 0))],
        out_specs=pl.BlockSpec((tm, n), lambda i: (i, 0)),
        compiler_params=_params("parallel"),
        name="matmul_resid",
    )(a, w, resid)


def _mlp_kernel(x_ref, g_ref, wu_ref, wd_ref, gf_ref, o_ref, xn_ref, acc_ref, *, final_norm):
    f = pl.program_id(1)

    @pl.when(f == 0)
    def _():
        xn_ref[...] = _rms(x_ref[...], g_ref[...]).astype(BF16)
        acc_ref[...] = jnp.zeros_like(acc_ref)

    a = jnp.dot(xn_ref[...], wu_ref[...], preferred_element_type=F32)
    a = jnp.maximum(a, 0.0)
    a = a * a
    acc_ref[...] += jnp.dot(a.astype(BF16), wd_ref[...], preferred_element_type=F32)

    @pl.when(f == pl.num_programs(1) - 1)
    def _():
        y = x_ref[...] + acc_ref[...]
        if final_norm:
            y = _rms(y, gf_ref[...])
        o_ref[...] = y


def _mlp(x, g, wu, wd, gf, *, tm, tf, final_norm):
    m, d = x.shape
    ff = wu.shape[1]
    return pl.pallas_call(
        functools.partial(_mlp_kernel, final_norm=final_norm),
        out_shape=jax.ShapeDtypeStruct((m, d), F32),
        grid=(m // tm, ff // tf),
        in_specs=[pl.BlockSpec((tm, d), lambda i, f: (i, 0)),
                  pl.BlockSpec((1, d), lambda i, f: (0, 0)),
                  pl.BlockSpec((d, tf), lambda i, f: (0, f)),
                  pl.BlockSpec((tf, d), lambda i, f: (f, 0)),
                  pl.BlockSpec((1, d), lambda i, f: (0, 0))],
        out_specs=pl.BlockSpec((tm, d), lambda i, f: (i, 0)),
        scratch_shapes=[pltpu.VMEM((tm, d), BF16), pltpu.VMEM((tm, d), F32)],
        compiler_params=_params("parallel", "arbitrary"),
        name="mlp",
    )(x, g, wu, wd, gf)


def _attn_kernel(*refs, tq, t, past, front_pad, topk):
    if past:
        (q_ref, qi_ref, kiwi_ref, kn_ref, vn_ref, kin_ref, pk_ref, pv_ref, pki_ref, nb_ref, o_ref,
         k_ref, v_ref, ki_ref, skey_ref, qs_ref, qis_ref, wcol_ref, madd_ref, s_ref, m_ref, acc_ref) = refs
    else:
        (q_ref, qi_ref, kiwi_ref, kn_ref, vn_ref, kin_ref, nb_ref, o_ref,
         k_ref, v_ref, ki_ref, skey_ref, qs_ref, qis_ref, wcol_ref, madd_ref, s_ref, m_ref, acc_ref) = refs
    w = KEY_WINDOW
    rows = GQ * tq
    rb = madd_ref.shape[0]
    keys_on_rows = skey_ref.shape[1] == w
    kax = 0 if keys_on_rows else 1
    qshape = (1, tq) if keys_on_rows else (tq, 1)
    i = pl.program_id(1)
    q0 = past + i * tq
    lv = q0 + tq
    nw = jnp.right_shift(lv + (w - 1), 9)
    qchunk = jnp.right_shift(q0 + lax.broadcasted_iota(jnp.int32, qshape, 1 - kax), CHUNK_SHIFT)
    kidx = lax.broadcasted_iota(jnp.int32, (w, 1) if keys_on_rows else (1, w), kax)
    nt = (((1,), (1,)), ((), ()))

    def win_row(j):
        return pl.multiple_of(lv + front_pad - w * (j + 1), tq)

    @pl.when(i == 0)
    def _():
        ones_col = jnp.where(lax.broadcasted_iota(jnp.int32, (1, LANE - HEAD_DIM), 1) == 0, 1.0, 0.0)

        def put(r0, n, kc, vc, kic):
            ones = jnp.broadcast_to(ones_col, (n, LANE - HEAD_DIM)).astype(BF16)
            for g in range(N_KV_HEADS):
                gs = slice(g * HEAD_DIM, (g + 1) * HEAD_DIM)
                k_ref[g, r0:r0 + n, :] = kc[:, gs]
                v_ref[g, r0:r0 + n, :] = jnp.concatenate([vc[:, gs], ones], axis=1)
            ki_ref[r0:r0 + n, :] = kic

        kvw = N_KV_HEADS * HEAD_DIM
        if front_pad:
            put(0, front_pad, jnp.zeros((front_pad, kvw), BF16), jnp.zeros((front_pad, kvw), BF16),
                jnp.zeros((front_pad, IDX_DIM), BF16))
        step = min(t, w)
        for c0 in range(0, past, w):
            put(front_pad + c0, w, pk_ref[c0:c0 + w, :].astype(BF16), pv_ref[c0:c0 + w, :].astype(BF16),
                pki_ref[c0:c0 + w, :].astype(BF16))
        for c0 in range(0, t, step):
            put(front_pad + past + c0, step, kn_ref[c0:c0 + step, :], vn_ref[c0:c0 + step, :],
                kin_ref[c0:c0 + step, 0:IDX_DIM])

    for h in range(N_HEADS):
        g, jj = divmod(h, GQ)
        qs_ref[g, jj * tq:(jj + 1) * tq, :] = q_ref[:, h * HEAD_DIM:(h + 1) * HEAD_DIM]

    if keys_on_rows:
        wi_t = kiwi_ref[...].T[IDX_DIM:IDX_DIM + IDX_HEADS, :]
        for hp in range(IDX_HEADS // 2):
            for u in range(2):
                h = 2 * hp + u
                qis_ref[hp, u * tq:(u + 1) * tq, :] = qi_ref[:, h * IDX_DIM:(h + 1) * IDX_DIM]
    else:
        wi = kiwi_ref[:, IDX_DIM:IDX_DIM + IDX_HEADS]
        for h in range(IDX_HEADS):
            qis_ref[0, h * tq:(h + 1) * tq, :] = qi_ref[:, h * IDX_DIM:(h + 1) * IDX_DIM]
            wcol_ref[h * tq:(h + 1) * tq, :] = wi[:, h:h + 1]

    def score_body(j, carry):
        kib = ki_ref[pl.ds(win_row(j), w), :]
        if keys_on_rows:
            sc = jnp.zeros((w, tq), F32)
            for hp in range(IDX_HEADS // 2):
                s2 = jnp.maximum(lax.dot_general(kib, qis_ref[hp], nt, preferred_element_type=F32), 0.0)
                sc = sc + s2[:, 0:tq] * wi_t[2 * hp:2 * hp + 1, :]
                sc = sc + s2[:, tq:2 * tq] * wi_t[2 * hp + 1:2 * hp + 2, :]
        else:
            s8 = jnp.maximum(lax.dot_general(qis_ref[0], kib, nt, preferred_element_type=F32), 0.0)
            s8 = s8 * wcol_ref[...]
            sc = s8[0:tq]
            for h in range(1, IDX_HEADS):
                sc = sc + s8[h * tq:(h + 1) * tq]
        sc = sc * IDX_SCALE
        kpos = (lv - w * (j + 1)) + kidx
        vis = (kpos >= 0) & (jnp.right_shift(kpos, CHUNK_SHIFT) <= qchunk)
        bits = pltpu.bitcast(sc, jnp.int32)
        key = bits ^ (jnp.right_shift(bits, 31) & 0x7FFFFFFF)
        key = jnp.where(sc == 0.0, 0, key)
        skey_ref[j] = jnp.where(vis, key, KEY_NEG_INF)
        return carry

    lax.fori_loop(0, nw, score_body, 0)

    if keys_on_rows:
        part = (4 * 8, tq)

        def fold(c):
            return jnp.sum(c.reshape(w // part[0], part[0], tq), axis=0)
    else:
        part = (tq, LANE)

        def fold(c):
            return (c[:, 0:LANE] + c[:, LANE:2 * LANE]) + (c[:, 2 * LANE:3 * LANE] + c[:, 3 * LANE:4 * LANE])

    def total(acc):
        return jnp.sum(acc, axis=kax, keepdims=True)

    def count_ge(cand):
        def body(j, acc):
            return acc + fold(jnp.where(skey_ref[j] >= cand, 1.0, 0.0))
        return total(lax.fori_loop(0, nw, body, jnp.zeros(part, F32)))

    bits_per_check = 4

    def search_cond(carry):
        it, _, cnt = carry
        return (it < 32) & (jnp.sum(jnp.where(cnt != float(topk), 1.0, 0.0)) > 0.0)

    def search_body(carry):
        it, prefix, cnt = carry
        for u in range(bits_per_check):
            cand = prefix + lax.shift_left(jnp.int32(1), 31 - (it + u))
            c = count_ge(cand)
            take = c >= float(topk)
            prefix = jnp.where(take, cand, prefix)
            cnt = jnp.where(take, c, cnt)
        return it + bits_per_check, prefix, cnt

    _, tstar, _ = lax.while_loop(
        search_cond, search_body,
        (jnp.int32(0), jnp.full(qshape, INT_MIN, jnp.int32), jnp.full(qshape, -1.0, F32)))

    def tie_count_body(j, carry):
        ag, ae = carry
        kt = skey_ref[j]
        return (ag + fold(jnp.where(kt > tstar, 1.0, 0.0)),
                ae + fold(jnp.where(kt == tstar, 1.0, 0.0)))

    ag, ae = lax.fori_loop(0, nw, tie_count_body, (jnp.zeros(part, F32), jnp.zeros(part, F32)))
    need = float(topk) - total(ag)
    n_eq = total(ae)
    bad = (n_eq > need) & (tstar > KEY_NEG_INF)
    n_bad = jnp.sum(jnp.where(bad, 1.0, 0.0))

    @pl.when(n_bad > 0.0)
    def _():
        r_i = lax.broadcasted_iota(jnp.int32, (w, w), 0)
        c_i = lax.broadcasted_iota(jnp.int32, (w, w), 1)
        earlier = jnp.where(c_i < r_i if keys_on_rows else r_i < c_i, 1.0, 0.0).astype(BF16)

        def tie_body(t, seen):
            j = nw - 1 - t
            kt = skey_ref[j]
            eq = kt == tstar
            eqf = jnp.where(eq, 1.0, 0.0)
            if keys_on_rows:
                inwin = jnp.dot(earlier, eqf.astype(BF16), preferred_element_type=F32)
            else:
                inwin = jnp.dot(eqf.astype(BF16), earlier, preferred_element_type=F32)
            skey_ref[j] = jnp.where(eq & bad & (seen + inwin >= need), KEY_NEG_INF, kt)
            return seen + total(eqf)

        lax.fori_loop(0, nw, tie_body, jnp.zeros(qshape, F32))

    teff = jnp.maximum(tstar, KEY_NEG_INF + 1)

    m_ref[...] = jnp.full(m_ref.shape, NEG_BIG, F32)
    acc_ref[...] = jnp.zeros(acc_ref.shape, F32)

    def attend(j, near):
        row = win_row(j)
        tile = jnp.where(skey_ref[j] >= teff, 0.0, NEG_BIG)
        if keys_on_rows:
            tile = tile.T
        for r in range(rb // tq):
            madd_ref[r * tq:(r + 1) * tq, :] = tile
        tiles = [(g, r0) for g in range(N_KV_HEADS) for r0 in range(0, rows, rb)]

        for t, (g, r0) in enumerate(tiles):
            kb = k_ref[g, pl.ds(row, w), :]
            s = lax.dot_general(qs_ref[g, r0:r0 + rb, :], kb, nt, preferred_element_type=F32)
            s = s + madd_ref[...]
            if near:
                s = s + nb_ref[g, r0:r0 + rb, :]
            s_ref[t] = s
        for t, (g, r0) in enumerate(tiles):
            rs = slice(r0, r0 + rb)
            s = s_ref[t]
            m_old = m_ref[g, rs, :]
            m_new = jnp.maximum(m_old, jnp.max(s, axis=1, keepdims=True))
            p = jnp.exp2(s - jnp.concatenate([m_new] * (w // LANE), axis=1))
            vb = v_ref[g, pl.ds(row, w), :]
            acc_ref[g, rs, :] = (jnp.exp2(m_old - m_new) * acc_ref[g, rs, :]
                                 + jnp.dot(p.astype(BF16), vb, preferred_element_type=F32))
            m_ref[g, rs, :] = m_new

    attend(0, True)

    def far_body(j, carry):
        attend(j, False)
        return carry

    lax.fori_loop(1, nw, far_body, 0)

    for h in range(N_HEADS):
        g, jj = divmod(h, GQ)
        a = acc_ref[g, jj * tq:(jj + 1) * tq, :]
        o_ref[:, h * HEAD_DIM:(h + 1) * HEAD_DIM] = (
            a[:, 0:HEAD_DIM] / a[:, HEAD_DIM:HEAD_DIM + 1]).astype(o_ref.dtype)


def _rel_bucket(rel):
    half = NUM_BUCKETS // 2
    max_exact = half // 2
    base = jnp.where(rel > 0, half, 0)
    n = jnp.abs(rel)
    nf = jnp.maximum(n, 1).astype(jnp.float32)
    large = max_exact + (jnp.log(nf / max_exact) / math.log(MAX_DISTANCE / max_exact)
                         * (half - max_exact)).astype(jnp.int32)
    large = jnp.minimum(large, half - 1)
    return base + jnp.where(n < max_exact, n, large)


def _near_bias(rel_bias, tq):
    r = jnp.arange(tq, dtype=jnp.int32)[:, None]
    c = jnp.arange(KEY_WINDOW, dtype=jnp.int32)[None, :]
    rel = (tq - KEY_WINDOW + c) - r
    tab = rel_bias.astype(F32)
    far = _rel_bucket(jnp.full((1,), -(1 << 20), jnp.int32))
    far_row = jnp.dot(jax.nn.one_hot(far, NUM_BUCKETS, dtype=F32), tab, precision=lax.Precision.HIGHEST)
    pick = jax.nn.one_hot(_rel_bucket(rel), NUM_BUCKETS, dtype=F32)
    nb = jnp.einsum('qcb,bh->hqc', pick, (tab - far_row) * LOG2E, precision=lax.Precision.HIGHEST)
    return nb.reshape(N_KV_HEADS, GQ * tq, KEY_WINDOW)


def _dsa_mixer(projb, kiwi, past_k, past_v, past_ik, rel_bias, b, t):
    past = past_k.shape[1]
    n_keys = past + t
    topk = min(TOPK_MAX, n_keys // 4)
    tq = 128 if t % 128 == 0 else t
    assert KEY_WINDOW % tq == 0 and past % KEY_WINDOW == 0 and tq % 16 == 0 and n_keys >= KEY_WINDOW
    assert t % min(t, KEY_WINDOW) == 0
    front_pad = KEY_WINDOW - tq
    lp = front_pad + n_keys
    nq = t // tq
    nw_max = (n_keys + KEY_WINDOW - 1) // KEY_WINDOW
    rows = GQ * tq
    rb = min(rows, MAX_TILE_ROWS)
    keys_on_rows = tq % LANE == 0
    assert keys_on_rows or IDX_HEADS * tq <= MAX_TILE_ROWS
    kvw = N_KV_HEADS * HEAD_DIM
    qw = N_HEADS * HEAD_DIM
    qiw = IDX_HEADS * IDX_DIM

    qrow = lambda bi, i: (bi * nq + i, 0)
    in_specs = [pl.BlockSpec((tq, qw), qrow),
                pl.BlockSpec((tq, qiw), lambda bi, i: (bi * nq + i, QI_OFF // qiw)),
                pl.BlockSpec((tq, LANE), qrow),
                pl.BlockSpec((t, kvw), lambda bi, i: (bi, K_OFF // kvw)),
                pl.BlockSpec((t, kvw), lambda bi, i: (bi, V_OFF // kvw)),
                pl.BlockSpec((t, LANE), lambda bi, i: (bi, KI_OFF // LANE))]
    args = [projb, projb, kiwi, projb, projb, projb]
    if past:
        in_specs += [pl.BlockSpec((None, past, kvw), lambda bi, i: (bi, 0, 0)),
                     pl.BlockSpec((None, past, kvw), lambda bi, i: (bi, 0, 0)),
                     pl.BlockSpec((None, past, IDX_DIM), lambda bi, i: (bi, 0, 0))]
        args += [past_k.reshape(b, past, kvw), past_v.reshape(b, past, kvw), past_ik]
    in_specs.append(pl.BlockSpec((N_KV_HEADS, rows, KEY_WINDOW), lambda bi, i: (0, 0, 0)))
    args.append(_near_bias(rel_bias, tq))

    kern = functools.partial(_attn_kernel, tq=tq, t=t, past=past, front_pad=front_pad, topk=topk)
    return pl.pallas_call(
        kern,
        out_shape=jax.ShapeDtypeStruct((b * t, qw), BF16),
        grid=(b, nq),
        in_specs=in_specs,
        out_specs=pl.BlockSpec((tq, qw), qrow),
        scratch_shapes=[pltpu.VMEM((N_KV_HEADS, lp, HEAD_DIM), BF16),
                        pltpu.VMEM((N_KV_HEADS, lp, LANE), BF16),
                        pltpu.VMEM((lp, IDX_DIM), BF16),
                        pltpu.VMEM((nw_max, KEY_WINDOW, tq) if keys_on_rows
                                   else (nw_max, tq, KEY_WINDOW), jnp.int32),
                        pltpu.VMEM((N_KV_HEADS, rows, HEAD_DIM), BF16),
                        pltpu.VMEM((IDX_HEADS // 2, 2 * tq, IDX_DIM) if keys_on_rows
                                   else (1, IDX_HEADS * tq, IDX_DIM), BF16),
                        pltpu.VMEM((8, 1) if keys_on_rows else (IDX_HEADS * tq, 1), F32),
                        pltpu.VMEM((rb, KEY_WINDOW), F32),
                        pltpu.VMEM((N_KV_HEADS * rows // rb, rb, KEY_WINDOW), F32),
                        pltpu.VMEM((N_KV_HEADS, rows, LANE), F32),
                        pltpu.VMEM((N_KV_HEADS, rows, LANE), F32)],
        compiler_params=_params("arbitrary", "arbitrary"),
        name="dsa_attention",
    )(*args)


def _ssd_kernel(z_ref, xlo_ref, xhi_ref, dt_ref, cst_ref, st0_ref, cw_ref, cb_ref, dtb_ref, alog_ref,
                dexp_ref, nw_ref, expand_ref, y_ref, ncv_ref, stout_ref,
                xpad_ref, act_ref, yd_ref, st_ref, *, q):
    c = pl.program_id(1)
    half = CONV_DIM // 2

    @pl.when(c == 0)
    def _():
        xpad_ref[8 - (D_CONV - 1):8, :] = cst_ref[...]
        for h in range(SSM_HEADS):
            g, jj = divmod(h, HEADS_PER_GROUP)
            st_ref[g, :, jj * SSM_HEAD_DIM:(jj + 1) * SSM_HEAD_DIM] = st0_ref[h].T

    cc = 512
    for c0 in range(0, CONV_DIM, cc):
        cs = slice(c0, c0 + cc)
        src = xlo_ref[:, c0:c0 + cc] if c0 < half else xhi_ref[:, c0 - half:c0 - half + cc]
        xpad_ref[8:8 + q, cs] = src
        conv = cb_ref[:, cs] + xpad_ref[5:5 + q, cs] * cw_ref[0:1, cs]
        conv = conv + xpad_ref[6:6 + q, cs] * cw_ref[1:2, cs]
        conv = conv + xpad_ref[7:7 + q, cs] * cw_ref[2:3, cs]
        conv = conv + xpad_ref[8:8 + q, cs] * cw_ref[3:4, cs]
        act_ref[:, cs] = conv * _sigmoid(conv)
        xpad_ref[0:8, cs] = xpad_ref[q:q + 8, cs]

    hi = lax.Precision.HIGHEST
    x_dt = dt_ref[:, 0:SSM_HEADS] + dtb_ref[...]
    dt = jnp.maximum(x_dt, 0.0) + jnp.log1p(jnp.exp(-jnp.abs(x_dt)))
    a = dt * (-jnp.exp(alog_ref[...]))
    rr = lax.broadcasted_iota(jnp.int32, (q, q), 0)
    cl = lax.broadcasted_iota(jnp.int32, (q, q), 1)
    causal = rr >= cl
    tri = jnp.where(causal, 1.0, 0.0)
    a_cs = jnp.dot(tri, a, precision=hi, preferred_element_type=F32)
    eye = jnp.where(lax.broadcasted_iota(jnp.int32, (SSM_HEADS, SSM_HEADS), 0)
                    == lax.broadcasted_iota(jnp.int32, (SSM_HEADS, SSM_HEADS), 1), 1.0, 0.0)
    a_cs_t = lax.dot_general(eye, a_cs, (((1,), (1,)), ((), ())), precision=hi,
                             preferred_element_type=F32)
    a_end = a_cs[q - 1:q, :]
    per_head = jnp.concatenate([dt, jnp.exp(a_cs), jnp.exp(a_end - a_cs)], axis=0)
    top = per_head.astype(BF16)
    rest = (per_head - top.astype(F32)).astype(BF16)
    spread = jnp.dot(jnp.concatenate([top, rest], axis=0), expand_ref[...],
                     preferred_element_type=F32)
    dt_x = spread[0:q] + spread[3 * q:4 * q]
    e_x = spread[q:2 * q] + spread[4 * q:5 * q]
    dte_x = spread[2 * q:3 * q] + spread[5 * q:6 * q]
    dec_x = e_x[q - 1:q, :]

    for g in range(SSM_GROUPS):
        gs = slice(g * GROUP_W, (g + 1) * GROUP_W)
        bg = act_ref[:, D_INNER + g * D_STATE:D_INNER + (g + 1) * D_STATE].astype(BF16)
        cg = act_ref[:, D_INNER + (SSM_GROUPS + g) * D_STATE:
                     D_INNER + (SSM_GROUPS + g + 1) * D_STATE].astype(BF16)
        cb = lax.dot_general(cg, bg, (((1,), (1,)), ((), ())), preferred_element_type=F32)
        xs_g = act_ref[:, gs]
        xd_g = xs_g * dt_x[:, gs]
        st_g = st_ref[g]
        y_off = jnp.dot(cg, st_g.astype(BF16), preferred_element_type=F32) * e_x[:, gs]
        for jj in range(HEADS_PER_GROUP):
            h = g * HEADS_PER_GROUP + jj
            seg = a_cs[:, h:h + 1] - a_cs_t[h:h + 1, :]
            decay = jnp.where(causal, jnp.exp(jnp.where(causal, seg, 0.0)), 0.0)
            hs = slice(jj * SSM_HEAD_DIM, (jj + 1) * SSM_HEAD_DIM)
            yd_ref[:, hs] = jnp.dot((cb * decay).astype(BF16), xd_g[:, hs].astype(BF16),
                                    preferred_element_type=F32)
        st_ref[g] = st_g * dec_x[:, gs] + lax.dot_general(
            bg, (xd_g * dte_x[:, gs]).astype(BF16), (((0,), (0,)), ((), ())),
            preferred_element_type=F32)
        y = yd_ref[...] + y_off + xs_g * dexp_ref[:, gs]
        zg = z_ref[:, gs]
        yg = y * (zg * _sigmoid(zg))
        ms = jnp.mean(yg * yg, axis=-1, keepdims=True)
        y_ref[:, gs] = (yg * lax.rsqrt(ms + EPS) * nw_ref[:, gs]).astype(y_ref.dtype)

    @pl.when(c == pl.num_programs(1) - 1)
    def _():
        ncv_ref[...] = xpad_ref[8 - (D_CONV - 1):8, :]
        for h in range(SSM_HEADS):
            g, jj = divmod(h, HEADS_PER_GROUP)
            stout_ref[h] = st_ref[g, :, jj * SSM_HEAD_DIM:(jj + 1) * SSM_HEAD_DIM].T


def _ssd_mixer(proj, conv_state, ssm_state, conv_w, conv_b, dt_bias, a_log, d_skip, norm_w, b, t):
    q = 128 if t % 128 == 0 else t
    assert q % 8 == 0 and t >= D_CONV - 1 and CONV_DIM == 2 * D_INNER
    nc = t // q
    expand = jnp.repeat(jnp.eye(SSM_HEADS, dtype=BF16), SSM_HEAD_DIM, axis=1)
    dexp = jnp.repeat(d_skip.astype(F32), SSM_HEAD_DIM)[None, :]
    row = lambda k: (lambda bi, c: (bi * nc + c, k))
    const2 = lambda bi, c: (0, 0)
    state_spec = pl.BlockSpec((None, SSM_HEADS, SSM_HEAD_DIM, D_STATE), lambda bi, c: (bi, 0, 0, 0))
    conv_spec = pl.BlockSpec((None, D_CONV - 1, CONV_DIM), lambda bi, c: (bi, 0, 0))
    y, new_conv, new_ssm = pl.pallas_call(
        functools.partial(_ssd_kernel, q=q),
        out_shape=[jax.ShapeDtypeStruct((b * t, D_INNER), BF16),
                   jax.ShapeDtypeStruct((b, D_CONV - 1, CONV_DIM), F32),
                   jax.ShapeDtypeStruct((b, SSM_HEADS, SSM_HEAD_DIM, D_STATE), F32)],
        grid=(b, nc),
        in_specs=[pl.BlockSpec((q, D_INNER), row(SSM_Z_OFF // D_INNER)),
                  pl.BlockSpec((q, D_INNER), row(SSM_XBC_OFF // D_INNER)),
                  pl.BlockSpec((q, D_INNER), row(SSM_XBC_OFF // D_INNER + 1)),
                  pl.BlockSpec((q, LANE), row(SSM_DT_OFF // LANE)),
                  conv_spec,
                  state_spec,
                  pl.BlockSpec((D_CONV, CONV_DIM), const2),
                  pl.BlockSpec((1, CONV_DIM), const2),
                  pl.BlockSpec((1, SSM_HEADS), const2),
                  pl.BlockSpec((1, SSM_HEADS), const2),
                  pl.BlockSpec((1, D_INNER), const2),
                  pl.BlockSpec((1, D_INNER), const2),
                  pl.BlockSpec((SSM_HEADS, D_INNER), const2)],
        out_specs=[pl.BlockSpec((q, D_INNER), row(0)), conv_spec, state_spec],
        scratch_shapes=[pltpu.VMEM((q + 8, CONV_DIM), F32),
                        pltpu.VMEM((q, CONV_DIM), F32),
                        pltpu.VMEM((q, GROUP_W), F32),
                        pltpu.VMEM((SSM_GROUPS, D_STATE, GROUP_W), F32)],
        compiler_params=_params("arbitrary", "arbitrary"),
        name="ssd_mixer",
    )(proj, proj, proj, proj, conv_state.astype(F32), ssm_state.astype(F32), conv_w.astype(F32),
      conv_b.astype(F32)[None, :], dt_bias.astype(F32)[None, :], a_log.astype(F32)[None, :], dexp,
      norm_w.astype(F32)[None, :], expand)
    return y, new_conv, new_ssm.astype(ssm_state.dtype)


def _tile_m(m):
    for tm in (1024, 512, 256, 128, 64, 32, 16, 8):
        if m % tm == 0:
            return tm
    raise ValueError(m)


def _pad_cols(w, n):
    return jnp.pad(w, ((0, 0), (0, n - w.shape[1])))


def _prep_weights(norm_mix, norm_ffn, norm_final, attn_w_in, attn_w_o, ssm_w_in, ssm_w_out,
                  mlp_w_up, mlp_w_down):
    col_scale = jnp.where(jnp.arange(ATTN_COLS) < K_OFF, ATTN_SCALE * LOG2E, 1.0).astype(F32)
    return dict(
        attn_in=_pad_cols(attn_w_in[0] * col_scale, _round_up(ATTN_COLS, LANE)).astype(BF16),
        attn_o=attn_w_o[0].astype(BF16),
        ssm_in=_pad_cols(ssm_w_in[0], _round_up(SSM_COLS, LANE)).astype(BF16),
        ssm_out=ssm_w_out[0].astype(BF16),
        up=[mlp_w_up[i].astype(BF16) for i in range(2)],
        down=[mlp_w_down[i].astype(BF16) for i in range(2)],
        g_mix=[norm_mix[i].astype(F32)[None, :] for i in range(2)],
        g_ffn=[norm_ffn[i].astype(F32)[None, :] for i in range(2)],
        g_final=norm_final.astype(F32)[None, :],
    )


def _trunk(x, past_k, past_v, past_ik, conv_st, ssm_st, rel_bias, wts, ssm_conv_w, ssm_conv_b,
           ssm_dt_bias, ssm_a_log, ssm_d, ssm_norm):
    b, t, d = x.shape
    m = b * t
    tm = _tile_m(m)
    x2 = x.reshape(m, d).astype(F32)

    projb, k_new, v_new, ki_new, kiwi = _attn_proj(x2, wts["g_mix"][0], wts["attn_in"], tm=min(tm, 512))
    ao = _dsa_mixer(projb, kiwi, past_k, past_v, past_ik, rel_bias, b, t)
    x2 = _matmul_resid(ao, wts["attn_o"], x2, tm=min(tm, 512))
    x2 = _mlp(x2, wts["g_ffn"][0], wts["up"][0], wts["down"][0], wts["g_final"],
              tm=tm, tf=1024, final_norm=False)

    proj = _norm_matmul(x2, wts["g_mix"][1], wts["ssm_in"], tm=min(tm, 256), tn=wts["ssm_in"].shape[1])
    y, new_conv, new_ssm = _ssd_mixer(proj, conv_st, ssm_st, ssm_conv_w, ssm_conv_b, ssm_dt_bias,
                                      ssm_a_log, ssm_d, ssm_norm, b, t)
    x2 = _matmul_resid(y, wts["ssm_out"], x2, tm=min(tm, 512))
    x2 = _mlp(x2, wts["g_ffn"][1], wts["up"][1], wts["down"][1], wts["g_final"],
              tm=tm, tf=1024, final_norm=True)

    dt = x.dtype
    return (x2.reshape(b, t, d).astype(dt),
            k_new.reshape(1, b, t, N_KV_HEADS, HEAD_DIM).astype(dt),
            v_new.reshape(1, b, t, N_KV_HEADS, HEAD_DIM).astype(dt),
            ki_new.reshape(1, b, t, IDX_DIM).astype(dt), new_conv[None].astype(dt), new_ssm[None])


def kernel(x_prompt, x_sample, cache_k, cache_v, cache_idx_k, state_conv, state_ssm, rel_bias, norm_mix, norm_ffn, norm_final, attn_w_in, attn_w_o, ssm_w_in, ssm_conv_w, ssm_conv_b, ssm_dt_bias, ssm_a_log, ssm_d, ssm_norm, ssm_w_out, mlp_w_up, mlp_w_down):
    wts = _prep_weights(norm_mix, norm_ffn, norm_final, attn_w_in, attn_w_o, ssm_w_in, ssm_w_out,
                        mlp_w_up, mlp_w_down)
    bp = x_prompt.shape[0]
    dtp = x_prompt.dtype
    empty_k = jnp.zeros((bp, 0, N_KV_HEADS, HEAD_DIM), dtp)
    empty_ik = jnp.zeros((bp, 0, IDX_DIM), dtp)
    zero_conv = jnp.zeros((bp, D_CONV - 1, CONV_DIM), dtp)
    zero_ssm = jnp.zeros((bp, SSM_HEADS, SSM_HEAD_DIM, D_STATE), dtp)
    args = (rel_bias, wts, ssm_conv_w[0], ssm_conv_b[0], ssm_dt_bias[0], ssm_a_log[0], ssm_d[0],
            ssm_norm[0])
    yp, kp, vp, ikp, cp, sp = _trunk(x_prompt, empty_k, empty_k, empty_ik, zero_conv, zero_ssm, *args)
    ys, ks, vs, iks, cs, ss = _trunk(x_sample, cache_k[0], cache_v[0], cache_idx_k[0],
                                     state_conv[0], state_ssm[0], *args)
    return (yp, ys, kp, vp, ikp, cp, sp, ks, vs, iks, cs, ss)
```

```python
import functools
import math

import jax
import jax.numpy as jnp
from jax import lax
from jax.experimental import pallas as pl
from jax.experimental.pallas import tpu as pltpu

F32 = jnp.float32
BF16 = jnp.bfloat16

D_MODEL = 1024
CHUNK = 64
CHUNK_SHIFT = 6
N_HEADS = 16
HEAD_DIM = 64
N_KV_HEADS = 4
GQ = N_HEADS // N_KV_HEADS
IDX_HEADS = 8
IDX_DIM = 64
TOPK_MAX = 256
IDX_SCALE = (IDX_HEADS * IDX_DIM) ** -0.5
ATTN_SCALE = HEAD_DIM ** -0.5
NUM_BUCKETS = 32
MAX_DISTANCE = 128
D_INNER = 2 * D_MODEL
SSM_HEAD_DIM = 64
SSM_HEADS = D_INNER // SSM_HEAD_DIM
SSM_GROUPS = 8
HEADS_PER_GROUP = SSM_HEADS // SSM_GROUPS
GROUP_W = HEADS_PER_GROUP * SSM_HEAD_DIM
D_STATE = 128
D_CONV = 4
CONV_DIM = D_INNER + 2 * SSM_GROUPS * D_STATE
D_FF = 4 * D_MODEL
EPS = 1e-6

Q_OFF = 0
K_OFF = N_HEADS * HEAD_DIM
V_OFF = K_OFF + N_KV_HEADS * HEAD_DIM
QI_OFF = V_OFF + N_KV_HEADS * HEAD_DIM
KI_OFF = QI_OFF + IDX_HEADS * IDX_DIM
WI_OFF = KI_OFF + IDX_DIM
ATTN_COLS = WI_OFF + IDX_HEADS

SSM_Z_OFF = 0
SSM_XBC_OFF = D_INNER
SSM_DT_OFF = D_INNER + CONV_DIM
SSM_COLS = SSM_DT_OFF + SSM_HEADS

LANE = 128
KEY_WINDOW = 512
MAX_TILE_ROWS = 128
LOG2E = 1.4426950408889634
VMEM_LIMIT = 56 * 1024 * 1024

NEG_BIG = -1e30
KEY_NEG_INF = -2139095041
INT_MIN = -2147483648


def _round_up(n, m):
    return (n + m - 1) // m * m


def _rms(x, g):
    ms = jnp.mean(x * x, axis=-1, keepdims=True)
    return x * lax.rsqrt(ms + EPS) * g


def _sigmoid(x):
    return 1.0 / (1.0 + jnp.exp(-x))


def _params(*sem):
    return pltpu.CompilerParams(dimension_semantics=sem, vmem_limit_bytes=VMEM_LIMIT)


def _norm_matmul_kernel(x_ref, g_ref, w_ref, o_ref, xn_ref):
    @pl.when(pl.program_id(1) == 0)
    def _():
        xn_ref[...] = _rms(x_ref[...], g_ref[...]).astype(BF16)

    o_ref[...] = jnp.dot(xn_ref[...], w_ref[...], preferred_element_type=F32)


def _norm_matmul(x, g, w, *, tm, tn):
    m, k = x.shape
    n = w.shape[1]
    return pl.pallas_call(
        _norm_matmul_kernel,
        out_shape=jax.ShapeDtypeStruct((m, n), F32),
        grid=(m // tm, n // tn),
        in_specs=[pl.BlockSpec((tm, k), lambda i, j: (i, 0)),
                  pl.BlockSpec((1, k), lambda i, j: (0, 0)),
                  pl.BlockSpec((k, tn), lambda i, j: (0, j))],
        out_specs=pl.BlockSpec((tm, tn), lambda i, j: (i, j)),
        scratch_shapes=[pltpu.VMEM((tm, k), BF16)],
        compiler_params=_params("parallel", "arbitrary"),
        name="norm_matmul",
    )(x, g, w)


def _attn_proj_kernel(x_ref, g_ref, w_ref, pb_ref, k_ref, v_ref, ki_ref, kiwi_ref):
    xn = _rms(x_ref[...], g_ref[...]).astype(BF16)
    acc = jnp.dot(xn, w_ref[...], preferred_element_type=F32)
    pb_ref[...] = acc.astype(BF16)
    k_ref[...] = acc[:, K_OFF:V_OFF]
    v_ref[...] = acc[:, V_OFF:QI_OFF]
    ki_ref[...] = acc[:, KI_OFF:WI_OFF]
    kiwi_ref[...] = acc[:, KI_OFF:KI_OFF + LANE]


def _attn_proj(x, g, w, *, tm):
    m, k = x.shape
    n = w.shape[1]
    kvw = N_KV_HEADS * HEAD_DIM
    row = lambda i: (i, 0)
    return pl.pallas_call(
        _attn_proj_kernel,
        out_shape=[jax.ShapeDtypeStruct((m, n), BF16),
                   jax.ShapeDtypeStruct((m, kvw), F32),
                   jax.ShapeDtypeStruct((m, kvw), F32),
                   jax.ShapeDtypeStruct((m, IDX_DIM), F32),
                   jax.ShapeDtypeStruct((m, LANE), F32)],
        grid=(m // tm,),
        in_specs=[pl.BlockSpec((tm, k), row),
                  pl.BlockSpec((1, k), lambda i: (0, 0)),
                  pl.BlockSpec((k, n), lambda i: (0, 0))],
        out_specs=[pl.BlockSpec((tm, n), row), pl.BlockSpec((tm, kvw), row), pl.BlockSpec((tm, kvw), row),
                   pl.BlockSpec((tm, IDX_DIM), row), pl.BlockSpec((tm, LANE), row)],
        compiler_params=_params("parallel"),
        name="attn_proj",
    )(x, g, w)


def _tail_kernel(a_ref, wo_ref, r_ref, g_ref, wu_ref, wd_ref, gf_ref, o_ref, *, tf, final_norm):
    x1 = r_ref[...] + jnp.dot(a_ref[...], wo_ref[...], preferred_element_type=F32)
    xn = _rms(x1, g_ref[...]).astype(BF16)
    acc = x1
    for f0 in range(0, wu_ref.shape[1], tf):
        h = jnp.maximum(jnp.dot(xn, wu_ref[:, f0:f0 + tf], preferred_element_type=F32), 0.0)
        acc = acc + jnp.dot((h * h).astype(BF16), wd_ref[f0:f0 + tf, :], preferred_element_type=F32)
    if final_norm:
        acc = _rms(acc, gf_ref[...])
    o_ref[...] = acc


def _layer_tail(a, wo, resid, g, wu, wd, gf, *, tm, tf, final_norm):
    m, ka = a.shape
    d = wo.shape[1]
    ff = wu.shape[1]
    row = lambda i: (i, 0)
    whole = lambda shape: pl.BlockSpec(shape, lambda i: (0, 0), pipeline_mode=pl.Buffered(1))
    return pl.pallas_call(
        functools.partial(_tail_kernel, tf=tf, final_norm=final_norm),
        out_shape=jax.ShapeDtypeStruct((m, d), F32),
        grid=(m // tm,),
        in_specs=[pl.BlockSpec((tm, ka), row), whole((ka, d)), pl.BlockSpec((tm, d), row),
                  whole((1, d)), whole((d, ff)), whole((ff, d)), whole((1, d))],
        out_specs=pl.BlockSpec((tm, d), row),
        compiler_params=_params("parallel"),
        name="layer_tail",
    )(a, wo, resid, g, wu, wd, gf)


def _attn_kernel(*refs, tq, t, past, front_pad, topk):
    if past:
        (q_ref, qi_ref, kiwi_ref, kn_ref, vn_ref, kin_ref, pk_ref, pv_ref, pki_ref, nb_ref, o_ref,
         k_ref, v_ref, ki_ref, skey_ref, qs_ref, qis_ref, wcol_ref, madd_ref, s_ref, m_ref, acc_ref) = refs
    else:
        (q_ref, qi_ref, kiwi_ref, kn_ref, vn_ref, kin_ref, nb_ref, o_ref,
         k_ref, v_ref, ki_ref, skey_ref, qs_ref, qis_ref, wcol_ref, madd_ref, s_ref, m_ref, acc_ref) = refs
    w = KEY_WINDOW
    rows = GQ * tq
    rb = madd_ref.shape[0]
    keys_on_rows = skey_ref.shape[1] == w
    kax = 0 if keys_on_rows else 1
    qshape = (1, tq) if keys_on_rows else (tq, 1)
    i = pl.program_id(1)
    q0 = past + i * tq
    lv = q0 + tq
    nw = jnp.right_shift(lv + (w - 1), 9)
    qchunk = jnp.right_shift(q0 + lax.broadcasted_iota(jnp.int32, qshape, 1 - kax), CHUNK_SHIFT)
    kidx = lax.broadcasted_iota(jnp.int32, (w, 1) if keys_on_rows else (1, w), kax)
    nt = (((1,), (1,)), ((), ()))

    def win_row(j):
        return pl.multiple_of(lv + front_pad - w * (j + 1), tq)

    @pl.when(i == 0)
    def _():
        ones_col = jnp.where(lax.broadcasted_iota(jnp.int32, (1, LANE - HEAD_DIM), 1) == 0, 1.0, 0.0)

        def put(r0, n, kc, vc, kic):
            ones = jnp.broadcast_to(ones_col, (n, LANE - HEAD_DIM)).astype(BF16)
            for g in range(N_KV_HEADS):
                gs = slice(g * HEAD_DIM, (g + 1) * HEAD_DIM)
                k_ref[g, r0:r0 + n, :] = kc[:, gs]
                v_ref[g, r0:r0 + n, :] = jnp.concatenate([vc[:, gs], ones], axis=1)
            ki_ref[r0:r0 + n, :] = kic

        kvw = N_KV_HEADS * HEAD_DIM
        if front_pad:
            put(0, front_pad, jnp.zeros((front_pad, kvw), BF16), jnp.zeros((front_pad, kvw), BF16),
                jnp.zeros((front_pad, IDX_DIM), BF16))
        step = min(t, w)
        for c0 in range(0, past, w):
            put(front_pad + c0, w, pk_ref[c0:c0 + w, :].astype(BF16), pv_ref[c0:c0 + w, :].astype(BF16),
                pki_ref[c0:c0 + w, :].astype(BF16))
        for c0 in range(0, t, step):
            put(front_pad + past + c0, step, kn_ref[c0:c0 + step, :], vn_ref[c0:c0 + step, :],
                kin_ref[c0:c0 + step, 0:IDX_DIM])

    for h in range(N_HEADS):
        g, jj = divmod(h, GQ)
        qs_ref[g, jj * tq:(jj + 1) * tq, :] = q_ref[:, h * HEAD_DIM:(h + 1) * HEAD_DIM]

    if keys_on_rows:
        wi_t = kiwi_ref[...].T[IDX_DIM:IDX_DIM + IDX_HEADS, :]
        for hp in range(IDX_HEADS // 2):
            for u in range(2):
                h = 2 * hp + u
                qis_ref[hp, u * tq:(u + 1) * tq, :] = qi_ref[:, h * IDX_DIM:(h + 1) * IDX_DIM]
    else:
        wi = kiwi_ref[:, IDX_DIM:IDX_DIM + IDX_HEADS]
        for h in range(IDX_HEADS):
            qis_ref[0, h * tq:(h + 1) * tq, :] = qi_ref[:, h * IDX_DIM:(h + 1) * IDX_DIM]
            wcol_ref[h * tq:(h + 1) * tq, :] = wi[:, h:h + 1]

    def score_body(j, carry):
        kib = ki_ref[pl.ds(win_row(j), w), :]
        if keys_on_rows:
            sc = jnp.zeros((w, tq), F32)
            for hp in range(IDX_HEADS // 2):
                s2 = jnp.maximum(lax.dot_general(kib, qis_ref[hp], nt, preferred_element_type=F32), 0.0)
                sc = sc + s2[:, 0:tq] * wi_t[2 * hp:2 * hp + 1, :]
                sc = sc + s2[:, tq:2 * tq] * wi_t[2 * hp + 1:2 * hp + 2, :]
        else:
            s8 = jnp.maximum(lax.dot_general(qis_ref[0], kib, nt, preferred_element_type=F32), 0.0)
            s8 = s8 * wcol_ref[...]
            sc = s8[0:tq]
            for h in range(1, IDX_HEADS):
                sc = sc + s8[h * tq:(h + 1) * tq]
        sc = sc * IDX_SCALE
        kpos = (lv - w * (j + 1)) + kidx
        vis = (kpos >= 0) & (jnp.right_shift(kpos, CHUNK_SHIFT) <= qchunk)
        bits = pltpu.bitcast(sc, jnp.int32)
        key = bits ^ (jnp.right_shift(bits, 31) & 0x7FFFFFFF)
        key = jnp.where(sc == 0.0, 0, key)
        skey_ref[j] = jnp.where(vis, key, KEY_NEG_INF)
        return carry

    lax.fori_loop(0, nw, score_body, 0)

    if keys_on_rows:
        part = (4 * 8, tq)

        def fold(c):
            return jnp.sum(c.reshape(w // part[0], part[0], tq), axis=0)
    else:
        part = (tq, LANE)

        def fold(c):
            return (c[:, 0:LANE] + c[:, LANE:2 * LANE]) + (c[:, 2 * LANE:3 * LANE] + c[:, 3 * LANE:4 * LANE])

    def total(acc):
        return jnp.sum(acc, axis=kax, keepdims=True)

    def count_ge(cand):
        def body(j, acc):
            return acc + fold(jnp.where(skey_ref[j] >= cand, 1.0, 0.0))
        return total(lax.fori_loop(0, nw, body, jnp.zeros(part, F32)))

    def search_body(it, prefix):
        cand = prefix + lax.shift_left(jnp.int32(1), 31 - it)
        return jnp.where(count_ge(cand) >= float(topk), cand, prefix)

    tstar = lax.fori_loop(0, 32, search_body, jnp.full(qshape, INT_MIN, jnp.int32))

    def tie_count_body(j, carry):
        ag, ae = carry
        kt = skey_ref[j]
        return (ag + fold(jnp.where(kt > tstar, 1.0, 0.0)),
                ae + fold(jnp.where(kt == tstar, 1.0, 0.0)))

    ag, ae = lax.fori_loop(0, nw, tie_count_body, (jnp.zeros(part, F32), jnp.zeros(part, F32)))
    need = float(topk) - total(ag)
    n_eq = total(ae)
    bad = (n_eq > need) & (tstar > KEY_NEG_INF)
    n_bad = jnp.sum(jnp.where(bad, 1.0, 0.0))

    @pl.when(n_bad > 0.0)
    def _():
        r_i = lax.broadcasted_iota(jnp.int32, (w, w), 0)
        c_i = lax.broadcasted_iota(jnp.int32, (w, w), 1)
        earlier = jnp.where(c_i < r_i if keys_on_rows else r_i < c_i, 1.0, 0.0).astype(BF16)

        def tie_body(t, seen):
            j = nw - 1 - t
            kt = skey_ref[j]
            eq = kt == tstar
            eqf = jnp.where(eq, 1.0, 0.0)
            if keys_on_rows:
                inwin = jnp.dot(earlier, eqf.astype(BF16), preferred_element_type=F32)
            else:
                inwin = jnp.dot(eqf.astype(BF16), earlier, preferred_element_type=F32)
            skey_ref[j] = jnp.where(eq & bad & (seen + inwin >= need), KEY_NEG_INF, kt)
            return seen + total(eqf)

        lax.fori_loop(0, nw, tie_body, jnp.zeros(qshape, F32))

    teff = jnp.maximum(tstar, KEY_NEG_INF + 1)

    m_ref[...] = jnp.full(m_ref.shape, NEG_BIG, F32)
    acc_ref[...] = jnp.zeros(acc_ref.shape, F32)

    def attend(j, near):
        row = win_row(j)
        tile = jnp.where(skey_ref[j] >= teff, 0.0, NEG_BIG)
        if keys_on_rows:
            tile = tile.T
        for r in range(rb // tq):
            madd_ref[r * tq:(r + 1) * tq, :] = tile
        tiles = [(g, r0) for g in range(N_KV_HEADS) for r0 in range(0, rows, rb)]

        for t, (g, r0) in enumerate(tiles):
            kb = k_ref[g, pl.ds(row, w), :]
            s = lax.dot_general(qs_ref[g, r0:r0 + rb, :], kb, nt, preferred_element_type=F32)
            s = s + madd_ref[...]
            if near:
                s = s + nb_ref[g, r0:r0 + rb, :]
            s_ref[t] = s
        for t, (g, r0) in enumerate(tiles):
            rs = slice(r0, r0 + rb)
            s = s_ref[t]
            m_old = m_ref[g, rs, :]
            m_new = jnp.maximum(m_old, jnp.max(s, axis=1, keepdims=True))
            p = jnp.exp2(s - jnp.concatenate([m_new] * (w // LANE), axis=1))
            vb = v_ref[g, pl.ds(row, w), :]
            acc_ref[g, rs, :] = (jnp.exp2(m_old - m_new) * acc_ref[g, rs, :]
                                 + jnp.dot(p.astype(BF16), vb, preferred_element_type=F32))
            m_ref[g, rs, :] = m_new

    attend(0, True)

    def far_body(j, carry):
        attend(j, False)
        return carry

    lax.fori_loop(1, nw, far_body, 0)

    for h in range(N_HEADS):
        g, jj = divmod(h, GQ)
        a = acc_ref[g, jj * tq:(jj + 1) * tq, :]
        o_ref[:, h * HEAD_DIM:(h + 1) * HEAD_DIM] = (
            a[:, 0:HEAD_DIM] / a[:, HEAD_DIM:HEAD_DIM + 1]).astype(o_ref.dtype)


def _rel_bucket(rel):
    half = NUM_BUCKETS // 2
    max_exact = half // 2
    base = jnp.where(rel > 0, half, 0)
    n = jnp.abs(rel)
    nf = jnp.maximum(n, 1).astype(jnp.float32)
    large = max_exact + (jnp.log(nf / max_exact) / math.log(MAX_DISTANCE / max_exact)
                         * (half - max_exact)).astype(jnp.int32)
    large = jnp.minimum(large, half - 1)
    return base + jnp.where(n < max_exact, n, large)


def _near_bias(rel_bias, tq):
    r = jnp.arange(tq, dtype=jnp.int32)[:, None]
    c = jnp.arange(KEY_WINDOW, dtype=jnp.int32)[None, :]
    rel = (tq - KEY_WINDOW + c) - r
    tab = rel_bias.astype(F32)
    far = _rel_bucket(jnp.full((1,), -(1 << 20), jnp.int32))
    far_row = jnp.dot(jax.nn.one_hot(far, NUM_BUCKETS, dtype=F32), tab, precision=lax.Precision.HIGHEST)
    pick = jax.nn.one_hot(_rel_bucket(rel), NUM_BUCKETS, dtype=F32)
    nb = jnp.einsum('qcb,bh->hqc', pick, (tab - far_row) * LOG2E, precision=lax.Precision.HIGHEST)
    return nb.reshape(N_KV_HEADS, GQ * tq, KEY_WINDOW)


def _dsa_mixer(projb, kiwi, past_k, past_v, past_ik, rel_bias, b, t):
    past = past_k.shape[1]
    n_keys = past + t
    topk = min(TOPK_MAX, n_keys // 4)
    tq = 128 if t % 128 == 0 else t
    assert KEY_WINDOW % tq == 0 and past % KEY_WINDOW == 0 and tq % 16 == 0 and n_keys >= KEY_WINDOW
    assert t % min(t, KEY_WINDOW) == 0
    front_pad = KEY_WINDOW - tq
    lp = front_pad + n_keys
    nq = t // tq
    nw_max = (n_keys + KEY_WINDOW - 1) // KEY_WINDOW
    rows = GQ * tq
    rb = min(rows, MAX_TILE_ROWS)
    keys_on_rows = tq % LANE == 0
    assert keys_on_rows or IDX_HEADS * tq <= MAX_TILE_ROWS
    kvw = N_KV_HEADS * HEAD_DIM
    qw = N_HEADS * HEAD_DIM
    qiw = IDX_HEADS * IDX_DIM

    qrow = lambda bi, i: (bi * nq + i, 0)
    in_specs = [pl.BlockSpec((tq, qw), qrow),
                pl.BlockSpec((tq, qiw), lambda bi, i: (bi * nq + i, QI_OFF // qiw)),
                pl.BlockSpec((tq, LANE), qrow),
                pl.BlockSpec((t, kvw), lambda bi, i: (bi, K_OFF // kvw)),
                pl.BlockSpec((t, kvw), lambda bi, i: (bi, V_OFF // kvw)),
                pl.BlockSpec((t, LANE), lambda bi, i: (bi, KI_OFF // LANE))]
    args = [projb, projb, kiwi, projb, projb, projb]
    if past:
        in_specs += [pl.BlockSpec((None, past, kvw), lambda bi, i: (bi, 0, 0)),
                     pl.BlockSpec((None, past, kvw), lambda bi, i: (bi, 0, 0)),
                     pl.BlockSpec((None, past, IDX_DIM), lambda bi, i: (bi, 0, 0))]
        args += [past_k.reshape(b, past, kvw), past_v.reshape(b, past, kvw), past_ik]
    in_specs.append(pl.BlockSpec((N_KV_HEADS, rows, KEY_WINDOW), lambda bi, i: (0, 0, 0)))
    args.append(_near_bias(rel_bias, tq))

    kern = functools.partial(_attn_kernel, tq=tq, t=t, past=past, front_pad=front_pad, topk=topk)
    return pl.pallas_call(
        kern,
        out_shape=jax.ShapeDtypeStruct((b * t, qw), BF16),
        grid=(b, nq),
        in_specs=in_specs,
        out_specs=pl.BlockSpec((tq, qw), qrow),
        scratch_shapes=[pltpu.VMEM((N_KV_HEADS, lp, HEAD_DIM), BF16),
                        pltpu.VMEM((N_KV_HEADS, lp, LANE), BF16),
                        pltpu.VMEM((lp, IDX_DIM), BF16),
                        pltpu.VMEM((nw_max, KEY_WINDOW, tq) if keys_on_rows
                                   else (nw_max, tq, KEY_WINDOW), jnp.int32),
                        pltpu.VMEM((N_KV_HEADS, rows, HEAD_DIM), BF16),
                        pltpu.VMEM((IDX_HEADS // 2, 2 * tq, IDX_DIM) if keys_on_rows
                                   else (1, IDX_HEADS * tq, IDX_DIM), BF16),
                        pltpu.VMEM((8, 1) if keys_on_rows else (IDX_HEADS * tq, 1), F32),
                        pltpu.VMEM((rb, KEY_WINDOW), F32),
                        pltpu.VMEM((N_KV_HEADS * rows // rb, rb, KEY_WINDOW), F32),
                        pltpu.VMEM((N_KV_HEADS, rows, LANE), F32),
                        pltpu.VMEM((N_KV_HEADS, rows, LANE), F32)],
        compiler_params=_params("arbitrary", "arbitrary"),
        name="dsa_attention",
    )(*args)


def _ssd_kernel(z_ref, xlo_ref, xhi_ref, dt_ref, cst_ref, st0_ref, cw_ref, cb_ref, dtb_ref, alog_ref,
                dexp_ref, nw_ref, expand_ref, y_ref, ncv_ref, stout_ref,
                xpad_ref, act_ref, yd_ref, st_ref, *, q):
    c = pl.program_id(1)
    half = CONV_DIM // 2

    @pl.when(c == 0)
    def _():
        xpad_ref[8 - (D_CONV - 1):8, :] = cst_ref[...]
        for h in range(SSM_HEADS):
            g, jj = divmod(h, HEADS_PER_GROUP)
            st_ref[g, :, jj * SSM_HEAD_DIM:(jj + 1) * SSM_HEAD_DIM] = st0_ref[h].T

    cc = 512
    for c0 in range(0, CONV_DIM, cc):
        cs = slice(c0, c0 + cc)
        src = xlo_ref[:, c0:c0 + cc] if c0 < half else xhi_ref[:, c0 - half:c0 - half + cc]
        xpad_ref[8:8 + q, cs] = src
        conv = cb_ref[:, cs] + xpad_ref[5:5 + q, cs] * cw_ref[0:1, cs]
        conv = conv + xpad_ref[6:6 + q, cs] * cw_ref[1:2, cs]
        conv = conv + xpad_ref[7:7 + q, cs] * cw_ref[2:3, cs]
        conv = conv + xpad_ref[8:8 + q, cs] * cw_ref[3:4, cs]
        act_ref[:, cs] = conv * _sigmoid(conv)
        xpad_ref[0:8, cs] = xpad_ref[q:q + 8, cs]

    hi = lax.Precision.HIGHEST
    x_dt = dt_ref[:, 0:SSM_HEADS] + dtb_ref[...]
    dt = jnp.maximum(x_dt, 0.0) + jnp.log1p(jnp.exp(-jnp.abs(x_dt)))
    a = dt * (-jnp.exp(alog_ref[...]))
    rr = lax.broadcasted_iota(jnp.int32, (q, q), 0)
    cl = lax.broadcasted_iota(jnp.int32, (q, q), 1)
    causal = rr >= cl
    tri = jnp.where(causal, 1.0, 0.0)
    a_cs = jnp.dot(tri, a, precision=hi, preferred_element_type=F32)
    eye = jnp.where(lax.broadcasted_iota(jnp.int32, (SSM_HEADS, SSM_HEADS), 0)
                    == lax.broadcasted_iota(jnp.int32, (SSM_HEADS, SSM_HEADS), 1), 1.0, 0.0)
    a_cs_t = lax.dot_general(eye, a_cs, (((1,), (1,)), ((), ())), precision=hi,
                             preferred_element_type=F32)
    a_end = a_cs[q - 1:q, :]
    per_head = jnp.concatenate([dt, jnp.exp(a_cs), jnp.exp(a_end - a_cs)], axis=0)
    top = per_head.astype(BF16)
    rest = (per_head - top.astype(F32)).astype(BF16)
    spread = jnp.dot(jnp.concatenate([top, rest], axis=0), expand_ref[...],
                     preferred_element_type=F32)
    dt_x = spread[0:q] + spread[3 * q:4 * q]
    e_x = spread[q:2 * q] + spread[4 * q:5 * q]
    dte_x = spread[2 * q:3 * q] + spread[5 * q:6 * q]
    dec_x = e_x[q - 1:q, :]

    for g in range(SSM_GROUPS):
        gs = slice(g * GROUP_W, (g + 1) * GROUP_W)
        bg = act_ref[:, D_INNER + g * D_STATE:D_INNER + (g + 1) * D_STATE].astype(BF16)
        cg = act_ref[:, D_INNER + (SSM_GROUPS + g) * D_STATE:
                     D_INNER + (SSM_GROUPS + g + 1) * D_STATE].astype(BF16)
        cb = lax.dot_general(cg, bg, (((1,), (1,)), ((), ())), preferred_element_type=F32)
        xs_g = act_ref[:, gs]
        xd_g = xs_g * dt_x[:, gs]
        st_g = st_ref[g]
        y_off = jnp.dot(cg, st_g.astype(BF16), preferred_element_type=F32) * e_x[:, gs]
        for jj in range(HEADS_PER_GROUP):
            h = g * HEADS_PER_GROUP + jj
            seg = a_cs[:, h:h + 1] - a_cs_t[h:h + 1, :]
            decay = jnp.where(causal, jnp.exp(seg), 0.0)
            hs = slice(jj * SSM_HEAD_DIM, (jj + 1) * SSM_HEAD_DIM)
            yd_ref[:, hs] = jnp.dot((cb * decay).astype(BF16), xd_g[:, hs].astype(BF16),
                                    preferred_element_type=F32)
        st_ref[g] = st_g * dec_x[:, gs] + lax.dot_general(
            bg, (xd_g * dte_x[:, gs]).astype(BF16), (((0,), (0,)), ((), ())),
            preferred_element_type=F32)
        y = yd_ref[...] + y_off + xs_g * dexp_ref[:, gs]
        zg = z_ref[:, gs]
        yg = y * (zg * _sigmoid(zg))
        ms = jnp.mean(yg * yg, axis=-1, keepdims=True)
        y_ref[:, gs] = (yg * lax.rsqrt(ms + EPS) * nw_ref[:, gs]).astype(y_ref.dtype)

    @pl.when(c == pl.num_programs(1) - 1)
    def _():
        ncv_ref[...] = xpad_ref[8 - (D_CONV - 1):8, :]
        for h in range(SSM_HEADS):
            g, jj = divmod(h, HEADS_PER_GROUP)
            stout_ref[h] = st_ref[g, :, jj * SSM_HEAD_DIM:(jj + 1) * SSM_HEAD_DIM].T


def _ssd_mixer(proj, conv_state, ssm_state, conv_w, conv_b, dt_bias, a_log, d_skip, norm_w, b, t):
    q = 128 if t % 128 == 0 else t
    assert q % 8 == 0 and t >= D_CONV - 1 and CONV_DIM == 2 * D_INNER
    nc = t // q
    expand = jnp.repeat(jnp.eye(SSM_HEADS, dtype=BF16), SSM_HEAD_DIM, axis=1)
    dexp = jnp.repeat(d_skip.astype(F32), SSM_HEAD_DIM)[None, :]
    row = lambda k: (lambda bi, c: (bi * nc + c, k))
    const2 = lambda bi, c: (0, 0)
    state_spec = pl.BlockSpec((None, SSM_HEADS, SSM_HEAD_DIM, D_STATE), lambda bi, c: (bi, 0, 0, 0))
    conv_spec = pl.BlockSpec((None, D_CONV - 1, CONV_DIM), lambda bi, c: (bi, 0, 0))
    y, new_conv, new_ssm = pl.pallas_call(
        functools.partial(_ssd_kernel, q=q),
        out_shape=[jax.ShapeDtypeStruct((b * t, D_INNER), BF16),
                   jax.ShapeDtypeStruct((b, D_CONV - 1, CONV_DIM), F32),
                   jax.ShapeDtypeStruct((b, SSM_HEADS, SSM_HEAD_DIM, D_STATE), F32)],
        grid=(b, nc),
        in_specs=[pl.BlockSpec((q, D_INNER), row(SSM_Z_OFF // D_INNER)),
                  pl.BlockSpec((q, D_INNER), row(SSM_XBC_OFF // D_INNER)),
                  pl.BlockSpec((q, D_INNER), row(SSM_XBC_OFF // D_INNER + 1)),
                  pl.BlockSpec((q, LANE), row(SSM_DT_OFF // LANE)),
                  conv_spec,
                  state_spec,
                  pl.BlockSpec((D_CONV, CONV_DIM), const2),
                  pl.BlockSpec((1, CONV_DIM), const2),
                  pl.BlockSpec((1, SSM_HEADS), const2),
                  pl.BlockSpec((1, SSM_HEADS), const2),
                  pl.BlockSpec((1, D_INNER), const2),
                  pl.BlockSpec((1, D_INNER), const2),
                  pl.BlockSpec((SSM_HEADS, D_INNER), const2)],
        out_specs=[pl.BlockSpec((q, D_INNER), row(0)), conv_spec, state_spec],
        scratch_shapes=[pltpu.VMEM((q + 8, CONV_DIM), F32),
                        pltpu.VMEM((q, CONV_DIM), F32),
                        pltpu.VMEM((q, GROUP_W), F32),
                        pltpu.VMEM((SSM_GROUPS, D_STATE, GROUP_W), F32)],
        compiler_params=_params("arbitrary", "arbitrary"),
        name="ssd_mixer",
    )(proj, proj, proj, proj, conv_state.astype(F32), ssm_state.astype(F32), conv_w.astype(F32),
      conv_b.astype(F32)[None, :], dt_bias.astype(F32)[None, :], a_log.astype(F32)[None, :], dexp,
      norm_w.astype(F32)[None, :], expand)
    return y, new_conv, new_ssm.astype(ssm_state.dtype)


def _tile_m(m):
    for tm in (1024, 512, 256, 128, 64, 32, 16, 8):
        if m % tm == 0:
            return tm
    raise ValueError(m)


def _pad_cols(w, n):
    return jnp.pad(w, ((0, 0), (0, n - w.shape[1])))


def _prep_weights(norm_mix, norm_ffn, norm_final, attn_w_in, attn_w_o, ssm_w_in, ssm_w_out,
                  mlp_w_up, mlp_w_down):
    col_scale = jnp.where(jnp.arange(ATTN_COLS) < K_OFF, ATTN_SCALE * LOG2E, 1.0).astype(F32)
    return dict(
        attn_in=_pad_cols(attn_w_in[0] * col_scale, _round_up(ATTN_COLS, LANE)).astype(BF16),
        attn_o=attn_w_o[0].astype(BF16),
        ssm_in=_pad_cols(ssm_w_in[0], _round_up(SSM_COLS, LANE)).astype(BF16),
        ssm_out=ssm_w_out[0].astype(BF16),
        up=[mlp_w_up[i].astype(BF16) for i in range(2)],
        down=[mlp_w_down[i].astype(BF16) for i in range(2)],
        g_mix=[norm_mix[i].astype(F32)[None, :] for i in range(2)],
        g_ffn=[norm_ffn[i].astype(F32)[None, :] for i in range(2)],
        g_final=norm_final.astype(F32)[None, :],
    )


def _trunk(x, past_k, past_v, past_ik, conv_st, ssm_st, rel_bias, wts, ssm_conv_w, ssm_conv_b,
           ssm_dt_bias, ssm_a_log, ssm_d, ssm_norm):
    b, t, d = x.shape
    m = b * t
    tm = _tile_m(m)
    x2 = x.reshape(m, d).astype(F32)

    projb, k_new, v_new, ki_new, kiwi = _attn_proj(x2, wts["g_mix"][0], wts["attn_in"], tm=min(tm, 512))
    ao = _dsa_mixer(projb, kiwi, past_k, past_v, past_ik, rel_bias, b, t)
    x2 = _layer_tail(ao, wts["attn_o"], x2, wts["g_ffn"][0], wts["up"][0], wts["down"][0],
                     wts["g_final"], tm=min(tm, 512), tf=1024, final_norm=False)

    proj = _norm_matmul(x2, wts["g_mix"][1], wts["ssm_in"], tm=min(tm, 256), tn=wts["ssm_in"].shape[1])
    y, new_conv, new_ssm = _ssd_mixer(proj, conv_st, ssm_st, ssm_conv_w, ssm_conv_b, ssm_dt_bias,
                                      ssm_a_log, ssm_d, ssm_norm, b, t)
    x2 = _layer_tail(y, wts["ssm_out"], x2, wts["g_ffn"][1], wts["up"][1], wts["down"][1],
                     wts["g_final"], tm=min(tm, 512), tf=1024, final_norm=True)

    dt = x.dtype
    return (x2.reshape(b, t, d).astype(dt),
            k_new.reshape(1, b, t, N_KV_HEADS, HEAD_DIM).astype(dt),
            v_new.reshape(1, b, t, N_KV_HEADS, HEAD_DIM).astype(dt),
            ki_new.reshape(1, b, t, IDX_DIM).astype(dt), new_conv[None].astype(dt), new_ssm[None])


def kernel(x_prompt, x_sample, cache_k, cache_v, cache_idx_k, state_conv, state_ssm, rel_bias, norm_mix, norm_ffn, norm_final, attn_w_in, attn_w_o, ssm_w_in, ssm_conv_w, ssm_conv_b, ssm_dt_bias, ssm_a_log, ssm_d, ssm_norm, ssm_w_out, mlp_w_up, mlp_w_down):
    wts = _prep_weights(norm_mix, norm_ffn, norm_final, attn_w_in, attn_w_o, ssm_w_in, ssm_w_out,
                        mlp_w_up, mlp_w_down)
    bp = x_prompt.shape[0]
    dtp = x_prompt.dtype
    empty_k = jnp.zeros((bp, 0, N_KV_HEADS, HEAD_DIM), dtp)
    empty_ik = jnp.zeros((bp, 0, IDX_DIM), dtp)
    zero_conv = jnp.zeros((bp, D_CONV - 1, CONV_DIM), dtp)
    zero_ssm = jnp.zeros((bp, SSM_HEADS, SSM_HEAD_DIM, D_STATE), dtp)
    args = (rel_bias, wts, ssm_conv_w[0], ssm_conv_b[0], ssm_dt_bias[0], ssm_a_log[0], ssm_d[0],
            ssm_norm[0])
    yp, kp, vp, ikp, cp, sp = _trunk(x_prompt, empty_k, empty_k, empty_ik, zero_conv, zero_ssm, *args)
    ys, ks, vs, iks, cs, ss = _trunk(x_sample, cache_k[0], cache_v[0], cache_idx_k[0],
                                     state_conv[0], state_ssm[0], *args)
    return (yp, ys, kp, vp, ikp, cp, sp, ks, vs, iks, cs, ss)
```

```python
import functools
import math

import jax
import jax.numpy as jnp
from jax import lax
from jax.experimental import pallas as pl
from jax.experimental.pallas import tpu as pltpu

F32 = jnp.float32
BF16 = jnp.bfloat16

D_MODEL = 1024
CHUNK = 64
CHUNK_SHIFT = 6
N_HEADS = 16
HEAD_DIM = 64
N_KV_HEADS = 4
GQ = N_HEADS // N_KV_HEADS
IDX_HEADS = 8
IDX_DIM = 64
TOPK_MAX = 256
IDX_SCALE = (IDX_HEADS * IDX_DIM) ** -0.5
ATTN_SCALE = HEAD_DIM ** -0.5
NUM_BUCKETS = 32
MAX_DISTANCE = 128
D_INNER = 2 * D_MODEL
SSM_HEAD_DIM = 64
SSM_HEADS = D_INNER // SSM_HEAD_DIM
SSM_GROUPS = 8
HEADS_PER_GROUP = SSM_HEADS // SSM_GROUPS
GROUP_W = HEADS_PER_GROUP * SSM_HEAD_DIM
D_STATE = 128
D_CONV = 4
CONV_DIM = D_INNER + 2 * SSM_GROUPS * D_STATE
D_FF = 4 * D_MODEL
EPS = 1e-6

Q_OFF = 0
K_OFF = N_HEADS * HEAD_DIM
V_OFF = K_OFF + N_KV_HEADS * HEAD_DIM
QI_OFF = V_OFF + N_KV_HEADS * HEAD_DIM
KI_OFF = QI_OFF + IDX_HEADS * IDX_DIM
WI_OFF = KI_OFF + IDX_DIM
ATTN_COLS = WI_OFF + IDX_HEADS

SSM_Z_OFF = 0
SSM_XBC_OFF = D_INNER
SSM_DT_OFF = D_INNER + CONV_DIM
SSM_COLS = SSM_DT_OFF + SSM_HEADS

LANE = 128
KEY_WINDOW = 512
MAX_TILE_ROWS = 128
LOG2E = 1.4426950408889634
VMEM_LIMIT = 56 * 1024 * 1024

NEG_BIG = -1e30
KEY_NEG_INF = -2139095041
INT_MIN = -2147483648


def _round_up(n, m):
    return (n + m - 1) // m * m


def _rms(x, g):
    ms = jnp.mean(x * x, axis=-1, keepdims=True)
    return x * lax.rsqrt(ms + EPS) * g


def _sigmoid(x):
    return 1.0 / (1.0 + jnp.exp(-x))


def _params(*sem):
    return pltpu.CompilerParams(dimension_semantics=sem, vmem_limit_bytes=VMEM_LIMIT)


def _ssm_proj_kernel(x_ref, g_ref, w_ref, cst_ref, cw_ref, cb_ref, z_ref, act_ref, dt_ref, ncv_ref,
                     xpad_ref, *, tm, tiles_per_stream):
    keep = D_CONV - 1
    xn = _rms(x_ref[...], g_ref[...]).astype(BF16)
    z_ref[...] = jnp.dot(xn, w_ref[:, SSM_Z_OFF:SSM_Z_OFF + D_INNER], preferred_element_type=F32)
    dt_ref[...] = jnp.dot(xn, w_ref[:, SSM_DT_OFF:SSM_DT_OFF + LANE], preferred_element_type=F32)

    @pl.when(pl.program_id(0) % tiles_per_stream == 0)
    def _():
        xpad_ref[8 - keep:8, :] = cst_ref[...]

    cc = 512
    for c0 in range(0, CONV_DIM, cc):
        cs = slice(c0, c0 + cc)
        xpad_ref[8:8 + tm, cs] = jnp.dot(xn, w_ref[:, SSM_XBC_OFF + c0:SSM_XBC_OFF + c0 + cc],
                                         preferred_element_type=F32)
        conv = cb_ref[:, cs] + xpad_ref[5:5 + tm, cs] * cw_ref[0:1, cs]
        conv = conv + xpad_ref[6:6 + tm, cs] * cw_ref[1:2, cs]
        conv = conv + xpad_ref[7:7 + tm, cs] * cw_ref[2:3, cs]
        conv = conv + xpad_ref[8:8 + tm, cs] * cw_ref[3:4, cs]
        act_ref[:, cs] = conv * _sigmoid(conv)
        xpad_ref[0:8, cs] = xpad_ref[tm:tm + 8, cs]
    ncv_ref[...] = xpad_ref[8 - keep:8, :]


def _ssm_proj(x, g, w, conv_state, conv_w, conv_b, *, b, t, tm):
    m, k = x.shape
    assert t % tm == 0 and tm % 8 == 0 and D_CONV == 4
    tps = t // tm
    row = lambda i: (i, 0)
    whole = lambda shape: pl.BlockSpec(shape, lambda i: (0, 0), pipeline_mode=pl.Buffered(1))
    conv_spec = pl.BlockSpec((None, D_CONV - 1, CONV_DIM), lambda i: (i // tps, 0, 0))
    return pl.pallas_call(
        functools.partial(_ssm_proj_kernel, tm=tm, tiles_per_stream=tps),
        out_shape=[jax.ShapeDtypeStruct((m, D_INNER), F32),
                   jax.ShapeDtypeStruct((m, CONV_DIM), F32),
                   jax.ShapeDtypeStruct((m, LANE), F32),
                   jax.ShapeDtypeStruct((b, D_CONV - 1, CONV_DIM), F32)],
        grid=(m // tm,),
        in_specs=[pl.BlockSpec((tm, k), row), whole((1, k)), whole(w.shape), conv_spec,
                  whole((D_CONV, CONV_DIM)), whole((1, CONV_DIM))],
        out_specs=[pl.BlockSpec((tm, D_INNER), row), pl.BlockSpec((tm, CONV_DIM), row),
                   pl.BlockSpec((tm, LANE), row), conv_spec],
        scratch_shapes=[pltpu.VMEM((tm + 8, CONV_DIM), F32)],
        compiler_params=_params("arbitrary"),
        name="ssm_proj",
    )(x, g, w, conv_state, conv_w, conv_b)


def _attn_proj_kernel(x_ref, g_ref, w_ref, pb_ref, k_ref, v_ref, ki_ref, kiwi_ref):
    xn = _rms(x_ref[...], g_ref[...]).astype(BF16)
    acc = jnp.dot(xn, w_ref[...], preferred_element_type=F32)
    pb_ref[...] = acc.astype(BF16)
    k_ref[...] = acc[:, K_OFF:V_OFF]
    v_ref[...] = acc[:, V_OFF:QI_OFF]
    ki_ref[...] = acc[:, KI_OFF:WI_OFF]
    kiwi_ref[...] = acc[:, KI_OFF:KI_OFF + LANE]


def _attn_proj(x, g, w, *, tm):
    m, k = x.shape
    n = w.shape[1]
    kvw = N_KV_HEADS * HEAD_DIM
    row = lambda i: (i, 0)
    return pl.pallas_call(
        _attn_proj_kernel,
        out_shape=[jax.ShapeDtypeStruct((m, n), BF16),
                   jax.ShapeDtypeStruct((m, kvw), F32),
                   jax.ShapeDtypeStruct((m, kvw), F32),
                   jax.ShapeDtypeStruct((m, IDX_DIM), F32),
                   jax.ShapeDtypeStruct((m, LANE), F32)],
        grid=(m // tm,),
        in_specs=[pl.BlockSpec((tm, k), row),
                  pl.BlockSpec((1, k), lambda i: (0, 0)),
                  pl.BlockSpec((k, n), lambda i: (0, 0))],
        out_specs=[pl.BlockSpec((tm, n), row), pl.BlockSpec((tm, kvw), row), pl.BlockSpec((tm, kvw), row),
                   pl.BlockSpec((tm, IDX_DIM), row), pl.BlockSpec((tm, LANE), row)],
        compiler_params=_params("parallel"),
        name="attn_proj",
    )(x, g, w)


def _tail_kernel(a_ref, wo_ref, r_ref, g_ref, wu_ref, wd_ref, gf_ref, o_ref, *, tf, final_norm):
    x1 = r_ref[...] + jnp.dot(a_ref[...], wo_ref[...], preferred_element_type=F32)
    xn = _rms(x1, g_ref[...]).astype(BF16)
    acc = x1
    for f0 in range(0, wu_ref.shape[1], tf):
        h = jnp.maximum(jnp.dot(xn, wu_ref[:, f0:f0 + tf], preferred_element_type=F32), 0.0)
        acc = acc + jnp.dot((h * h).astype(BF16), wd_ref[f0:f0 + tf, :], preferred_element_type=F32)
    if final_norm:
        acc = _rms(acc, gf_ref[...])
    o_ref[...] = acc


def _layer_tail(a, wo, resid, g, wu, wd, gf, *, tm, tf, final_norm):
    m, ka = a.shape
    d = wo.shape[1]
    ff = wu.shape[1]
    row = lambda i: (i, 0)
    whole = lambda shape: pl.BlockSpec(shape, lambda i: (0, 0), pipeline_mode=pl.Buffered(1))
    return pl.pallas_call(
        functools.partial(_tail_kernel, tf=tf, final_norm=final_norm),
        out_shape=jax.ShapeDtypeStruct((m, d), F32),
        grid=(m // tm,),
        in_specs=[pl.BlockSpec((tm, ka), row), whole((ka, d)), pl.BlockSpec((tm, d), row),
                  whole((1, d)), whole((d, ff)), whole((ff, d)), whole((1, d))],
        out_specs=pl.BlockSpec((tm, d), row),
        compiler_params=_params("parallel"),
        name="layer_tail",
    )(a, wo, resid, g, wu, wd, gf)


def _attn_kernel(*refs, tq, t, past, front_pad, topk):
    if past:
        (q_ref, qi_ref, kiwi_ref, kn_ref, vn_ref, kin_ref, pk_ref, pv_ref, pki_ref, nb_ref, o_ref,
         k_ref, v_ref, ki_ref, skey_ref, qs_ref, qis_ref, wcol_ref, madd_ref, s_ref, m_ref, acc_ref) = refs
    else:
        (q_ref, qi_ref, kiwi_ref, kn_ref, vn_ref, kin_ref, nb_ref, o_ref,
         k_ref, v_ref, ki_ref, skey_ref, qs_ref, qis_ref, wcol_ref, madd_ref, s_ref, m_ref, acc_ref) = refs
    w = KEY_WINDOW
    rows = GQ * tq
    rb = madd_ref.shape[0]
    keys_on_rows = skey_ref.shape[1] == w
    kax = 0 if keys_on_rows else 1
    qshape = (1, tq) if keys_on_rows else (tq, 1)
    i = pl.program_id(1)
    q0 = past + i * tq
    lv = q0 + tq
    nw = jnp.right_shift(lv + (w - 1), 9)
    qchunk = jnp.right_shift(q0 + lax.broadcasted_iota(jnp.int32, qshape, 1 - kax), CHUNK_SHIFT)
    kidx = lax.broadcasted_iota(jnp.int32, (w, 1) if keys_on_rows else (1, w), kax)
    nt = (((1,), (1,)), ((), ()))

    def win_row(j):
        return pl.multiple_of(lv + front_pad - w * (j + 1), tq)

    @pl.when(i == 0)
    def _():
        ones_col = jnp.where(lax.broadcasted_iota(jnp.int32, (1, LANE - HEAD_DIM), 1) == 0, 1.0, 0.0)

        def put(r0, n, kc, vc, kic):
            ones = jnp.broadcast_to(ones_col, (n, LANE - HEAD_DIM)).astype(BF16)
            for g in range(N_KV_HEADS):
                gs = slice(g * HEAD_DIM, (g + 1) * HEAD_DIM)
                k_ref[g, r0:r0 + n, :] = kc[:, gs]
                v_ref[g, r0:r0 + n, :] = jnp.concatenate([vc[:, gs], ones], axis=1)
            ki_ref[r0:r0 + n, :] = kic

        kvw = N_KV_HEADS * HEAD_DIM
        if front_pad:
            put(0, front_pad, jnp.zeros((front_pad, kvw), BF16), jnp.zeros((front_pad, kvw), BF16),
                jnp.zeros((front_pad, IDX_DIM), BF16))
        step = min(t, w)
        for c0 in range(0, past, w):
            put(front_pad + c0, w, pk_ref[c0:c0 + w, :].astype(BF16), pv_ref[c0:c0 + w, :].astype(BF16),
                pki_ref[c0:c0 + w, :].astype(BF16))
        for c0 in range(0, t, step):
            put(front_pad + past + c0, step, kn_ref[c0:c0 + step, :], vn_ref[c0:c0 + step, :],
                kin_ref[c0:c0 + step, 0:IDX_DIM])

    for h in range(N_HEADS):
        g, jj = divmod(h, GQ)
        qs_ref[g, jj * tq:(jj + 1) * tq, :] = q_ref[:, h * HEAD_DIM:(h + 1) * HEAD_DIM]

    if keys_on_rows:
        wi_t = kiwi_ref[...].T[IDX_DIM:IDX_DIM + IDX_HEADS, :]
        for hp in range(IDX_HEADS // 2):
            for u in range(2):
                h = 2 * hp + u
                qis_ref[hp, u * tq:(u + 1) * tq, :] = qi_ref[:, h * IDX_DIM:(h + 1) * IDX_DIM]
    else:
        wi = kiwi_ref[:, IDX_DIM:IDX_DIM + IDX_HEADS]
        for h in range(IDX_HEADS):
            qis_ref[0, h * tq:(h + 1) * tq, :] = qi_ref[:, h * IDX_DIM:(h + 1) * IDX_DIM]
            wcol_ref[h * tq:(h + 1) * tq, :] = wi[:, h:h + 1]

    def score_body(j, carry):
        kib = ki_ref[pl.ds(win_row(j), w), :]
        if keys_on_rows:
            sc = jnp.zeros((w, tq), F32)
            for hp in range(IDX_HEADS // 2):
                s2 = jnp.maximum(lax.dot_general(kib, qis_ref[hp], nt, preferred_element_type=F32), 0.0)
                sc = sc + s2[:, 0:tq] * wi_t[2 * hp:2 * hp + 1, :]
                sc = sc + s2[:, tq:2 * tq] * wi_t[2 * hp + 1:2 * hp + 2, :]
        else:
            s8 = jnp.maximum(lax.dot_general(qis_ref[0], kib, nt, preferred_element_type=F32), 0.0)
            s8 = s8 * wcol_ref[...]
            sc = s8[0:tq]
            for h in range(1, IDX_HEADS):
                sc = sc + s8[h * tq:(h + 1) * tq]
        sc = sc * IDX_SCALE
        kpos = (lv - w * (j + 1)) + kidx
        vis = (kpos >= 0) & (jnp.right_shift(kpos, CHUNK_SHIFT) <= qchunk)
        bits = pltpu.bitcast(sc, jnp.int32)
        key = bits ^ (jnp.right_shift(bits, 31) & 0x7FFFFFFF)
        key = jnp.where(sc == 0.0, 0, key)
        skey_ref[j] = jnp.where(vis, key, KEY_NEG_INF)
        return carry

    lax.fori_loop(0, nw, score_body, 0)

    if keys_on_rows:
        part = (4 * 8, tq)

        def fold(c):
            return jnp.sum(c.reshape(w // part[0], part[0], tq), axis=0)
    else:
        part = (tq, LANE)

        def fold(c):
            return (c[:, 0:LANE] + c[:, LANE:2 * LANE]) + (c[:, 2 * LANE:3 * LANE] + c[:, 3 * LANE:4 * LANE])

    def total(acc):
        return jnp.sum(acc, axis=kax, keepdims=True)

    def count_ge(cand):
        def body(j, acc):
            return acc + fold(jnp.where(skey_ref[j] >= cand, 1.0, 0.0))
        return total(lax.fori_loop(0, nw, body, jnp.zeros(part, F32)))

    def search_body(it, prefix):
        cand = prefix + lax.shift_left(jnp.int32(1), 31 - it)
        return jnp.where(count_ge(cand) >= float(topk), cand, prefix)

    tstar = lax.fori_loop(0, 32, search_body, jnp.full(qshape, INT_MIN, jnp.int32))

    def tie_count_body(j, carry):
        ag, ae = carry
        kt = skey_ref[j]
        return (ag + fold(jnp.where(kt > tstar, 1.0, 0.0)),
                ae + fold(jnp.where(kt == tstar, 1.0, 0.0)))

    ag, ae = lax.fori_loop(0, nw, tie_count_body, (jnp.zeros(part, F32), jnp.zeros(part, F32)))
    need = float(topk) - total(ag)
    n_eq = total(ae)
    bad = (n_eq > need) & (tstar > KEY_NEG_INF)
    n_bad = jnp.sum(jnp.where(bad, 1.0, 0.0))

    @pl.when(n_bad > 0.0)
    def _():
        r_i = lax.broadcasted_iota(jnp.int32, (w, w), 0)
        c_i = lax.broadcasted_iota(jnp.int32, (w, w), 1)
        earlier = jnp.where(c_i < r_i if keys_on_rows else r_i < c_i, 1.0, 0.0).astype(BF16)

        def tie_body(t, seen):
            j = nw - 1 - t
            kt = skey_ref[j]
            eq = kt == tstar
            eqf = jnp.where(eq, 1.0, 0.0)
            if keys_on_rows:
                inwin = jnp.dot(earlier, eqf.astype(BF16), preferred_element_type=F32)
            else:
                inwin = jnp.dot(eqf.astype(BF16), earlier, preferred_element_type=F32)
            skey_ref[j] = jnp.where(eq & bad & (seen + inwin >= need), KEY_NEG_INF, kt)
            return seen + total(eqf)

        lax.fori_loop(0, nw, tie_body, jnp.zeros(qshape, F32))

    teff = jnp.maximum(tstar, KEY_NEG_INF + 1)

    m_ref[...] = jnp.full(m_ref.shape, NEG_BIG, F32)
    acc_ref[...] = jnp.zeros(acc_ref.shape, F32)

    def attend(j, near):
        row = win_row(j)
        tile = jnp.where(skey_ref[j] >= teff, 0.0, NEG_BIG)
        if keys_on_rows:
            tile = tile.T
        for r in range(rb // tq):
            madd_ref[r * tq:(r + 1) * tq, :] = tile
        tiles = [(g, r0) for g in range(N_KV_HEADS) for r0 in range(0, rows, rb)]

        for t, (g, r0) in enumerate(tiles):
            kb = k_ref[g, pl.ds(row, w), :]
            s = lax.dot_general(qs_ref[g, r0:r0 + rb, :], kb, nt, preferred_element_type=F32)
            s = s + madd_ref[...]
            if near:
                s = s + nb_ref[g, r0:r0 + rb, :]
            s_ref[t] = s
        for t, (g, r0) in enumerate(tiles):
            rs = slice(r0, r0 + rb)
            s = s_ref[t]
            m_old = m_ref[g, rs, :]
            m_new = jnp.maximum(m_old, jnp.max(s, axis=1, keepdims=True))
            p = jnp.exp2(s - jnp.concatenate([m_new] * (w // LANE), axis=1))
            vb = v_ref[g, pl.ds(row, w), :]
            acc_ref[g, rs, :] = (jnp.exp2(m_old - m_new) * acc_ref[g, rs, :]
                                 + jnp.dot(p.astype(BF16), vb, preferred_element_type=F32))
            m_ref[g, rs, :] = m_new

    attend(0, True)

    def far_body(j, carry):
        attend(j, False)
        return carry

    lax.fori_loop(1, nw, far_body, 0)

    for h in range(N_HEADS):
        g, jj = divmod(h, GQ)
        a = acc_ref[g, jj * tq:(jj + 1) * tq, :]
        o_ref[:, h * HEAD_DIM:(h + 1) * HEAD_DIM] = (
            a[:, 0:HEAD_DIM] / a[:, HEAD_DIM:HEAD_DIM + 1]).astype(o_ref.dtype)


def _rel_bucket(rel):
    half = NUM_BUCKETS // 2
    max_exact = half // 2
    base = jnp.where(rel > 0, half, 0)
    n = jnp.abs(rel)
    nf = jnp.maximum(n, 1).astype(jnp.float32)
    large = max_exact + (jnp.log(nf / max_exact) / math.log(MAX_DISTANCE / max_exact)
                         * (half - max_exact)).astype(jnp.int32)
    large = jnp.minimum(large, half - 1)
    return base + jnp.where(n < max_exact, n, large)


def _near_bias(rel_bias, tq):
    r = jnp.arange(tq, dtype=jnp.int32)[:, None]
    c = jnp.arange(KEY_WINDOW, dtype=jnp.int32)[None, :]
    rel = (tq - KEY_WINDOW + c) - r
    tab = rel_bias.astype(F32)
    far = _rel_bucket(jnp.full((1,), -(1 << 20), jnp.int32))
    far_row = jnp.dot(jax.nn.one_hot(far, NUM_BUCKETS, dtype=F32), tab, precision=lax.Precision.HIGHEST)
    pick = jax.nn.one_hot(_rel_bucket(rel), NUM_BUCKETS, dtype=F32)
    nb = jnp.einsum('qcb,bh->hqc', pick, (tab - far_row) * LOG2E, precision=lax.Precision.HIGHEST)
    return nb.reshape(N_KV_HEADS, GQ * tq, KEY_WINDOW)


def _dsa_mixer(projb, kiwi, past_k, past_v, past_ik, rel_bias, b, t):
    past = past_k.shape[1]
    n_keys = past + t
    topk = min(TOPK_MAX, n_keys // 4)
    tq = 128 if t % 128 == 0 else t
    assert KEY_WINDOW % tq == 0 and past % KEY_WINDOW == 0 and tq % 16 == 0 and n_keys >= KEY_WINDOW
    assert t % min(t, KEY_WINDOW) == 0
    front_pad = KEY_WINDOW - tq
    lp = front_pad + n_keys
    nq = t // tq
    nw_max = (n_keys + KEY_WINDOW - 1) // KEY_WINDOW
    rows = GQ * tq
    rb = min(rows, MAX_TILE_ROWS)
    keys_on_rows = tq % LANE == 0
    assert keys_on_rows or IDX_HEADS * tq <= MAX_TILE_ROWS
    kvw = N_KV_HEADS * HEAD_DIM
    qw = N_HEADS * HEAD_DIM
    qiw = IDX_HEADS * IDX_DIM

    qrow = lambda bi, i: (bi * nq + i, 0)
    in_specs = [pl.BlockSpec((tq, qw), qrow),
                pl.BlockSpec((tq, qiw), lambda bi, i: (bi * nq + i, QI_OFF // qiw)),
                pl.BlockSpec((tq, LANE), qrow),
                pl.BlockSpec((t, kvw), lambda bi, i: (bi, K_OFF // kvw)),
                pl.BlockSpec((t, kvw), lambda bi, i: (bi, V_OFF // kvw)),
                pl.BlockSpec((t, LANE), lambda bi, i: (bi, KI_OFF // LANE))]
    args = [projb, projb, kiwi, projb, projb, projb]
    if past:
        in_specs += [pl.BlockSpec((None, past, kvw), lambda bi, i: (bi, 0, 0)),
                     pl.BlockSpec((None, past, kvw), lambda bi, i: (bi, 0, 0)),
                     pl.BlockSpec((None, past, IDX_DIM), lambda bi, i: (bi, 0, 0))]
        args += [past_k.reshape(b, past, kvw), past_v.reshape(b, past, kvw), past_ik]
    in_specs.append(pl.BlockSpec((N_KV_HEADS, rows, KEY_WINDOW), lambda bi, i: (0, 0, 0)))
    args.append(_near_bias(rel_bias, tq))

    kern = functools.partial(_attn_kernel, tq=tq, t=t, past=past, front_pad=front_pad, topk=topk)
    return pl.pallas_call(
        kern,
        out_shape=jax.ShapeDtypeStruct((b * t, qw), BF16),
        grid=(b, nq),
        in_specs=in_specs,
        out_specs=pl.BlockSpec((tq, qw), qrow),
        scratch_shapes=[pltpu.VMEM((N_KV_HEADS, lp, HEAD_DIM), BF16),
                        pltpu.VMEM((N_KV_HEADS, lp, LANE), BF16),
                        pltpu.VMEM((lp, IDX_DIM), BF16),
                        pltpu.VMEM((nw_max, KEY_WINDOW, tq) if keys_on_rows
                                   else (nw_max, tq, KEY_WINDOW), jnp.int32),
                        pltpu.VMEM((N_KV_HEADS, rows, HEAD_DIM), BF16),
                        pltpu.VMEM((IDX_HEADS // 2, 2 * tq, IDX_DIM) if keys_on_rows
                                   else (1, IDX_HEADS * tq, IDX_DIM), BF16),
                        pltpu.VMEM((8, 1) if keys_on_rows else (IDX_HEADS * tq, 1), F32),
                        pltpu.VMEM((rb, KEY_WINDOW), F32),
                        pltpu.VMEM((N_KV_HEADS * rows // rb, rb, KEY_WINDOW), F32),
                        pltpu.VMEM((N_KV_HEADS, rows, LANE), F32),
                        pltpu.VMEM((N_KV_HEADS, rows, LANE), F32)],
        compiler_params=_params("arbitrary", "arbitrary"),
        name="dsa_attention",
    )(*args)


def _ssd_kernel(z_ref, act_ref, dt_ref, st0_ref, dtb_ref, alog_ref, dexp_ref, nw_ref, expand_ref,
                y_ref, stout_ref, yd_ref, st_ref, *, q):
    c = pl.program_id(1)

    @pl.when(c == 0)
    def _():
        for h in range(SSM_HEADS):
            g, jj = divmod(h, HEADS_PER_GROUP)
            st_ref[g, :, jj * SSM_HEAD_DIM:(jj + 1) * SSM_HEAD_DIM] = st0_ref[h].T

    hi = lax.Precision.HIGHEST
    x_dt = dt_ref[:, 0:SSM_HEADS] + dtb_ref[...]
    dt = jnp.maximum(x_dt, 0.0) + jnp.log1p(jnp.exp(-jnp.abs(x_dt)))
    a = dt * (-jnp.exp(alog_ref[...]))
    rr = lax.broadcasted_iota(jnp.int32, (q, q), 0)
    cl = lax.broadcasted_iota(jnp.int32, (q, q), 1)
    causal = rr >= cl
    tri = jnp.where(causal, 1.0, 0.0)
    a_cs = jnp.dot(tri, a, precision=hi, preferred_element_type=F32)
    eye = jnp.where(lax.broadcasted_iota(jnp.int32, (SSM_HEADS, SSM_HEADS), 0)
                    == lax.broadcasted_iota(jnp.int32, (SSM_HEADS, SSM_HEADS), 1), 1.0, 0.0)
    a_cs_t = lax.dot_general(eye, a_cs, (((1,), (1,)), ((), ())), precision=hi,
                             preferred_element_type=F32)
    a_end = a_cs[q - 1:q, :]
    per_head = jnp.concatenate([dt, jnp.exp(a_cs), jnp.exp(a_end - a_cs)], axis=0)
    top = per_head.astype(BF16)
    rest = (per_head - top.astype(F32)).astype(BF16)
    spread = jnp.dot(jnp.concatenate([top, rest], axis=0), expand_ref[...],
                     preferred_element_type=F32)
    dt_x = spread[0:q] + spread[3 * q:4 * q]
    e_x = spread[q:2 * q] + spread[4 * q:5 * q]
    dte_x = spread[2 * q:3 * q] + spread[5 * q:6 * q]
    dec_x = e_x[q - 1:q, :]

    for g in range(SSM_GROUPS):
        gs = slice(g * GROUP_W, (g + 1) * GROUP_W)
        bg = act_ref[:, D_INNER + g * D_STATE:D_INNER + (g + 1) * D_STATE].astype(BF16)
        cg = act_ref[:, D_INNER + (SSM_GROUPS + g) * D_STATE:
                     D_INNER + (SSM_GROUPS + g + 1) * D_STATE].astype(BF16)
        cb = lax.dot_general(cg, bg, (((1,), (1,)), ((), ())), preferred_element_type=F32)
        xs_g = act_ref[:, gs]
        xd_g = xs_g * dt_x[:, gs]
        st_g = st_ref[g]
        y_off = jnp.dot(cg, st_g.astype(BF16), preferred_element_type=F32) * e_x[:, gs]
        for jj in range(HEADS_PER_GROUP):
            h = g * HEADS_PER_GROUP + jj
            seg = a_cs[:, h:h + 1] - a_cs_t[h:h + 1, :]
            decay = jnp.where(causal, jnp.exp(seg), 0.0)
            hs = slice(jj * SSM_HEAD_DIM, (jj + 1) * SSM_HEAD_DIM)
            yd_ref[:, hs] = jnp.dot((cb * decay).astype(BF16), xd_g[:, hs].astype(BF16),
                                    preferred_element_type=F32)
        st_ref[g] = st_g * dec_x[:, gs] + lax.dot_general(
            bg, (xd_g * dte_x[:, gs]).astype(BF16), (((0,), (0,)), ((), ())),
            preferred_element_type=F32)
        y = yd_ref[...] + y_off + xs_g * dexp_ref[:, gs]
        zg = z_ref[:, gs]
        yg = y * (zg * _sigmoid(zg))
        ms = jnp.mean(yg * yg, axis=-1, keepdims=True)
        y_ref[:, gs] = (yg * lax.rsqrt(ms + EPS) * nw_ref[:, gs]).astype(y_ref.dtype)

    @pl.when(c == pl.num_programs(1) - 1)
    def _():
        for h in range(SSM_HEADS):
            g, jj = divmod(h, HEADS_PER_GROUP)
            stout_ref[h] = st_ref[g, :, jj * SSM_HEAD_DIM:(jj + 1) * SSM_HEAD_DIM].T


def _ssd_mixer(z, act, dt, ssm_state, dt_bias, a_log, d_skip, norm_w, b, t):
    q = 128 if t % 128 == 0 else t
    assert q % 8 == 0
    nc = t // q
    expand = jnp.repeat(jnp.eye(SSM_HEADS, dtype=BF16), SSM_HEAD_DIM, axis=1)
    dexp = jnp.repeat(d_skip.astype(F32), SSM_HEAD_DIM)[None, :]
    row = lambda bi, c: (bi * nc + c, 0)
    const2 = lambda bi, c: (0, 0)
    state_spec = pl.BlockSpec((None, SSM_HEADS, SSM_HEAD_DIM, D_STATE), lambda bi, c: (bi, 0, 0, 0))
    y, new_ssm = pl.pallas_call(
        functools.partial(_ssd_kernel, q=q),
        out_shape=[jax.ShapeDtypeStruct((b * t, D_INNER), BF16),
                   jax.ShapeDtypeStruct((b, SSM_HEADS, SSM_HEAD_DIM, D_STATE), F32)],
        grid=(b, nc),
        in_specs=[pl.BlockSpec((q, D_INNER), row),
                  pl.BlockSpec((q, CONV_DIM), row),
                  pl.BlockSpec((q, LANE), row),
                  state_spec,
                  pl.BlockSpec((1, SSM_HEADS), const2),
                  pl.BlockSpec((1, SSM_HEADS), const2),
                  pl.BlockSpec((1, D_INNER), const2),
                  pl.BlockSpec((1, D_INNER), const2),
                  pl.BlockSpec((SSM_HEADS, D_INNER), const2)],
        out_specs=[pl.BlockSpec((q, D_INNER), row), state_spec],
        scratch_shapes=[pltpu.VMEM((q, GROUP_W), F32),
                        pltpu.VMEM((SSM_GROUPS, D_STATE, GROUP_W), F32)],
        compiler_params=_params("arbitrary", "arbitrary"),
        name="ssd_mixer",
    )(z, act, dt, ssm_state.astype(F32), dt_bias.astype(F32)[None, :], a_log.astype(F32)[None, :], dexp,
      norm_w.astype(F32)[None, :], expand)
    return y, new_ssm.astype(ssm_state.dtype)


def _tile_m(m):
    for tm in (1024, 512, 256, 128, 64, 32, 16, 8):
        if m % tm == 0:
            return tm
    raise ValueError(m)


def _pad_cols(w, n):
    return jnp.pad(w, ((0, 0), (0, n - w.shape[1])))


def _prep_weights(norm_mix, norm_ffn, norm_final, attn_w_in, attn_w_o, ssm_w_in, ssm_w_out,
                  mlp_w_up, mlp_w_down):
    col_scale = jnp.where(jnp.arange(ATTN_COLS) < K_OFF, ATTN_SCALE * LOG2E, 1.0).astype(F32)
    return dict(
        attn_in=_pad_cols(attn_w_in[0] * col_scale, _round_up(ATTN_COLS, LANE)).astype(BF16),
        attn_o=attn_w_o[0].astype(BF16),
        ssm_in=_pad_cols(ssm_w_in[0], _round_up(SSM_COLS, LANE)).astype(BF16),
        ssm_out=ssm_w_out[0].astype(BF16),
        up=[mlp_w_up[i].astype(BF16) for i in range(2)],
        down=[mlp_w_down[i].astype(BF16) for i in range(2)],
        g_mix=[norm_mix[i].astype(F32)[None, :] for i in range(2)],
        g_ffn=[norm_ffn[i].astype(F32)[None, :] for i in range(2)],
        g_final=norm_final.astype(F32)[None, :],
    )


def _trunk(x, past_k, past_v, past_ik, conv_st, ssm_st, rel_bias, wts, ssm_conv_w, ssm_conv_b,
           ssm_dt_bias, ssm_a_log, ssm_d, ssm_norm):
    b, t, d = x.shape
    m = b * t
    tm = _tile_m(m)
    x2 = x.reshape(m, d).astype(F32)

    projb, k_new, v_new, ki_new, kiwi = _attn_proj(x2, wts["g_mix"][0], wts["attn_in"], tm=min(tm, 512))
    ao = _dsa_mixer(projb, kiwi, past_k, past_v, past_ik, rel_bias, b, t)
    x2 = _layer_tail(ao, wts["attn_o"], x2, wts["g_ffn"][0], wts["up"][0], wts["down"][0],
                     wts["g_final"], tm=min(tm, 512), tf=1024, final_norm=False)

    z, act, dt_raw, new_conv = _ssm_proj(x2, wts["g_mix"][1], wts["ssm_in"], conv_st.astype(F32),
                                         ssm_conv_w.astype(F32), ssm_conv_b.astype(F32)[None, :],
                                         b=b, t=t, tm=min(t, 256))
    y, new_ssm = _ssd_mixer(z, act, dt_raw, ssm_st, ssm_dt_bias, ssm_a_log, ssm_d, ssm_norm, b, t)
    x2 = _layer_tail(y, wts["ssm_out"], x2, wts["g_ffn"][1], wts["up"][1], wts["down"][1],
                     wts["g_final"], tm=min(tm, 512), tf=1024, final_norm=True)

    dt = x.dtype
    return (x2.reshape(b, t, d).astype(dt),
            k_new.reshape(1, b, t, N_KV_HEADS, HEAD_DIM).astype(dt),
            v_new.reshape(1, b, t, N_KV_HEADS, HEAD_DIM).astype(dt),
            ki_new.reshape(1, b, t, IDX_DIM).astype(dt), new_conv[None].astype(dt), new_ssm[None])


def kernel(x_prompt, x_sample, cache_k, cache_v, cache_idx_k, state_conv, state_ssm, rel_bias, norm_mix, norm_ffn, norm_final, attn_w_in, attn_w_o, ssm_w_in, ssm_conv_w, ssm_conv_b, ssm_dt_bias, ssm_a_log, ssm_d, ssm_norm, ssm_w_out, mlp_w_up, mlp_w_down):
    wts = _prep_weights(norm_mix, norm_ffn, norm_final, attn_w_in, attn_w_o, ssm_w_in, ssm_w_out,
                        mlp_w_up, mlp_w_down)
    bp = x_prompt.shape[0]
    dtp = x_prompt.dtype
    empty_k = jnp.zeros((bp, 0, N_KV_HEADS, HEAD_DIM), dtp)
    empty_ik = jnp.zeros((bp, 0, IDX_DIM), dtp)
    zero_conv = jnp.zeros((bp, D_CONV - 1, CONV_DIM), dtp)
    zero_ssm = jnp.zeros((bp, SSM_HEADS, SSM_HEAD_DIM, D_STATE), dtp)
    args = (rel_bias, wts, ssm_conv_w[0], ssm_conv_b[0], ssm_dt_bias[0], ssm_a_log[0], ssm_d[0],
            ssm_norm[0])
    yp, kp, vp, ikp, cp, sp = _trunk(x_prompt, empty_k, empty_k, empty_ik, zero_conv, zero_ssm, *args)
    ys, ks, vs, iks, cs, ss = _trunk(x_sample, cache_k[0], cache_v[0], cache_idx_k[0],
                                     state_conv[0], state_ssm[0], *args)
    return (yp, ys, kp, vp, ikp, cp, sp, ks, vs, iks, cs, ss)
```

```python
import functools
import math

import jax
import jax.numpy as jnp
from jax import lax
from jax.experimental import pallas as pl
from jax.experimental.pallas import tpu as pltpu

F32 = jnp.float32
BF16 = jnp.bfloat16

D_MODEL = 1024
CHUNK = 64
CHUNK_SHIFT = 6
N_HEADS = 16
HEAD_DIM = 64
N_KV_HEADS = 4
GQ = N_HEADS // N_KV_HEADS
IDX_HEADS = 8
IDX_DIM = 64
TOPK_MAX = 256
IDX_SCALE = (IDX_HEADS * IDX_DIM) ** -0.5
ATTN_SCALE = HEAD_DIM ** -0.5
NUM_BUCKETS = 32
MAX_DISTANCE = 128
D_INNER = 2 * D_MODEL
SSM_HEAD_DIM = 64
SSM_HEADS = D_INNER // SSM_HEAD_DIM
SSM_GROUPS = 8
HEADS_PER_GROUP = SSM_HEADS // SSM_GROUPS
GROUP_W = HEADS_PER_GROUP * SSM_HEAD_DIM
D_STATE = 128
D_CONV = 4
CONV_DIM = D_INNER + 2 * SSM_GROUPS * D_STATE
D_FF = 4 * D_MODEL
EPS = 1e-6

Q_OFF = 0
K_OFF = N_HEADS * HEAD_DIM
V_OFF = K_OFF + N_KV_HEADS * HEAD_DIM
QI_OFF = V_OFF + N_KV_HEADS * HEAD_DIM
KI_OFF = QI_OFF + IDX_HEADS * IDX_DIM
WI_OFF = KI_OFF + IDX_DIM
ATTN_COLS = WI_OFF + IDX_HEADS

SSM_Z_OFF = 0
SSM_XBC_OFF = D_INNER
SSM_DT_OFF = D_INNER + CONV_DIM
SSM_COLS = SSM_DT_OFF + SSM_HEADS

LANE = 128
KEY_WINDOW = 512
MAX_TILE_ROWS = 128
LOG2E = 1.4426950408889634
VMEM_LIMIT = 56 * 1024 * 1024

NEG_BIG = -1e30
KEY_NEG_INF = -2139095041
INT_MIN = -2147483648


def _round_up(n, m):
    return (n + m - 1) // m * m


def _rms(x, g):
    ms = jnp.mean(x * x, axis=-1, keepdims=True)
    return x * lax.rsqrt(ms + EPS) * g


def _sigmoid(x):
    return 1.0 / (1.0 + jnp.exp(-x))


def _params(*sem):
    return pltpu.CompilerParams(dimension_semantics=sem, vmem_limit_bytes=VMEM_LIMIT)


def _norm_matmul_kernel(x_ref, g_ref, w_ref, o_ref, xn_ref):
    @pl.when(pl.program_id(1) == 0)
    def _():
        xn_ref[...] = _rms(x_ref[...], g_ref[...]).astype(BF16)

    o_ref[...] = jnp.dot(xn_ref[...], w_ref[...], preferred_element_type=F32)


def _norm_matmul(x, g, w, *, tm, tn):
    m, k = x.shape
    n = w.shape[1]
    return pl.pallas_call(
        _norm_matmul_kernel,
        out_shape=jax.ShapeDtypeStruct((m, n), F32),
        grid=(m // tm, n // tn),
        in_specs=[pl.BlockSpec((tm, k), lambda i, j: (i, 0)),
                  pl.BlockSpec((1, k), lambda i, j: (0, 0)),
                  pl.BlockSpec((k, tn), lambda i, j: (0, j))],
        out_specs=pl.BlockSpec((tm, tn), lambda i, j: (i, j)),
        scratch_shapes=[pltpu.VMEM((tm, k), BF16)],
        compiler_params=_params("parallel", "arbitrary"),
        name="norm_matmul",
    )(x, g, w)


def _attn_proj_kernel(x_ref, g_ref, w_ref, pb_ref, k_ref, v_ref, ki_ref, kiwi_ref):
    xn = _rms(x_ref[...], g_ref[...]).astype(BF16)
    acc = jnp.dot(xn, w_ref[...], preferred_element_type=F32)
    pb_ref[...] = acc.astype(BF16)
    for g in range(N_KV_HEADS):
        k_ref[:, g, :] = acc[:, K_OFF + g * HEAD_DIM:K_OFF + (g + 1) * HEAD_DIM]
        v_ref[:, g, :] = acc[:, V_OFF + g * HEAD_DIM:V_OFF + (g + 1) * HEAD_DIM]
    ki_ref[...] = acc[:, KI_OFF:WI_OFF]
    kiwi_ref[...] = acc[:, KI_OFF:KI_OFF + LANE]


def _attn_proj(x, g, w, *, tm):
    m, k = x.shape
    n = w.shape[1]
    row = lambda i: (i, 0)
    kv_spec = pl.BlockSpec((tm, N_KV_HEADS, HEAD_DIM), lambda i: (i, 0, 0))
    return pl.pallas_call(
        _attn_proj_kernel,
        out_shape=[jax.ShapeDtypeStruct((m, n), BF16),
                   jax.ShapeDtypeStruct((m, N_KV_HEADS, HEAD_DIM), F32),
                   jax.ShapeDtypeStruct((m, N_KV_HEADS, HEAD_DIM), F32),
                   jax.ShapeDtypeStruct((m, IDX_DIM), F32),
                   jax.ShapeDtypeStruct((m, LANE), F32)],
        grid=(m // tm,),
        in_specs=[pl.BlockSpec((tm, k), row),
                  pl.BlockSpec((1, k), lambda i: (0, 0)),
                  pl.BlockSpec((k, n), lambda i: (0, 0))],
        out_specs=[pl.BlockSpec((tm, n), row), kv_spec, kv_spec,
                   pl.BlockSpec((tm, IDX_DIM), row), pl.BlockSpec((tm, LANE), row)],
        compiler_params=_params("parallel"),
        name="attn_proj",
    )(x, g, w)


def _tail_kernel(a_ref, wo_ref, r_ref, g_ref, wu_ref, wd_ref, gf_ref, o_ref, *, tf, final_norm):
    x1 = r_ref[...] + jnp.dot(a_ref[...], wo_ref[...], preferred_element_type=F32)
    xn = _rms(x1, g_ref[...]).astype(BF16)
    acc = x1
    for f0 in range(0, wu_ref.shape[1], tf):
        h = jnp.maximum(jnp.dot(xn, wu_ref[:, f0:f0 + tf], preferred_element_type=F32), 0.0)
        acc = acc + jnp.dot((h * h).astype(BF16), wd_ref[f0:f0 + tf, :], preferred_element_type=F32)
    if final_norm:
        acc = _rms(acc, gf_ref[...])
    o_ref[...] = acc


def _layer_tail(a, wo, resid, g, wu, wd, gf, *, tm, tf, final_norm):
    m, ka = a.shape
    d = wo.shape[1]
    ff = wu.shape[1]
    row = lambda i: (i, 0)
    whole = lambda shape: pl.BlockSpec(shape, lambda i: (0, 0), pipeline_mode=pl.Buffered(1))
    return pl.pallas_call(
        functools.partial(_tail_kernel, tf=tf, final_norm=final_norm),
        out_shape=jax.ShapeDtypeStruct((m, d), F32),
        grid=(m // tm,),
        in_specs=[pl.BlockSpec((tm, ka), row), whole((ka, d)), pl.BlockSpec((tm, d), row),
                  whole((1, d)), whole((d, ff)), whole((ff, d)), whole((1, d))],
        out_specs=pl.BlockSpec((tm, d), row),
        compiler_params=_params("parallel"),
        name="layer_tail",
    )(a, wo, resid, g, wu, wd, gf)


def _attn_kernel(*refs, tq, t, past, front_pad, topk):
    if past:
        (q_ref, qi_ref, kiwi_ref, kn_ref, vn_ref, kin_ref, pk_ref, pv_ref, pki_ref, nb_ref, o_ref,
         k_ref, v_ref, ki_ref, skey_ref, qs_ref, qis_ref, wcol_ref, madd_ref, s_ref, m_ref, acc_ref) = refs
    else:
        (q_ref, qi_ref, kiwi_ref, kn_ref, vn_ref, kin_ref, nb_ref, o_ref,
         k_ref, v_ref, ki_ref, skey_ref, qs_ref, qis_ref, wcol_ref, madd_ref, s_ref, m_ref, acc_ref) = refs
    w = KEY_WINDOW
    rows = GQ * tq
    rb = madd_ref.shape[0]
    keys_on_rows = skey_ref.shape[1] == w
    kax = 0 if keys_on_rows else 1
    qshape = (1, tq) if keys_on_rows else (tq, 1)
    i = pl.program_id(1)
    q0 = past + i * tq
    lv = q0 + tq
    nw = jnp.right_shift(lv + (w - 1), 9)
    qchunk = jnp.right_shift(q0 + lax.broadcasted_iota(jnp.int32, qshape, 1 - kax), CHUNK_SHIFT)
    kidx = lax.broadcasted_iota(jnp.int32, (w, 1) if keys_on_rows else (1, w), kax)
    nt = (((1,), (1,)), ((), ()))

    def win_row(j):
        return pl.multiple_of(lv + front_pad - w * (j + 1), tq)

    @pl.when(i == 0)
    def _():
        ones_col = jnp.where(lax.broadcasted_iota(jnp.int32, (1, LANE - HEAD_DIM), 1) == 0, 1.0, 0.0)

        def put(r0, n, kc, vc, kic):
            ones = jnp.broadcast_to(ones_col, (n, LANE - HEAD_DIM)).astype(BF16)
            for g in range(N_KV_HEADS):
                gs = slice(g * HEAD_DIM, (g + 1) * HEAD_DIM)
                k_ref[g, r0:r0 + n, :] = kc[:, gs]
                v_ref[g, r0:r0 + n, :] = jnp.concatenate([vc[:, gs], ones], axis=1)
            ki_ref[r0:r0 + n, :] = kic

        kvw = N_KV_HEADS * HEAD_DIM
        if front_pad:
            put(0, front_pad, jnp.zeros((front_pad, kvw), BF16), jnp.zeros((front_pad, kvw), BF16),
                jnp.zeros((front_pad, IDX_DIM), BF16))
        step = min(t, w)
        for c0 in range(0, past, w):
            put(front_pad + c0, w, pk_ref[c0:c0 + w, :].astype(BF16), pv_ref[c0:c0 + w, :].astype(BF16),
                pki_ref[c0:c0 + w, :].astype(BF16))
        for c0 in range(0, t, step):
            put(front_pad + past + c0, step, kn_ref[c0:c0 + step, :], vn_ref[c0:c0 + step, :],
                kin_ref[c0:c0 + step, 0:IDX_DIM])

    for h in range(N_HEADS):
        g, jj = divmod(h, GQ)
        qs_ref[g, jj * tq:(jj + 1) * tq, :] = q_ref[:, h * HEAD_DIM:(h + 1) * HEAD_DIM]

    if keys_on_rows:
        wi_t = kiwi_ref[...].T[IDX_DIM:IDX_DIM + IDX_HEADS, :]
        for hp in range(IDX_HEADS // 2):
            for u in range(2):
                h = 2 * hp + u
                qis_ref[hp, u * tq:(u + 1) * tq, :] = qi_ref[:, h * IDX_DIM:(h + 1) * IDX_DIM]
    else:
        wi = kiwi_ref[:, IDX_DIM:IDX_DIM + IDX_HEADS]
        for h in range(IDX_HEADS):
            qis_ref[0, h * tq:(h + 1) * tq, :] = qi_ref[:, h * IDX_DIM:(h + 1) * IDX_DIM]
            wcol_ref[h * tq:(h + 1) * tq, :] = wi[:, h:h + 1]

    def score_body(j, carry):
        kib = ki_ref[pl.ds(win_row(j), w), :]
        if keys_on_rows:
            sc = jnp.zeros((w, tq), F32)
            for hp in range(IDX_HEADS // 2):
                s2 = jnp.maximum(lax.dot_general(kib, qis_ref[hp], nt, preferred_element_type=F32), 0.0)
                sc = sc + s2[:, 0:tq] * wi_t[2 * hp:2 * hp + 1, :]
                sc = sc + s2[:, tq:2 * tq] * wi_t[2 * hp + 1:2 * hp + 2, :]
        else:
            s8 = jnp.maximum(lax.dot_general(qis_ref[0], kib, nt, preferred_element_type=F32), 0.0)
            s8 = s8 * wcol_ref[...]
            sc = s8[0:tq]
            for h in range(1, IDX_HEADS):
                sc = sc + s8[h * tq:(h + 1) * tq]
        sc = sc * IDX_SCALE
        kpos = (lv - w * (j + 1)) + kidx
        vis = (kpos >= 0) & (jnp.right_shift(kpos, CHUNK_SHIFT) <= qchunk)
        bits = pltpu.bitcast(sc, jnp.int32)
        key = bits ^ (jnp.right_shift(bits, 31) & 0x7FFFFFFF)
        key = jnp.where(sc == 0.0, 0, key)
        skey_ref[j] = jnp.where(vis, key, KEY_NEG_INF)
        return carry

    lax.fori_loop(0, nw, score_body, 0)

    if keys_on_rows:
        part = (4 * 8, tq)

        def fold(c):
            return jnp.sum(c.reshape(w // part[0], part[0], tq), axis=0)
    else:
        part = (tq, LANE)

        def fold(c):
            return (c[:, 0:LANE] + c[:, LANE:2 * LANE]) + (c[:, 2 * LANE:3 * LANE] + c[:, 3 * LANE:4 * LANE])

    def total(acc):
        return jnp.sum(acc, axis=kax, keepdims=True)

    def count_ge(cand):
        def body(j, acc):
            return acc + fold(jnp.where(skey_ref[j] >= cand, 1.0, 0.0))
        return total(lax.fori_loop(0, nw, body, jnp.zeros(part, F32)))

    def search_body(it, carry):
        prefix, n_lo, n_hi = carry
        cand = prefix + lax.shift_left(jnp.int32(1), 31 - it)
        c = count_ge(cand)
        take = c >= float(topk)
        return jnp.where(take, cand, prefix), jnp.where(take, c, n_lo), jnp.where(take, n_hi, c)

    n_all = (nw * w).astype(F32)
    tstar, n_ge, n_gt = lax.fori_loop(
        0, 32, search_body,
        (jnp.full(qshape, INT_MIN, jnp.int32), jnp.full(qshape, 1.0, F32) * n_all, jnp.zeros(qshape, F32)))

    need = float(topk) - n_gt
    n_eq = n_ge - n_gt
    bad = (n_eq > need) & (tstar > KEY_NEG_INF)
    n_bad = jnp.sum(jnp.where(bad, 1.0, 0.0))

    @pl.when(n_bad > 0.0)
    def _():
        r_i = lax.broadcasted_iota(jnp.int32, (w, w), 0)
        c_i = lax.broadcasted_iota(jnp.int32, (w, w), 1)
        earlier = jnp.where(c_i < r_i if keys_on_rows else r_i < c_i, 1.0, 0.0).astype(BF16)

        def tie_body(t, seen):
            j = nw - 1 - t
            kt = skey_ref[j]
            eq = kt == tstar
            eqf = jnp.where(eq, 1.0, 0.0)
            if keys_on_rows:
                inwin = jnp.dot(earlier, eqf.astype(BF16), preferred_element_type=F32)
            else:
                inwin = jnp.dot(eqf.astype(BF16), earlier, preferred_element_type=F32)
            skey_ref[j] = jnp.where(eq & bad & (seen + inwin >= need), KEY_NEG_INF, kt)
            return seen + total(eqf)

        lax.fori_loop(0, nw, tie_body, jnp.zeros(qshape, F32))

    teff = jnp.maximum(tstar, KEY_NEG_INF + 1)

    m_ref[...] = jnp.full(m_ref.shape, NEG_BIG, F32)
    acc_ref[...] = jnp.zeros(acc_ref.shape, F32)

    def attend(j, near):
        row = win_row(j)
        tile = jnp.where(skey_ref[j] >= teff, 0.0, NEG_BIG)
        if keys_on_rows:
            tile = tile.T
        for r in range(rb // tq):
            madd_ref[r * tq:(r + 1) * tq, :] = tile
        tiles = [(g, r0) for g in range(N_KV_HEADS) for r0 in range(0, rows, rb)]

        for t, (g, r0) in enumerate(tiles):
            kb = k_ref[g, pl.ds(row, w), :]
            s = lax.dot_general(qs_ref[g, r0:r0 + rb, :], kb, nt, preferred_element_type=F32)
            s = s + madd_ref[...]
            if near:
                s = s + nb_ref[g, r0:r0 + rb, :]
            s_ref[t] = s
        for t, (g, r0) in enumerate(tiles):
            rs = slice(r0, r0 + rb)
            s = s_ref[t]
            m_old = m_ref[g, rs, :]
            m_new = jnp.maximum(m_old, jnp.max(s, axis=1, keepdims=True))
            p = jnp.exp2(s - jnp.concatenate([m_new] * (w // LANE), axis=1))
            vb = v_ref[g, pl.ds(row, w), :]
            acc_ref[g, rs, :] = (jnp.exp2(m_old - m_new) * acc_ref[g, rs, :]
                                 + jnp.dot(p.astype(BF16), vb, preferred_element_type=F32))
            m_ref[g, rs, :] = m_new

    attend(0, True)

    def far_body(j, carry):
        attend(j, False)
        return carry

    lax.fori_loop(1, nw, far_body, 0)

    for h in range(N_HEADS):
        g, jj = divmod(h, GQ)
        a = acc_ref[g, jj * tq:(jj + 1) * tq, :]
        o_ref[:, h * HEAD_DIM:(h + 1) * HEAD_DIM] = (
            a[:, 0:HEAD_DIM] / a[:, HEAD_DIM:HEAD_DIM + 1]).astype(o_ref.dtype)


def _rel_bucket(rel):
    half = NUM_BUCKETS // 2
    max_exact = half // 2
    base = jnp.where(rel > 0, half, 0)
    n = jnp.abs(rel)
    nf = jnp.maximum(n, 1).astype(jnp.float32)
    large = max_exact + (jnp.log(nf / max_exact) / math.log(MAX_DISTANCE / max_exact)
                         * (half - max_exact)).astype(jnp.int32)
    large = jnp.minimum(large, half - 1)
    return base + jnp.where(n < max_exact, n, large)


def _near_bias(rel_bias, tq):
    r = jnp.arange(tq, dtype=jnp.int32)[:, None]
    c = jnp.arange(KEY_WINDOW, dtype=jnp.int32)[None, :]
    rel = (tq - KEY_WINDOW + c) - r
    tab = rel_bias.astype(F32)
    far = _rel_bucket(jnp.full((1,), -(1 << 20), jnp.int32))
    far_row = jnp.dot(jax.nn.one_hot(far, NUM_BUCKETS, dtype=F32), tab, precision=lax.Precision.HIGHEST)
    pick = jax.nn.one_hot(_rel_bucket(rel), NUM_BUCKETS, dtype=F32)
    nb = jnp.einsum('qcb,bh->hqc', pick, (tab - far_row) * LOG2E, precision=lax.Precision.HIGHEST)
    return nb.reshape(N_KV_HEADS, GQ * tq, KEY_WINDOW)


def _dsa_mixer(projb, kiwi, past_k, past_v, past_ik, rel_bias, b, t):
    past = past_k.shape[1]
    n_keys = past + t
    topk = min(TOPK_MAX, n_keys // 4)
    tq = 128 if t % 128 == 0 else t
    assert KEY_WINDOW % tq == 0 and past % KEY_WINDOW == 0 and tq % 16 == 0 and n_keys >= KEY_WINDOW
    assert t % min(t, KEY_WINDOW) == 0
    front_pad = KEY_WINDOW - tq
    lp = front_pad + n_keys
    nq = t // tq
    nw_max = (n_keys + KEY_WINDOW - 1) // KEY_WINDOW
    rows = GQ * tq
    rb = min(rows, MAX_TILE_ROWS)
    keys_on_rows = tq % LANE == 0
    assert keys_on_rows or IDX_HEADS * tq <= MAX_TILE_ROWS
    kvw = N_KV_HEADS * HEAD_DIM
    qw = N_HEADS * HEAD_DIM
    qiw = IDX_HEADS * IDX_DIM

    qrow = lambda bi, i: (bi * nq + i, 0)
    in_specs = [pl.BlockSpec((tq, qw), qrow),
                pl.BlockSpec((tq, qiw), lambda bi, i: (bi * nq + i, QI_OFF // qiw)),
                pl.BlockSpec((tq, LANE), qrow),
                pl.BlockSpec((t, kvw), lambda bi, i: (bi, K_OFF // kvw)),
                pl.BlockSpec((t, kvw), lambda bi, i: (bi, V_OFF // kvw)),
                pl.BlockSpec((t, LANE), lambda bi, i: (bi, KI_OFF // LANE))]
    args = [projb, projb, kiwi, projb, projb, projb]
    if past:
        in_specs += [pl.BlockSpec((None, past, kvw), lambda bi, i: (bi, 0, 0)),
                     pl.BlockSpec((None, past, kvw), lambda bi, i: (bi, 0, 0)),
                     pl.BlockSpec((None, past, IDX_DIM), lambda bi, i: (bi, 0, 0))]
        args += [past_k.reshape(b, past, kvw), past_v.reshape(b, past, kvw), past_ik]
    in_specs.append(pl.BlockSpec((N_KV_HEADS, rows, KEY_WINDOW), lambda bi, i: (0, 0, 0)))
    args.append(_near_bias(rel_bias, tq))

    kern = functools.partial(_attn_kernel, tq=tq, t=t, past=past, front_pad=front_pad, topk=topk)
    return pl.pallas_call(
        kern,
        out_shape=jax.ShapeDtypeStruct((b * t, qw), BF16),
        grid=(b, nq),
        in_specs=in_specs,
        out_specs=pl.BlockSpec((tq, qw), qrow),
        scratch_shapes=[pltpu.VMEM((N_KV_HEADS, lp, HEAD_DIM), BF16),
                        pltpu.VMEM((N_KV_HEADS, lp, LANE), BF16),
                        pltpu.VMEM((lp, IDX_DIM), BF16),
                        pltpu.VMEM((nw_max, KEY_WINDOW, tq) if keys_on_rows
                                   else (nw_max, tq, KEY_WINDOW), jnp.int32),
                        pltpu.VMEM((N_KV_HEADS, rows, HEAD_DIM), BF16),
                        pltpu.VMEM((IDX_HEADS // 2, 2 * tq, IDX_DIM) if keys_on_rows
                                   else (1, IDX_HEADS * tq, IDX_DIM), BF16),
                        pltpu.VMEM((8, 1) if keys_on_rows else (IDX_HEADS * tq, 1), F32),
                        pltpu.VMEM((rb, KEY_WINDOW), F32),
                        pltpu.VMEM((N_KV_HEADS * rows // rb, rb, KEY_WINDOW), F32),
                        pltpu.VMEM((N_KV_HEADS, rows, LANE), F32),
                        pltpu.VMEM((N_KV_HEADS, rows, LANE), F32)],
        compiler_params=_params("arbitrary", "arbitrary"),
        name="dsa_attention",
    )(*args)


def _ssd_kernel(z_ref, xlo_ref, xhi_ref, dt_ref, cst_ref, st0_ref, cw_ref, cb_ref, dtb_ref, alog_ref,
                dexp_ref, nw_ref, expand_ref, y_ref, ncv_ref, stout_ref,
                xpad_ref, act_ref, yd_ref, st_ref, *, q):
    c = pl.program_id(1)
    half = CONV_DIM // 2

    @pl.when(c == 0)
    def _():
        xpad_ref[8 - (D_CONV - 1):8, :] = cst_ref[...]
        for h in range(SSM_HEADS):
            g, jj = divmod(h, HEADS_PER_GROUP)
            st_ref[g, :, jj * SSM_HEAD_DIM:(jj + 1) * SSM_HEAD_DIM] = st0_ref[h].T

    cc = 512
    for c0 in range(0, CONV_DIM, cc):
        cs = slice(c0, c0 + cc)
        src = xlo_ref[:, c0:c0 + cc] if c0 < half else xhi_ref[:, c0 - half:c0 - half + cc]
        xpad_ref[8:8 + q, cs] = src
        conv = cb_ref[:, cs] + xpad_ref[5:5 + q, cs] * cw_ref[0:1, cs]
        conv = conv + xpad_ref[6:6 + q, cs] * cw_ref[1:2, cs]
        conv = conv + xpad_ref[7:7 + q, cs] * cw_ref[2:3, cs]
        conv = conv + xpad_ref[8:8 + q, cs] * cw_ref[3:4, cs]
        act_ref[:, cs] = conv * _sigmoid(conv)
        xpad_ref[0:8, cs] = xpad_ref[q:q + 8, cs]

    hi = lax.Precision.HIGHEST
    x_dt = dt_ref[:, 0:SSM_HEADS] + dtb_ref[...]
    dt = jnp.maximum(x_dt, 0.0) + jnp.log1p(jnp.exp(-jnp.abs(x_dt)))
    a = dt * (-jnp.exp(alog_ref[...]))
    rr = lax.broadcasted_iota(jnp.int32, (q, q), 0)
    cl = lax.broadcasted_iota(jnp.int32, (q, q), 1)
    causal = rr >= cl
    tri = jnp.where(causal, 1.0, 0.0)
    a_cs = jnp.dot(tri, a, precision=hi, preferred_element_type=F32)
    eye = jnp.where(lax.broadcasted_iota(jnp.int32, (SSM_HEADS, SSM_HEADS), 0)
                    == lax.broadcasted_iota(jnp.int32, (SSM_HEADS, SSM_HEADS), 1), 1.0, 0.0)
    a_cs_t = lax.dot_general(eye, a_cs, (((1,), (1,)), ((), ())), precision=hi,
                             preferred_element_type=F32)
    a_end = a_cs[q - 1:q, :]
    per_head = jnp.concatenate([dt, jnp.exp(a_cs), jnp.exp(a_end - a_cs)], axis=0)
    top = per_head.astype(BF16)
    rest = (per_head - top.astype(F32)).astype(BF16)
    spread = jnp.dot(jnp.concatenate([top, rest], axis=0), expand_ref[...],
                     preferred_element_type=F32)
    dt_x = spread[0:q] + spread[3 * q:4 * q]
    e_x = spread[q:2 * q] + spread[4 * q:5 * q]
    dte_x = spread[2 * q:3 * q] + spread[5 * q:6 * q]
    dec_x = e_x[q - 1:q, :]

    for g in range(SSM_GROUPS):
        gs = slice(g * GROUP_W, (g + 1) * GROUP_W)
        bg = act_ref[:, D_INNER + g * D_STATE:D_INNER + (g + 1) * D_STATE].astype(BF16)
        cg = act_ref[:, D_INNER + (SSM_GROUPS + g) * D_STATE:
                     D_INNER + (SSM_GROUPS + g + 1) * D_STATE].astype(BF16)
        cb = lax.dot_general(cg, bg, (((1,), (1,)), ((), ())), preferred_element_type=F32)
        xs_g = act_ref[:, gs]
        xd_g = xs_g * dt_x[:, gs]
        st_g = st_ref[g]
        y_off = jnp.dot(cg, st_g.astype(BF16), preferred_element_type=F32) * e_x[:, gs]
        for jj in range(HEADS_PER_GROUP):
            h = g * HEADS_PER_GROUP + jj
            seg = a_cs[:, h:h + 1] - a_cs_t[h:h + 1, :]
            decay = jnp.where(causal, jnp.exp(seg), 0.0)
            hs = slice(jj * SSM_HEAD_DIM, (jj + 1) * SSM_HEAD_DIM)
            yd_ref[:, hs] = jnp.dot((cb * decay).astype(BF16), xd_g[:, hs].astype(BF16),
                                    preferred_element_type=F32)
        st_ref[g] = st_g * dec_x[:, gs] + lax.dot_general(
            bg, (xd_g * dte_x[:, gs]).astype(BF16), (((0,), (0,)), ((), ())),
            preferred_element_type=F32)
        y = yd_ref[...] + y_off + xs_g * dexp_ref[:, gs]
        zg = z_ref[:, gs]
        yg = y * (zg * _sigmoid(zg))
        ms = jnp.mean(yg * yg, axis=-1, keepdims=True)
        y_ref[:, gs] = (yg * lax.rsqrt(ms + EPS) * nw_ref[:, gs]).astype(y_ref.dtype)

    @pl.when(c == pl.num_programs(1) - 1)
    def _():
        ncv_ref[...] = xpad_ref[8 - (D_CONV - 1):8, :]
        for h in range(SSM_HEADS):
            g, jj = divmod(h, HEADS_PER_GROUP)
            stout_ref[h] = st_ref[g, :, jj * SSM_HEAD_DIM:(jj + 1) * SSM_HEAD_DIM].T


def _ssd_mixer(proj, conv_state, ssm_state, conv_w, conv_b, dt_bias, a_log, d_skip, norm_w, b, t):
    q = 128 if t % 128 == 0 else t
    assert q % 8 == 0 and t >= D_CONV - 1 and CONV_DIM == 2 * D_INNER
    nc = t // q
    expand = jnp.repeat(jnp.eye(SSM_HEADS, dtype=BF16), SSM_HEAD_DIM, axis=1)
    dexp = jnp.repeat(d_skip.astype(F32), SSM_HEAD_DIM)[None, :]
    row = lambda k: (lambda bi, c: (bi * nc + c, k))
    const2 = lambda bi, c: (0, 0)
    state_spec = pl.BlockSpec((None, SSM_HEADS, SSM_HEAD_DIM, D_STATE), lambda bi, c: (bi, 0, 0, 0))
    conv_spec = pl.BlockSpec((None, D_CONV - 1, CONV_DIM), lambda bi, c: (bi, 0, 0))
    y, new_conv, new_ssm = pl.pallas_call(
        functools.partial(_ssd_kernel, q=q),
        out_shape=[jax.ShapeDtypeStruct((b * t, D_INNER), BF16),
                   jax.ShapeDtypeStruct((b, D_CONV - 1, CONV_DIM), F32),
                   jax.ShapeDtypeStruct((b, SSM_HEADS, SSM_HEAD_DIM, D_STATE), F32)],
        grid=(b, nc),
        in_specs=[pl.BlockSpec((q, D_INNER), row(SSM_Z_OFF // D_INNER)),
                  pl.BlockSpec((q, D_INNER), row(SSM_XBC_OFF // D_INNER)),
                  pl.BlockSpec((q, D_INNER), row(SSM_XBC_OFF // D_INNER + 1)),
                  pl.BlockSpec((q, LANE), row(SSM_DT_OFF // LANE)),
                  conv_spec,
                  state_spec,
                  pl.BlockSpec((D_CONV, CONV_DIM), const2),
                  pl.BlockSpec((1, CONV_DIM), const2),
                  pl.BlockSpec((1, SSM_HEADS), const2),
                  pl.BlockSpec((1, SSM_HEADS), const2),
                  pl.BlockSpec((1, D_INNER), const2),
                  pl.BlockSpec((1, D_INNER), const2),
                  pl.BlockSpec((SSM_HEADS, D_INNER), const2)],
        out_specs=[pl.BlockSpec((q, D_INNER), row(0)), conv_spec, state_spec],
        scratch_shapes=[pltpu.VMEM((q + 8, CONV_DIM), F32),
                        pltpu.VMEM((q, CONV_DIM), F32),
                        pltpu.VMEM((q, GROUP_W), F32),
                        pltpu.VMEM((SSM_GROUPS, D_STATE, GROUP_W), F32)],
        compiler_params=_params("arbitrary", "arbitrary"),
        name="ssd_mixer",
    )(proj, proj, proj, proj, conv_state.astype(F32), ssm_state.astype(F32), conv_w.astype(F32),
      conv_b.astype(F32)[None, :], dt_bias.astype(F32)[None, :], a_log.astype(F32)[None, :], dexp,
      norm_w.astype(F32)[None, :], expand)
    return y, new_conv, new_ssm.astype(ssm_state.dtype)


def _tile_m(m):
    for tm in (1024, 512, 256, 128, 64, 32, 16, 8):
        if m % tm == 0:
            return tm
    raise ValueError(m)


def _pad_cols(w, n):
    return jnp.pad(w, ((0, 0), (0, n - w.shape[1])))


def _prep_weights(norm_mix, norm_ffn, norm_final, attn_w_in, attn_w_o, ssm_w_in, ssm_w_out,
                  mlp_w_up, mlp_w_down):
    col_scale = jnp.where(jnp.arange(ATTN_COLS) < K_OFF, ATTN_SCALE * LOG2E, 1.0).astype(F32)
    return dict(
        attn_in=_pad_cols(attn_w_in[0] * col_scale, _round_up(ATTN_COLS, LANE)).astype(BF16),
        attn_o=attn_w_o[0].astype(BF16),
        ssm_in=_pad_cols(ssm_w_in[0], _round_up(SSM_COLS, LANE)).astype(BF16),
        ssm_out=ssm_w_out[0].astype(BF16),
        up=[mlp_w_up[i].astype(BF16) for i in range(2)],
        down=[mlp_w_down[i].astype(BF16) for i in range(2)],
        g_mix=[norm_mix[i].astype(F32)[None, :] for i in range(2)],
        g_ffn=[norm_ffn[i].astype(F32)[None, :] for i in range(2)],
        g_final=norm_final.astype(F32)[None, :],
    )


def _trunk(x, past_k, past_v, past_ik, conv_st, ssm_st, rel_bias, wts, ssm_conv_w, ssm_conv_b,
           ssm_dt_bias, ssm_a_log, ssm_d, ssm_norm):
    b, t, d = x.shape
    m = b * t
    tm = _tile_m(m)
    x2 = x.reshape(m, d).astype(F32)

    projb, k_new, v_new, ki_new, kiwi = _attn_proj(x2, wts["g_mix"][0], wts["attn_in"], tm=min(tm, 512))
    ao = _dsa_mixer(projb, kiwi, past_k, past_v, past_ik, rel_bias, b, t)
    x2 = _layer_tail(ao, wts["attn_o"], x2, wts["g_ffn"][0], wts["up"][0], wts["down"][0],
                     wts["g_final"], tm=min(tm, 512), tf=1024, final_norm=False)

    proj = _norm_matmul(x2, wts["g_mix"][1], wts["ssm_in"], tm=min(tm, 256), tn=wts["ssm_in"].shape[1])
    y, new_conv, new_ssm = _ssd_mixer(proj, conv_st, ssm_st, ssm_conv_w, ssm_conv_b, ssm_dt_bias,
                                      ssm_a_log, ssm_d, ssm_norm, b, t)
    x2 = _layer_tail(y, wts["ssm_out"], x2, wts["g_ffn"][1], wts["up"][1], wts["down"][1],
                     wts["g_final"], tm=min(tm, 512), tf=1024, final_norm=True)

    dt = x.dtype
    return (x2.reshape(b, t, d).astype(dt),
            k_new.reshape(1, b, t, N_KV_HEADS, HEAD_DIM).astype(dt),
            v_new.reshape(1, b, t, N_KV_HEADS, HEAD_DIM).astype(dt),
            ki_new.reshape(1, b, t, IDX_DIM).astype(dt), new_conv[None].astype(dt), new_ssm[None])


def kernel(x_prompt, x_sample, cache_k, cache_v, cache_idx_k, state_conv, state_ssm, rel_bias, norm_mix, norm_ffn, norm_final, attn_w_in, attn_w_o, ssm_w_in, ssm_conv_w, ssm_conv_b, ssm_dt_bias, ssm_a_log, ssm_d, ssm_norm, ssm_w_out, mlp_w_up, mlp_w_down):
    wts = _prep_weights(norm_mix, norm_ffn, norm_final, attn_w_in, attn_w_o, ssm_w_in, ssm_w_out,
                        mlp_w_up, mlp_w_down)
    bp = x_prompt.shape[0]
    dtp = x_prompt.dtype
    empty_k = jnp.zeros((bp, 0, N_KV_HEADS, HEAD_DIM), dtp)
    empty_ik = jnp.zeros((bp, 0, IDX_DIM), dtp)
    zero_conv = jnp.zeros((bp, D_CONV - 1, CONV_DIM), dtp)
    zero_ssm = jnp.zeros((bp, SSM_HEADS, SSM_HEAD_DIM, D_STATE), dtp)
    args = (rel_bias, wts, ssm_conv_w[0], ssm_conv_b[0], ssm_dt_bias[0], ssm_a_log[0], ssm_d[0],
            ssm_norm[0])
    yp, kp, vp, ikp, cp, sp = _trunk(x_prompt, empty_k, empty_k, empty_ik, zero_conv, zero_ssm, *args)
    ys, ks, vs, iks, cs, ss = _trunk(x_sample, cache_k[0], cache_v[0], cache_idx_k[0],
                                     state_conv[0], state_ssm[0], *args)
    return (yp, ys, kp, vp, ikp, cp, sp, ks, vs, iks, cs, ss)
```

```python
import functools
import math

import jax
import jax.numpy as jnp
from jax import lax
from jax.experimental import pallas as pl
from jax.experimental.pallas import tpu as pltpu

F32 = jnp.float32
BF16 = jnp.bfloat16

D_MODEL = 1024
CHUNK = 64
CHUNK_SHIFT = 6
N_HEADS = 16
HEAD_DIM = 64
N_KV_HEADS = 4
GQ = N_HEADS // N_KV_HEADS
IDX_HEADS = 8
IDX_DIM = 64
TOPK_MAX = 256
IDX_SCALE = (IDX_HEADS * IDX_DIM) ** -0.5
ATTN_SCALE = HEAD_DIM ** -0.5
NUM_BUCKETS = 32
MAX_DISTANCE = 128
D_INNER = 2 * D_MODEL
SSM_HEAD_DIM = 64
SSM_HEADS = D_INNER // SSM_HEAD_DIM
SSM_GROUPS = 8
HEADS_PER_GROUP = SSM_HEADS // SSM_GROUPS
GROUP_W = HEADS_PER_GROUP * SSM_HEAD_DIM
D_STATE = 128
D_CONV = 4
CONV_DIM = D_INNER + 2 * SSM_GROUPS * D_STATE
D_FF = 4 * D_MODEL
EPS = 1e-6

Q_OFF = 0
K_OFF = N_HEADS * HEAD_DIM
V_OFF = K_OFF + N_KV_HEADS * HEAD_DIM
QI_OFF = V_OFF + N_KV_HEADS * HEAD_DIM
KI_OFF = QI_OFF + IDX_HEADS * IDX_DIM
WI_OFF = KI_OFF + IDX_DIM
ATTN_COLS = WI_OFF + IDX_HEADS

SSM_Z_OFF = 0
SSM_XBC_OFF = D_INNER
SSM_DT_OFF = D_INNER + CONV_DIM
SSM_COLS = SSM_DT_OFF + SSM_HEADS

LANE = 128
KEY_WINDOW = 512
MAX_TILE_ROWS = 512
LOG2E = 1.4426950408889634
VMEM_LIMIT = 56 * 1024 * 1024

NEG_BIG = -1e30
KEY_NEG_INF = -2139095041
MIN_NORMAL_KEY = 1 << 23
INT_MIN = -2147483648


def _round_up(n, m):
    return (n + m - 1) // m * m


def _rms(x, g):
    ms = jnp.mean(x * x, axis=-1, keepdims=True)
    return x * lax.rsqrt(ms + EPS) * g


def _sigmoid(x):
    return 1.0 / (1.0 + jnp.exp(-x))


def _params(*sem):
    return pltpu.CompilerParams(dimension_semantics=sem, vmem_limit_bytes=VMEM_LIMIT)


def _norm_matmul_kernel(x_ref, g_ref, w_ref, o_ref, xn_ref):
    @pl.when(pl.program_id(1) == 0)
    def _():
        xn_ref[...] = _rms(x_ref[...], g_ref[...]).astype(BF16)

    o_ref[...] = jnp.dot(xn_ref[...], w_ref[...], preferred_element_type=F32)


def _norm_matmul(x, g, w, *, tm, tn):
    m, k = x.shape
    n = w.shape[1]
    return pl.pallas_call(
        _norm_matmul_kernel,
        out_shape=jax.ShapeDtypeStruct((m, n), F32),
        grid=(m // tm, n // tn),
        in_specs=[pl.BlockSpec((tm, k), lambda i, j: (i, 0)),
                  pl.BlockSpec((1, k), lambda i, j: (0, 0)),
                  pl.BlockSpec((k, tn), lambda i, j: (0, j))],
        out_specs=pl.BlockSpec((tm, tn), lambda i, j: (i, j)),
        scratch_shapes=[pltpu.VMEM((tm, k), BF16)],
        compiler_params=_params("parallel", "arbitrary"),
        name="norm_matmul",
    )(x, g, w)


def _attn_proj_kernel(x_ref, g_ref, w_ref, pb_ref, k_ref, v_ref, ki_ref, kiwi_ref):
    xn = _rms(x_ref[...], g_ref[...]).astype(BF16)
    acc = jnp.dot(xn, w_ref[...], preferred_element_type=F32)
    pb_ref[...] = acc.astype(BF16)
    for g in range(N_KV_HEADS):
        k_ref[:, g, :] = acc[:, K_OFF + g * HEAD_DIM:K_OFF + (g + 1) * HEAD_DIM]
        v_ref[:, g, :] = acc[:, V_OFF + g * HEAD_DIM:V_OFF + (g + 1) * HEAD_DIM]
    ki_ref[...] = acc[:, KI_OFF:WI_OFF]
    kiwi_ref[...] = acc[:, KI_OFF:KI_OFF + LANE]


def _attn_proj(x, g, w, *, tm):
    m, k = x.shape
    n = w.shape[1]
    row = lambda i: (i, 0)
    kv_spec = pl.BlockSpec((tm, N_KV_HEADS, HEAD_DIM), lambda i: (i, 0, 0))
    return pl.pallas_call(
        _attn_proj_kernel,
        out_shape=[jax.ShapeDtypeStruct((m, n), BF16),
                   jax.ShapeDtypeStruct((m, N_KV_HEADS, HEAD_DIM), F32),
                   jax.ShapeDtypeStruct((m, N_KV_HEADS, HEAD_DIM), F32),
                   jax.ShapeDtypeStruct((m, IDX_DIM), F32),
                   jax.ShapeDtypeStruct((m, LANE), F32)],
        grid=(m // tm,),
        in_specs=[pl.BlockSpec((tm, k), row),
                  pl.BlockSpec((1, k), lambda i: (0, 0)),
                  pl.BlockSpec((k, n), lambda i: (0, 0))],
        out_specs=[pl.BlockSpec((tm, n), row), kv_spec, kv_spec,
                   pl.BlockSpec((tm, IDX_DIM), row), pl.BlockSpec((tm, LANE), row)],
        compiler_params=_params("parallel"),
        name="attn_proj",
    )(x, g, w)


def _tail_kernel(a_ref, wo_ref, r_ref, g_ref, wu_ref, wd_ref, gf_ref, o_ref, *, tf, final_norm):
    x1 = r_ref[...] + jnp.dot(a_ref[...], wo_ref[...], preferred_element_type=F32)
    xn = _rms(x1, g_ref[...]).astype(BF16)
    acc = x1
    for f0 in range(0, wu_ref.shape[1], tf):
        h = jnp.maximum(jnp.dot(xn, wu_ref[:, f0:f0 + tf], preferred_element_type=F32), 0.0)
        acc = acc + jnp.dot((h * h).astype(BF16), wd_ref[f0:f0 + tf, :], preferred_element_type=F32)
    if final_norm:
        acc = _rms(acc, gf_ref[...])
    o_ref[...] = acc


def _layer_tail(a, wo, resid, g, wu, wd, gf, *, tm, tf, final_norm):
    m, ka = a.shape
    d = wo.shape[1]
    ff = wu.shape[1]
    row = lambda i: (i, 0)
    whole = lambda shape: pl.BlockSpec(shape, lambda i: (0, 0), pipeline_mode=pl.Buffered(1))
    return pl.pallas_call(
        functools.partial(_tail_kernel, tf=tf, final_norm=final_norm),
        out_shape=jax.ShapeDtypeStruct((m, d), F32),
        grid=(m // tm,),
        in_specs=[pl.BlockSpec((tm, ka), row), whole((ka, d)), pl.BlockSpec((tm, d), row),
                  whole((1, d)), whole((d, ff)), whole((ff, d)), whole((1, d))],
        out_specs=pl.BlockSpec((tm, d), row),
        compiler_params=_params("parallel"),
        name="layer_tail",
    )(a, wo, resid, g, wu, wd, gf)


def _attn_kernel(*refs, tq, t, past, front_pad, topk):
    if past:
        (q_ref, qi_ref, kiwi_ref, kn_ref, vn_ref, kin_ref, pk_ref, pv_ref, pki_ref, nb_ref, o_ref,
         k_ref, v_ref, ki_ref, score_ref, qs_ref, qis_ref, wcol_ref, madd_ref, s_ref, m_ref, acc_ref) = refs
    else:
        (q_ref, qi_ref, kiwi_ref, kn_ref, vn_ref, kin_ref, nb_ref, o_ref,
         k_ref, v_ref, ki_ref, score_ref, qs_ref, qis_ref, wcol_ref, madd_ref, s_ref, m_ref, acc_ref) = refs
    w = KEY_WINDOW
    rows = GQ * tq
    rb = s_ref.shape[1]
    mrows = madd_ref.shape[0]
    keys_on_rows = score_ref.shape[1] == w
    kax = 0 if keys_on_rows else 1
    qshape = (1, tq) if keys_on_rows else (tq, 1)
    i = pl.program_id(1)
    q0 = past + i * tq
    lv = q0 + tq
    nw = jnp.right_shift(lv + (w - 1), 9)
    qchunk = jnp.right_shift(q0 + lax.broadcasted_iota(jnp.int32, qshape, 1 - kax), CHUNK_SHIFT)
    kidx = lax.broadcasted_iota(jnp.int32, (w, 1) if keys_on_rows else (1, w), kax)
    nt = (((1,), (1,)), ((), ()))

    def win_row(j):
        return pl.multiple_of(lv + front_pad - w * (j + 1), tq)

    @pl.when(i == 0)
    def _():
        ones_col = jnp.where(lax.broadcasted_iota(jnp.int32, (1, LANE - HEAD_DIM), 1) == 0, 1.0, 0.0)

        def put(r0, n, kc, vc, kic):
            ones = jnp.broadcast_to(ones_col, (n, LANE - HEAD_DIM)).astype(BF16)
            for g in range(N_KV_HEADS):
                gs = slice(g * HEAD_DIM, (g + 1) * HEAD_DIM)
                k_ref[g, r0:r0 + n, :] = kc[:, gs]
                v_ref[g, r0:r0 + n, 0:HEAD_DIM] = vc[:, gs]
                v_ref[g, r0:r0 + n, HEAD_DIM:LANE] = ones
            ki_ref[r0:r0 + n, :] = kic

        kvw = N_KV_HEADS * HEAD_DIM
        if front_pad:
            put(0, front_pad, jnp.zeros((front_pad, kvw), BF16), jnp.zeros((front_pad, kvw), BF16),
                jnp.zeros((front_pad, IDX_DIM), BF16))
        step = min(t, w)
        for c0 in range(0, past, w):
            put(front_pad + c0, w, pk_ref[c0:c0 + w, :].astype(BF16), pv_ref[c0:c0 + w, :].astype(BF16),
                pki_ref[c0:c0 + w, :].astype(BF16))
        for c0 in range(0, t, step):
            put(front_pad + past + c0, step, kn_ref[c0:c0 + step, :], vn_ref[c0:c0 + step, :],
                kin_ref[c0:c0 + step, 0:IDX_DIM])

    for h in range(N_HEADS):
        g, jj = divmod(h, GQ)
        qs_ref[g, jj * tq:(jj + 1) * tq, :] = q_ref[:, h * HEAD_DIM:(h + 1) * HEAD_DIM]

    if keys_on_rows:
        wi_t = kiwi_ref[...].T[IDX_DIM:IDX_DIM + IDX_HEADS, :]
        for hp in range(IDX_HEADS // 2):
            for u in range(2):
                h = 2 * hp + u
                qis_ref[hp, u * tq:(u + 1) * tq, :] = qi_ref[:, h * IDX_DIM:(h + 1) * IDX_DIM]
    else:
        wi = kiwi_ref[:, IDX_DIM:IDX_DIM + IDX_HEADS]
        for h in range(IDX_HEADS):
            qis_ref[0, h * tq:(h + 1) * tq, :] = qi_ref[:, h * IDX_DIM:(h + 1) * IDX_DIM]
            wcol_ref[h * tq:(h + 1) * tq, :] = wi[:, h:h + 1]

    def visible_scores(sc, kpos):
        vis = (kpos >= 0) & (jnp.right_shift(kpos, CHUNK_SHIFT) <= qchunk)
        return jnp.where(vis, sc, -jnp.inf)

    def score_body(j, carry):
        if keys_on_rows:
            for k0 in range(0, w, w // 2):
                kib = ki_ref[pl.ds(win_row(j) + k0, w // 2), :]
                sc = jnp.zeros((w // 2, tq), F32)
                for hp in range(IDX_HEADS // 2):
                    s2 = jnp.maximum(lax.dot_general(kib, qis_ref[hp], nt, preferred_element_type=F32), 0.0)
                    sc = sc + s2[:, 0:tq] * wi_t[2 * hp:2 * hp + 1, :]
                    sc = sc + s2[:, tq:2 * tq] * wi_t[2 * hp + 1:2 * hp + 2, :]
                kpos = (lv - w * (j + 1)) + kidx[k0:k0 + w // 2]
                score_ref[j, k0:k0 + w // 2, :] = visible_scores(sc * IDX_SCALE, kpos)
        else:
            kib = ki_ref[pl.ds(win_row(j), w), :]
            s8 = jnp.maximum(lax.dot_general(qis_ref[0], kib, nt, preferred_element_type=F32), 0.0)
            s8 = s8 * wcol_ref[...]
            sc = s8[0:tq]
            for h in range(1, IDX_HEADS):
                sc = sc + s8[h * tq:(h + 1) * tq]
            score_ref[j] = visible_scores(sc * IDX_SCALE, (lv - w * (j + 1)) + kidx)
        return carry

    lax.fori_loop(0, nw, score_body, 0)

    def key_to_f32(key):
        return pltpu.bitcast(key ^ (jnp.right_shift(key, 31) & 0x7FFFFFFF), F32)

    if keys_on_rows:
        part = (4 * 8, tq)

        def fold(c):
            return jnp.sum(c.reshape(w // part[0], part[0], tq), axis=0)
    else:
        part = (tq, LANE)

        def fold(c):
            return (c[:, 0:LANE] + c[:, LANE:2 * LANE]) + (c[:, 2 * LANE:3 * LANE] + c[:, 3 * LANE:4 * LANE])

    def total(acc):
        return jnp.sum(acc, axis=kax, keepdims=True)

    def count_ge(cand):
        cand_f = key_to_f32(jnp.where((cand > 0) & (cand < MIN_NORMAL_KEY), MIN_NORMAL_KEY, cand))

        def body(j, acc):
            return acc + fold(jnp.where(score_ref[j] >= cand_f, 1.0, 0.0))
        return total(lax.fori_loop(0, nw, body, jnp.zeros(part, F32)))

    def step(cand, carry):
        prefix, n_lo, n_hi = carry
        c = count_ge(cand)
        take = c >= float(topk)
        return jnp.where(take, cand, prefix), jnp.where(take, c, n_lo), jnp.where(take, n_hi, c)

    def search_body(bit, carry):
        return step(carry[0] + lax.shift_left(jnp.int32(1), 30 - bit), carry)

    n_all = (nw * w).astype(F32)
    start = (jnp.full(qshape, INT_MIN, jnp.int32), jnp.full(qshape, 1.0, F32) * n_all, jnp.zeros(qshape, F32))
    sign = step(jnp.zeros(qshape, jnp.int32), start)
    tstar, n_ge, n_gt = lax.fori_loop(0, 31, search_body, sign)

    need = float(topk) - n_gt
    n_eq = n_ge - n_gt
    bad = (n_eq > need) & (tstar > KEY_NEG_INF)
    n_bad = jnp.sum(jnp.where(bad, 1.0, 0.0))
    tstar_f = key_to_f32(tstar)

    @pl.when(n_bad > 0.0)
    def _():
        r_i = lax.broadcasted_iota(jnp.int32, (w, w), 0)
        c_i = lax.broadcasted_iota(jnp.int32, (w, w), 1)
        earlier = jnp.where(c_i < r_i if keys_on_rows else r_i < c_i, 1.0, 0.0).astype(BF16)

        def tie_body(t, seen):
            j = nw - 1 - t
            kt = score_ref[j]
            eq = kt == tstar_f
            eqf = jnp.where(eq, 1.0, 0.0)
            if keys_on_rows:
                inwin = jnp.dot(earlier, eqf.astype(BF16), preferred_element_type=F32)
            else:
                inwin = jnp.dot(eqf.astype(BF16), earlier, preferred_element_type=F32)
            score_ref[j] = jnp.where(eq & bad & (seen + inwin >= need), -jnp.inf, kt)
            return seen + total(eqf)

        lax.fori_loop(0, nw, tie_body, jnp.zeros(qshape, F32))

    teff = key_to_f32(jnp.maximum(tstar, KEY_NEG_INF + 1))

    m_ref[...] = jnp.full(m_ref.shape, NEG_BIG, F32)
    acc_ref[...] = jnp.zeros(acc_ref.shape, F32)

    def attend(j, near):
        row = win_row(j)
        tile = jnp.where(score_ref[j] >= teff, 0.0, NEG_BIG)
        if keys_on_rows:
            tile = tile.T
        for r in range(mrows // tq):
            madd_ref[r * tq:(r + 1) * tq, :] = tile
        tiles = [(g, r0) for g in range(N_KV_HEADS) for r0 in range(0, rows, rb)]

        for t, (g, r0) in enumerate(tiles):
            kb = k_ref[g, pl.ds(row, w), :]
            s = lax.dot_general(qs_ref[g, r0:r0 + rb, :], kb, nt, preferred_element_type=F32)
            s = s + madd_ref[r0 % mrows:r0 % mrows + rb, :]
            if near:
                s = s + nb_ref[g, r0:r0 + rb, :]
            s_ref[t] = s
        for t, (g, r0) in enumerate(tiles):
            rs = slice(r0, r0 + rb)
            s = s_ref[t]
            m_old = m_ref[g, rs, :]
            m_new = jnp.maximum(m_old, jnp.max(s, axis=1, keepdims=True))
            p = jnp.exp2(s - jnp.concatenate([m_new] * (w // LANE), axis=1))
            vb = v_ref[g, pl.ds(row, w), :]
            acc_ref[g, rs, :] = (jnp.exp2(m_old - m_new) * acc_ref[g, rs, :]
                                 + jnp.dot(p.astype(BF16), vb, preferred_element_type=F32))
            m_ref[g, rs, :] = m_new

    attend(0, True)

    def far_body(j, carry):
        attend(j, False)
        return carry

    lax.fori_loop(1, nw, far_body, 0)

    for h in range(N_HEADS):
        g, jj = divmod(h, GQ)
        a = acc_ref[g, jj * tq:(jj + 1) * tq, :]
        o_ref[:, h * HEAD_DIM:(h + 1) * HEAD_DIM] = (
            a[:, 0:HEAD_DIM] / a[:, HEAD_DIM:HEAD_DIM + 1]).astype(o_ref.dtype)


def _rel_bucket(rel):
    half = NUM_BUCKETS // 2
    max_exact = half // 2
    base = jnp.where(rel > 0, half, 0)
    n = jnp.abs(rel)
    nf = jnp.maximum(n, 1).astype(jnp.float32)
    large = max_exact + jnp.floor(jnp.log(nf / max_exact) / math.log(MAX_DISTANCE / max_exact)
                                  * (half - max_exact)).astype(jnp.int32)
    large = jnp.minimum(large, half - 1)
    return base + jnp.where(n < max_exact, n, large)


def _near_bias(rel_bias, tq):
    r = jnp.arange(tq, dtype=jnp.int32)[:, None]
    c = jnp.arange(KEY_WINDOW, dtype=jnp.int32)[None, :]
    rel = (tq - KEY_WINDOW + c) - r
    tab = rel_bias.astype(F32)

    def pick(bucket):
        bucket = jnp.bitwise_and(bucket, NUM_BUCKETS - 1)
        d = bucket[..., None].astype(F32) - jnp.arange(NUM_BUCKETS, dtype=F32)
        return jnp.maximum(1.0 - jnp.abs(d), 0.0)

    far = _rel_bucket(jnp.full((1,), -(1 << 20), jnp.int32))
    far_row = jnp.dot(pick(far), tab, precision=lax.Precision.HIGHEST)
    nb = jnp.einsum('qcb,bh->hqc', pick(_rel_bucket(rel)), (tab - far_row) * LOG2E,
                    precision=lax.Precision.HIGHEST)
    return nb.reshape(N_KV_HEADS, GQ * tq, KEY_WINDOW)


def _dsa_mixer(projb, kiwi, past_k, past_v, past_ik, rel_bias, b, t):
    past = past_k.shape[1]
    n_keys = past + t
    topk = min(TOPK_MAX, n_keys // 4)
    tq = 128 if t % 128 == 0 else t
    assert KEY_WINDOW % tq == 0 and past % KEY_WINDOW == 0 and tq % 16 == 0 and n_keys >= KEY_WINDOW
    assert t % min(t, KEY_WINDOW) == 0
    front_pad = KEY_WINDOW - tq
    lp = front_pad + n_keys
    nq = t // tq
    nw_max = (n_keys + KEY_WINDOW - 1) // KEY_WINDOW
    rows = GQ * tq
    rb = min(rows, MAX_TILE_ROWS)
    keys_on_rows = tq % LANE == 0
    assert keys_on_rows or IDX_HEADS * tq <= LANE
    kvw = N_KV_HEADS * HEAD_DIM
    qw = N_HEADS * HEAD_DIM
    qiw = IDX_HEADS * IDX_DIM

    qrow = lambda bi, i: (bi * nq + i, 0)
    in_specs = [pl.BlockSpec((tq, qw), qrow),
                pl.BlockSpec((tq, qiw), lambda bi, i: (bi * nq + i, QI_OFF // qiw)),
                pl.BlockSpec((tq, LANE), qrow),
                pl.BlockSpec((t, kvw), lambda bi, i: (bi, K_OFF // kvw)),
                pl.BlockSpec((t, kvw), lambda bi, i: (bi, V_OFF // kvw)),
                pl.BlockSpec((t, LANE), lambda bi, i: (bi, KI_OFF // LANE))]
    args = [projb, projb, kiwi, projb, projb, projb]
    if past:
        in_specs += [pl.BlockSpec((None, past, kvw), lambda bi, i: (bi, 0, 0)),
                     pl.BlockSpec((None, past, kvw), lambda bi, i: (bi, 0, 0)),
                     pl.BlockSpec((None, past, IDX_DIM), lambda bi, i: (bi, 0, 0))]
        args += [past_k.reshape(b, past, kvw), past_v.reshape(b, past, kvw), past_ik]
    in_specs.append(pl.BlockSpec((N_KV_HEADS, rows, KEY_WINDOW), lambda bi, i: (0, 0, 0)))
    args.append(_near_bias(rel_bias, tq))

    kern = functools.partial(_attn_kernel, tq=tq, t=t, past=past, front_pad=front_pad, topk=topk)
    return pl.pallas_call(
        kern,
        out_shape=jax.ShapeDtypeStruct((b * t, qw), BF16),
        grid=(b, nq),
        in_specs=in_specs,
        out_specs=pl.BlockSpec((tq, qw), qrow),
        scratch_shapes=[pltpu.VMEM((N_KV_HEADS, lp, HEAD_DIM), BF16),
                        pltpu.VMEM((N_KV_HEADS, lp, LANE), BF16),
                        pltpu.VMEM((lp, IDX_DIM), BF16),
                        pltpu.VMEM((nw_max, KEY_WINDOW, tq) if keys_on_rows
                                   else (nw_max, tq, KEY_WINDOW), F32),
                        pltpu.VMEM((N_KV_HEADS, rows, HEAD_DIM), BF16),
                        pltpu.VMEM((IDX_HEADS // 2, 2 * tq, IDX_DIM) if keys_on_rows
                                   else (1, IDX_HEADS * tq, IDX_DIM), BF16),
                        pltpu.VMEM((8, 1) if keys_on_rows else (IDX_HEADS * tq, 1), F32),
                        pltpu.VMEM((max(rb, tq), KEY_WINDOW), F32),
                        pltpu.VMEM((N_KV_HEADS * rows // rb, rb, KEY_WINDOW), F32),
                        pltpu.VMEM((N_KV_HEADS, rows, LANE), F32),
                        pltpu.VMEM((N_KV_HEADS, rows, LANE), F32)],
        compiler_params=_params("arbitrary", "arbitrary"),
        name="dsa_attention",
    )(*args)


def _ssd_kernel(z_ref, xlo_ref, xhi_ref, dt_ref, cst_ref, st0_ref, cw_ref, cb_ref, dtb_ref, alog_ref,
                dexp_ref, nw_ref, expand_ref, y_ref, ncv_ref, stout_ref,
                xpad_ref, act_ref, yd_ref, st_ref, *, q):
    c = pl.program_id(1)
    half = CONV_DIM // 2

    @pl.when(c == 0)
    def _():
        xpad_ref[8 - (D_CONV - 1):8, :] = cst_ref[...]
        for h in range(SSM_HEADS):
            g, jj = divmod(h, HEADS_PER_GROUP)
            st_ref[g, :, jj * SSM_HEAD_DIM:(jj + 1) * SSM_HEAD_DIM] = st0_ref[h].T

    cc = 512
    for c0 in range(0, CONV_DIM, cc):
        cs = slice(c0, c0 + cc)
        src = xlo_ref[:, c0:c0 + cc] if c0 < half else xhi_ref[:, c0 - half:c0 - half + cc]
        xpad_ref[8:8 + q, cs] = src
        conv = cb_ref[:, cs] + xpad_ref[5:5 + q, cs] * cw_ref[0:1, cs]
        conv = conv + xpad_ref[6:6 + q, cs] * cw_ref[1:2, cs]
        conv = conv + xpad_ref[7:7 + q, cs] * cw_ref[2:3, cs]
        conv = conv + xpad_ref[8:8 + q, cs] * cw_ref[3:4, cs]
        act_ref[:, cs] = conv * _sigmoid(conv)
        xpad_ref[0:8, cs] = xpad_ref[q:q + 8, cs]

    hi = lax.Precision.HIGHEST
    x_dt = dt_ref[:, 0:SSM_HEADS] + dtb_ref[...]
    dt = jnp.maximum(x_dt, 0.0) + jnp.log1p(jnp.exp(-jnp.abs(x_dt)))
    a = dt * (-jnp.exp(alog_ref[...]))
    rr = lax.broadcasted_iota(jnp.int32, (q, q), 0)
    cl = lax.broadcasted_iota(jnp.int32, (q, q), 1)
    causal = rr >= cl
    tri = jnp.where(causal, 1.0, 0.0)
    a_cs = jnp.dot(tri, a, precision=hi, preferred_element_type=F32)
    eye = jnp.where(lax.broadcasted_iota(jnp.int32, (SSM_HEADS, SSM_HEADS), 0)
                    == lax.broadcasted_iota(jnp.int32, (SSM_HEADS, SSM_HEADS), 1), 1.0, 0.0)
    a_cs_t = lax.dot_general(eye, a_cs, (((1,), (1,)), ((), ())), precision=hi,
                             preferred_element_type=F32)
    a_end = a_cs[q - 1:q, :]
    per_head = jnp.concatenate([dt, jnp.exp(a_cs), jnp.exp(a_end - a_cs)], axis=0)
    top = per_head.astype(BF16)
    rest = (per_head - top.astype(F32)).astype(BF16)
    spread = jnp.dot(jnp.concatenate([top, rest], axis=0), expand_ref[...],
                     preferred_element_type=F32)
    dt_x = spread[0:q] + spread[3 * q:4 * q]
    e_x = spread[q:2 * q] + spread[4 * q:5 * q]
    dte_x = spread[2 * q:3 * q] + spread[5 * q:6 * q]
    dec_x = e_x[q - 1:q, :]

    for g in range(SSM_GROUPS):
        gs = slice(g * GROUP_W, (g + 1) * GROUP_W)
        bg = act_ref[:, D_INNER + g * D_STATE:D_INNER + (g + 1) * D_STATE].astype(BF16)
        cg = act_ref[:, D_INNER + (SSM_GROUPS + g) * D_STATE:
                     D_INNER + (SSM_GROUPS + g + 1) * D_STATE].astype(BF16)
        cb = lax.dot_general(cg, bg, (((1,), (1,)), ((), ())), preferred_element_type=F32)
        xs_g = act_ref[:, gs]
        xd_g = xs_g * dt_x[:, gs]
        st_g = st_ref[g]
        y_off = jnp.dot(cg, st_g.astype(BF16), preferred_element_type=F32) * e_x[:, gs]
        for jj in range(HEADS_PER_GROUP):
            h = g * HEADS_PER_GROUP + jj
            seg = a_cs[:, h:h + 1] - a_cs_t[h:h + 1, :]
            decay = jnp.where(causal, jnp.exp(seg), 0.0)
            hs = slice(jj * SSM_HEAD_DIM, (jj + 1) * SSM_HEAD_DIM)
            yd_ref[:, hs] = jnp.dot((cb * decay).astype(BF16), xd_g[:, hs].astype(BF16),
                                    preferred_element_type=F32)
        st_ref[g] = st_g * dec_x[:, gs] + lax.dot_general(
            bg, (xd_g * dte_x[:, gs]).astype(BF16), (((0,), (0,)), ((), ())),
            preferred_element_type=F32)
        y = yd_ref[...] + y_off + xs_g * dexp_ref[:, gs]
        zg = z_ref[:, gs]
        yg = y * (zg * _sigmoid(zg))
        ms = jnp.mean(yg * yg, axis=-1, keepdims=True)
        y_ref[:, gs] = (yg * lax.rsqrt(ms + EPS) * nw_ref[:, gs]).astype(y_ref.dtype)

    @pl.when(c == pl.num_programs(1) - 1)
    def _():
        ncv_ref[...] = xpad_ref[8 - (D_CONV - 1):8, :]
        for h in range(SSM_HEADS):
            g, jj = divmod(h, HEADS_PER_GROUP)
            stout_ref[h] = st_ref[g, :, jj * SSM_HEAD_DIM:(jj + 1) * SSM_HEAD_DIM].T


def _ssd_mixer(proj, conv_state, ssm_state, conv_w, conv_b, dt_bias, a_log, d_skip, norm_w, b, t):
    q = 128 if t % 128 == 0 else t
    assert q % 8 == 0 and t >= D_CONV - 1 and CONV_DIM == 2 * D_INNER
    nc = t // q
    expand = jnp.repeat(jnp.eye(SSM_HEADS, dtype=BF16), SSM_HEAD_DIM, axis=1)
    dexp = jnp.repeat(d_skip.astype(F32), SSM_HEAD_DIM)[None, :]
    row = lambda k: (lambda bi, c: (bi * nc + c, k))
    const2 = lambda bi, c: (0, 0)
    state_spec = pl.BlockSpec((None, SSM_HEADS, SSM_HEAD_DIM, D_STATE), lambda bi, c: (bi, 0, 0, 0))
    conv_spec = pl.BlockSpec((None, D_CONV - 1, CONV_DIM), lambda bi, c: (bi, 0, 0))
    y, new_conv, new_ssm = pl.pallas_call(
        functools.partial(_ssd_kernel, q=q),
        out_shape=[jax.ShapeDtypeStruct((b * t, D_INNER), BF16),
                   jax.ShapeDtypeStruct((b, D_CONV - 1, CONV_DIM), F32),
                   jax.ShapeDtypeStruct((b, SSM_HEADS, SSM_HEAD_DIM, D_STATE), F32)],
        grid=(b, nc),
        in_specs=[pl.BlockSpec((q, D_INNER), row(SSM_Z_OFF // D_INNER)),
                  pl.BlockSpec((q, D_INNER), row(SSM_XBC_OFF // D_INNER)),
                  pl.BlockSpec((q, D_INNER), row(SSM_XBC_OFF // D_INNER + 1)),
                  pl.BlockSpec((q, LANE), row(SSM_DT_OFF // LANE)),
                  conv_spec,
                  state_spec,
                  pl.BlockSpec((D_CONV, CONV_DIM), const2),
                  pl.BlockSpec((1, CONV_DIM), const2),
                  pl.BlockSpec((1, SSM_HEADS), const2),
                  pl.BlockSpec((1, SSM_HEADS), const2),
                  pl.BlockSpec((1, D_INNER), const2),
                  pl.BlockSpec((1, D_INNER), const2),
                  pl.BlockSpec((SSM_HEADS, D_INNER), const2)],
        out_specs=[pl.BlockSpec((q, D_INNER), row(0)), conv_spec, state_spec],
        scratch_shapes=[pltpu.VMEM((q + 8, CONV_DIM), F32),
                        pltpu.VMEM((q, CONV_DIM), F32),
                        pltpu.VMEM((q, GROUP_W), F32),
                        pltpu.VMEM((SSM_GROUPS, D_STATE, GROUP_W), F32)],
        compiler_params=_params("arbitrary", "arbitrary"),
        name="ssd_mixer",
    )(proj, proj, proj, proj, conv_state.astype(F32), ssm_state.astype(F32), conv_w.astype(F32),
      conv_b.astype(F32)[None, :], dt_bias.astype(F32)[None, :], a_log.astype(F32)[None, :], dexp,
      norm_w.astype(F32)[None, :], expand)
    return y, new_conv, new_ssm.astype(ssm_state.dtype)


def _tile_m(m):
    for tm in (1024, 512, 256, 128, 64, 32, 16, 8):
        if m % tm == 0:
            return tm
    raise ValueError(m)


def _pad_cols(w, n):
    return jnp.pad(w, ((0, 0), (0, n - w.shape[1])))


def _prep_weights(norm_mix, norm_ffn, norm_final, attn_w_in, attn_w_o, ssm_w_in, ssm_w_out,
                  mlp_w_up, mlp_w_down):
    col_scale = jnp.where(jnp.arange(ATTN_COLS) < K_OFF, ATTN_SCALE * LOG2E, 1.0).astype(F32)
    return dict(
        attn_in=_pad_cols(attn_w_in[0] * col_scale, _round_up(ATTN_COLS, LANE)).astype(BF16),
        attn_o=attn_w_o[0].astype(BF16),
        ssm_in=_pad_cols(ssm_w_in[0], _round_up(SSM_COLS, LANE)).astype(BF16),
        ssm_out=ssm_w_out[0].astype(BF16),
        up=[mlp_w_up[i].astype(BF16) for i in range(2)],
        down=[mlp_w_down[i].astype(BF16) for i in range(2)],
        g_mix=[norm_mix[i].astype(F32)[None, :] for i in range(2)],
        g_ffn=[norm_ffn[i].astype(F32)[None, :] for i in range(2)],
        g_final=norm_final.astype(F32)[None, :],
    )


def _trunk(x, past_k, past_v, past_ik, conv_st, ssm_st, rel_bias, wts, ssm_conv_w, ssm_conv_b,
           ssm_dt_bias, ssm_a_log, ssm_d, ssm_norm):
    b, t, d = x.shape
    m = b * t
    tm = _tile_m(m)
    x2 = x.reshape(m, d).astype(F32)

    projb, k_new, v_new, ki_new, kiwi = _attn_proj(x2, wts["g_mix"][0], wts["attn_in"], tm=min(tm, 512))
    ao = _dsa_mixer(projb, kiwi, past_k, past_v, past_ik, rel_bias, b, t)
    x2 = _layer_tail(ao, wts["attn_o"], x2, wts["g_ffn"][0], wts["up"][0], wts["down"][0],
                     wts["g_final"], tm=min(tm, 512), tf=1024, final_norm=False)

    proj = _norm_matmul(x2, wts["g_mix"][1], wts["ssm_in"], tm=min(tm, 256), tn=wts["ssm_in"].shape[1])
    y, new_conv, new_ssm = _ssd_mixer(proj, conv_st, ssm_st, ssm_conv_w, ssm_conv_b, ssm_dt_bias,
                                      ssm_a_log, ssm_d, ssm_norm, b, t)
    x2 = _layer_tail(y, wts["ssm_out"], x2, wts["g_ffn"][1], wts["up"][1], wts["down"][1],
                     wts["g_final"], tm=min(tm, 512), tf=1024, final_norm=True)

    dt = x.dtype
    return (x2.reshape(b, t, d).astype(dt),
            k_new.reshape(1, b, t, N_KV_HEADS, HEAD_DIM).astype(dt),
            v_new.reshape(1, b, t, N_KV_HEADS, HEAD_DIM).astype(dt),
            ki_new.reshape(1, b, t, IDX_DIM).astype(dt), new_conv[None].astype(dt), new_ssm[None])


def kernel(x_prompt, x_sample, cache_k, cache_v, cache_idx_k, state_conv, state_ssm, rel_bias, norm_mix, norm_ffn, norm_final, attn_w_in, attn_w_o, ssm_w_in, ssm_conv_w, ssm_conv_b, ssm_dt_bias, ssm_a_log, ssm_d, ssm_norm, ssm_w_out, mlp_w_up, mlp_w_down):
    wts = _prep_weights(norm_mix, norm_ffn, norm_final, attn_w_in, attn_w_o, ssm_w_in, ssm_w_out,
                        mlp_w_up, mlp_w_down)
    bp = x_prompt.shape[0]
    dtp = x_prompt.dtype
    empty_k = jnp.zeros((bp, 0, N_KV_HEADS, HEAD_DIM), dtp)
    empty_ik = jnp.zeros((bp, 0, IDX_DIM), dtp)
    zero_conv = jnp.zeros((bp, D_CONV - 1, CONV_DIM), dtp)
    zero_ssm = jnp.zeros((bp, SSM_HEADS, SSM_HEAD_DIM, D_STATE), dtp)
    args = (rel_bias, wts, ssm_conv_w[0], ssm_conv_b[0], ssm_dt_bias[0], ssm_a_log[0], ssm_d[0],
            ssm_norm[0])
    yp, kp, vp, ikp, cp, sp = _trunk(x_prompt, empty_k, empty_k, empty_ik, zero_conv, zero_ssm, *args)
    ys, ks, vs, iks, cs, ss = _trunk(x_sample, cache_k[0], cache_v[0], cache_idx_k[0],
                                     state_conv[0], state_ssm[0], *args)
    return (yp, ys, kp, vp, ikp, cp, sp, ks, vs, iks, cs, ss)
```

```python
import functools
import math

import jax
import jax.numpy as jnp
from jax import lax
from jax.experimental import pallas as pl
from jax.experimental.pallas import tpu as pltpu

F32 = jnp.float32
BF16 = jnp.bfloat16

D_MODEL = 1024
CHUNK = 64
CHUNK_SHIFT = 6
N_HEADS = 16
HEAD_DIM = 64
N_KV_HEADS = 4
GQ = N_HEADS // N_KV_HEADS
IDX_HEADS = 8
IDX_DIM = 64
TOPK_MAX = 256
IDX_SCALE = (IDX_HEADS * IDX_DIM) ** -0.5
ATTN_SCALE = HEAD_DIM ** -0.5
NUM_BUCKETS = 32
MAX_DISTANCE = 128
D_INNER = 2 * D_MODEL
SSM_HEAD_DIM = 64
SSM_HEADS = D_INNER // SSM_HEAD_DIM
SSM_GROUPS = 8
HEADS_PER_GROUP = SSM_HEADS // SSM_GROUPS
GROUP_W = HEADS_PER_GROUP * SSM_HEAD_DIM
D_STATE = 128
D_CONV = 4
CONV_DIM = D_INNER + 2 * SSM_GROUPS * D_STATE
D_FF = 4 * D_MODEL
EPS = 1e-6

Q_OFF = 0
K_OFF = N_HEADS * HEAD_DIM
V_OFF = K_OFF + N_KV_HEADS * HEAD_DIM
QI_OFF = V_OFF + N_KV_HEADS * HEAD_DIM
KI_OFF = QI_OFF + IDX_HEADS * IDX_DIM
WI_OFF = KI_OFF + IDX_DIM
ATTN_COLS = WI_OFF + IDX_HEADS

SSM_Z_OFF = 0
SSM_XBC_OFF = D_INNER
SSM_DT_OFF = D_INNER + CONV_DIM
SSM_COLS = SSM_DT_OFF + SSM_HEADS

LANE = 128
KEY_WINDOW = 512
MAX_TILE_ROWS = 512
LOG2E = 1.4426950408889634
VMEM_LIMIT = 56 * 1024 * 1024

NEG_BIG = -1e30
KEY_NEG_INF = -2139095041
MIN_NORMAL_KEY = 1 << 23
INT_MIN = -2147483648


def _round_up(n, m):
    return (n + m - 1) // m * m


def _rms(x, g):
    ms = jnp.mean(x * x, axis=-1, keepdims=True)
    return x * lax.rsqrt(ms + EPS) * g


def _sigmoid(x):
    return 1.0 / (1.0 + jnp.exp(-x))


def _params(*sem):
    return pltpu.CompilerParams(dimension_semantics=sem, vmem_limit_bytes=VMEM_LIMIT)


def _norm_matmul_kernel(x_ref, g_ref, w_ref, o_ref, xn_ref):
    @pl.when(pl.program_id(1) == 0)
    def _():
        xn_ref[...] = _rms(x_ref[...], g_ref[...]).astype(BF16)

    o_ref[...] = jnp.dot(xn_ref[...], w_ref[...], preferred_element_type=F32)


def _norm_matmul(x, g, w, *, tm, tn):
    m, k = x.shape
    n = w.shape[1]
    return pl.pallas_call(
        _norm_matmul_kernel,
        out_shape=jax.ShapeDtypeStruct((m, n), F32),
        grid=(m // tm, n // tn),
        in_specs=[pl.BlockSpec((tm, k), lambda i, j: (i, 0)),
                  pl.BlockSpec((1, k), lambda i, j: (0, 0)),
                  pl.BlockSpec((k, tn), lambda i, j: (0, j))],
        out_specs=pl.BlockSpec((tm, tn), lambda i, j: (i, j)),
        scratch_shapes=[pltpu.VMEM((tm, k), BF16)],
        compiler_params=_params("parallel", "arbitrary"),
        name="norm_matmul",
    )(x, g, w)


def _attn_proj_kernel(x_ref, g_ref, w_ref, pb_ref, k_ref, v_ref, ki_ref, kiwi_ref):
    xn = _rms(x_ref[...], g_ref[...]).astype(BF16)
    acc = jnp.dot(xn, w_ref[...], preferred_element_type=F32)
    pb_ref[...] = acc.astype(BF16)
    for g in range(N_KV_HEADS):
        k_ref[:, g, :] = acc[:, K_OFF + g * HEAD_DIM:K_OFF + (g + 1) * HEAD_DIM]
        v_ref[:, g, :] = acc[:, V_OFF + g * HEAD_DIM:V_OFF + (g + 1) * HEAD_DIM]
    ki_ref[...] = acc[:, KI_OFF:WI_OFF]
    kiwi_ref[...] = acc[:, KI_OFF:KI_OFF + LANE]


def _attn_proj(x, g, w, *, tm):
    m, k = x.shape
    n = w.shape[1]
    row = lambda i: (i, 0)
    kv_spec = pl.BlockSpec((tm, N_KV_HEADS, HEAD_DIM), lambda i: (i, 0, 0))
    return pl.pallas_call(
        _attn_proj_kernel,
        out_shape=[jax.ShapeDtypeStruct((m, n), BF16),
                   jax.ShapeDtypeStruct((m, N_KV_HEADS, HEAD_DIM), F32),
                   jax.ShapeDtypeStruct((m, N_KV_HEADS, HEAD_DIM), F32),
                   jax.ShapeDtypeStruct((m, IDX_DIM), F32),
                   jax.ShapeDtypeStruct((m, LANE), F32)],
        grid=(m // tm,),
        in_specs=[pl.BlockSpec((tm, k), row),
                  pl.BlockSpec((1, k), lambda i: (0, 0)),
                  pl.BlockSpec((k, n), lambda i: (0, 0))],
        out_specs=[pl.BlockSpec((tm, n), row), kv_spec, kv_spec,
                   pl.BlockSpec((tm, IDX_DIM), row), pl.BlockSpec((tm, LANE), row)],
        compiler_params=_params("parallel"),
        name="attn_proj",
    )(x, g, w)


def _tail_kernel(a_ref, wo_ref, r_ref, g_ref, wu_ref, wd_ref, gf_ref, o_ref, *, tf, final_norm):
    x1 = r_ref[...] + jnp.dot(a_ref[...], wo_ref[...], preferred_element_type=F32)
    xn = _rms(x1, g_ref[...]).astype(BF16)
    acc = x1
    for f0 in range(0, wu_ref.shape[1], tf):
        h = jnp.maximum(jnp.dot(xn, wu_ref[:, f0:f0 + tf], preferred_element_type=F32), 0.0)
        acc = acc + jnp.dot((h * h).astype(BF16), wd_ref[f0:f0 + tf, :], preferred_element_type=F32)
    if final_norm:
        acc = _rms(acc, gf_ref[...])
    o_ref[...] = acc


def _layer_tail(a, wo, resid, g, wu, wd, gf, *, tm, tf, final_norm):
    m, ka = a.shape
    d = wo.shape[1]
    ff = wu.shape[1]
    row = lambda i: (i, 0)
    whole = lambda shape: pl.BlockSpec(shape, lambda i: (0, 0), pipeline_mode=pl.Buffered(1))
    return pl.pallas_call(
        functools.partial(_tail_kernel, tf=tf, final_norm=final_norm),
        out_shape=jax.ShapeDtypeStruct((m, d), F32),
        grid=(m // tm,),
        in_specs=[pl.BlockSpec((tm, ka), row), whole((ka, d)), pl.BlockSpec((tm, d), row),
                  whole((1, d)), whole((d, ff)), whole((ff, d)), whole((1, d))],
        out_specs=pl.BlockSpec((tm, d), row),
        compiler_params=_params("parallel"),
        name="layer_tail",
    )(a, wo, resid, g, wu, wd, gf)


def _attn_kernel(*refs, tq, t, past, front_pad, topk):
    if past:
        (q_ref, qi_ref, kiwi_ref, kn_ref, vn_ref, kin_ref, pk_ref, pv_ref, pki_ref, nb_ref, o_ref,
         k_ref, v_ref, ki_ref, score_ref, qs_ref, qis_ref, wcol_ref, madd_ref, s_ref, m_ref, acc_ref) = refs
    else:
        (q_ref, qi_ref, kiwi_ref, kn_ref, vn_ref, kin_ref, nb_ref, o_ref,
         k_ref, v_ref, ki_ref, score_ref, qs_ref, qis_ref, wcol_ref, madd_ref, s_ref, m_ref, acc_ref) = refs
    w = KEY_WINDOW
    rows = GQ * tq
    rb = s_ref.shape[1]
    mrows = madd_ref.shape[0]
    keys_on_rows = score_ref.shape[1] == w
    kax = 0 if keys_on_rows else 1
    qshape = (1, tq) if keys_on_rows else (tq, 1)
    i = pl.program_id(1)
    q0 = past + i * tq
    lv = q0 + tq
    nw = jnp.right_shift(lv + (w - 1), 9)
    qchunk = jnp.right_shift(q0 + lax.broadcasted_iota(jnp.int32, qshape, 1 - kax), CHUNK_SHIFT)
    kidx = lax.broadcasted_iota(jnp.int32, (w, 1) if keys_on_rows else (1, w), kax)
    nt = (((1,), (1,)), ((), ()))

    def win_row(j):
        return pl.multiple_of(lv + front_pad - w * (j + 1), tq)

    @pl.when(i == 0)
    def _():
        ones_col = jnp.where(lax.broadcasted_iota(jnp.int32, (1, LANE - HEAD_DIM), 1) == 0, 1.0, 0.0)

        def put(r0, n, kc, vc, kic):
            ones = jnp.broadcast_to(ones_col, (n, LANE - HEAD_DIM)).astype(BF16)
            for g in range(N_KV_HEADS):
                gs = slice(g * HEAD_DIM, (g + 1) * HEAD_DIM)
                k_ref[g, r0:r0 + n, :] = kc[:, gs]
                v_ref[g, r0:r0 + n, :] = jnp.concatenate([vc[:, gs], ones], axis=1)
            ki_ref[r0:r0 + n, :] = kic

        kvw = N_KV_HEADS * HEAD_DIM
        if front_pad:
            put(0, front_pad, jnp.zeros((front_pad, kvw), BF16), jnp.zeros((front_pad, kvw), BF16),
                jnp.zeros((front_pad, IDX_DIM), BF16))
        step = min(t, w)
        for c0 in range(0, past, w):
            put(front_pad + c0, w, pk_ref[c0:c0 + w, :].astype(BF16), pv_ref[c0:c0 + w, :].astype(BF16),
                pki_ref[c0:c0 + w, :].astype(BF16))
        for c0 in range(0, t, step):
            put(front_pad + past + c0, step, kn_ref[c0:c0 + step, :], vn_ref[c0:c0 + step, :],
                kin_ref[c0:c0 + step, 0:IDX_DIM])

    for h in range(N_HEADS):
        g, jj = divmod(h, GQ)
        qs_ref[g, jj * tq:(jj + 1) * tq, :] = q_ref[:, h * HEAD_DIM:(h + 1) * HEAD_DIM]

    if keys_on_rows:
        wi_t = kiwi_ref[...].T[IDX_DIM:IDX_DIM + IDX_HEADS, :]
        for hp in range(IDX_HEADS // 2):
            for u in range(2):
                h = 2 * hp + u
                qis_ref[hp, u * tq:(u + 1) * tq, :] = qi_ref[:, h * IDX_DIM:(h + 1) * IDX_DIM]
    else:
        wi = kiwi_ref[:, IDX_DIM:IDX_DIM + IDX_HEADS]
        for h in range(IDX_HEADS):
            qis_ref[0, h * tq:(h + 1) * tq, :] = qi_ref[:, h * IDX_DIM:(h + 1) * IDX_DIM]
            wcol_ref[h * tq:(h + 1) * tq, :] = wi[:, h:h + 1]

    def visible_scores(sc, kpos):
        vis = (kpos >= 0) & (jnp.right_shift(kpos, CHUNK_SHIFT) <= qchunk)
        return jnp.where(vis, sc, -jnp.inf)

    def score_body(j, carry):
        if keys_on_rows:
            for k0 in range(0, w, w // 2):
                kib = ki_ref[pl.ds(win_row(j) + k0, w // 2), :]
                sc = jnp.zeros((w // 2, tq), F32)
                for hp in range(IDX_HEADS // 2):
                    s2 = jnp.maximum(lax.dot_general(kib, qis_ref[hp], nt, preferred_element_type=F32), 0.0)
                    sc = sc + s2[:, 0:tq] * wi_t[2 * hp:2 * hp + 1, :]
                    sc = sc + s2[:, tq:2 * tq] * wi_t[2 * hp + 1:2 * hp + 2, :]
                kpos = (lv - w * (j + 1)) + kidx[k0:k0 + w // 2]
                score_ref[j, k0:k0 + w // 2, :] = visible_scores(sc * IDX_SCALE, kpos)
        else:
            kib = ki_ref[pl.ds(win_row(j), w), :]
            s8 = jnp.maximum(lax.dot_general(qis_ref[0], kib, nt, preferred_element_type=F32), 0.0)
            s8 = s8 * wcol_ref[...]
            sc = s8[0:tq]
            for h in range(1, IDX_HEADS):
                sc = sc + s8[h * tq:(h + 1) * tq]
            score_ref[j] = visible_scores(sc * IDX_SCALE, (lv - w * (j + 1)) + kidx)
        return carry

    lax.fori_loop(0, nw, score_body, 0)

    def key_to_f32(key):
        return pltpu.bitcast(key ^ (jnp.right_shift(key, 31) & 0x7FFFFFFF), F32)

    if keys_on_rows:
        part = (4 * 8, tq)

        def fold(c):
            return jnp.sum(c.reshape(w // part[0], part[0], tq), axis=0)
    else:
        part = (tq, LANE)

        def fold(c):
            return (c[:, 0:LANE] + c[:, LANE:2 * LANE]) + (c[:, 2 * LANE:3 * LANE] + c[:, 3 * LANE:4 * LANE])

    def total(acc):
        return jnp.sum(acc, axis=kax, keepdims=True)

    def count_ge(cand):
        cand_f = key_to_f32(jnp.where((cand > 0) & (cand < MIN_NORMAL_KEY), MIN_NORMAL_KEY, cand))

        def body(j, acc):
            return acc + fold(jnp.where(score_ref[j] >= cand_f, 1.0, 0.0))
        return total(lax.fori_loop(0, nw, body, jnp.zeros(part, F32)))

    def step(cand, carry):
        prefix, n_lo, n_hi = carry
        c = count_ge(cand)
        take = c >= float(topk)
        return jnp.where(take, cand, prefix), jnp.where(take, c, n_lo), jnp.where(take, n_hi, c)

    def search_body(bit, carry):
        return step(carry[0] + lax.shift_left(jnp.int32(1), 30 - bit), carry)

    n_all = (nw * w).astype(F32)
    start = (jnp.full(qshape, INT_MIN, jnp.int32), jnp.full(qshape, 1.0, F32) * n_all, jnp.zeros(qshape, F32))
    sign = step(jnp.zeros(qshape, jnp.int32), start)
    tstar, n_ge, n_gt = lax.fori_loop(0, 31, search_body, sign)

    need = float(topk) - n_gt
    n_eq = n_ge - n_gt
    bad = (n_eq > need) & (tstar > KEY_NEG_INF)
    n_bad = jnp.sum(jnp.where(bad, 1.0, 0.0))
    tstar_f = key_to_f32(tstar)

    @pl.when(n_bad > 0.0)
    def _():
        r_i = lax.broadcasted_iota(jnp.int32, (w, w), 0)
        c_i = lax.broadcasted_iota(jnp.int32, (w, w), 1)
        earlier = jnp.where(c_i < r_i if keys_on_rows else r_i < c_i, 1.0, 0.0).astype(BF16)

        def tie_body(t, seen):
            j = nw - 1 - t
            kt = score_ref[j]
            eq = kt == tstar_f
            eqf = jnp.where(eq, 1.0, 0.0)
            if keys_on_rows:
                inwin = jnp.dot(earlier, eqf.astype(BF16), preferred_element_type=F32)
            else:
                inwin = jnp.dot(eqf.astype(BF16), earlier, preferred_element_type=F32)
            score_ref[j] = jnp.where(eq & bad & (seen + inwin >= need), -jnp.inf, kt)
            return seen + total(eqf)

        lax.fori_loop(0, nw, tie_body, jnp.zeros(qshape, F32))

    teff = key_to_f32(jnp.maximum(tstar, KEY_NEG_INF + 1))

    def attend(j, near):
        row = win_row(j)
        tile = jnp.where(score_ref[j] >= teff, 0.0, NEG_BIG)
        if keys_on_rows:
            tile = tile.T
        for r in range(mrows // tq):
            madd_ref[r * tq:(r + 1) * tq, :] = tile
        tiles = [(g, r0) for g in range(N_KV_HEADS) for r0 in range(0, rows, rb)]

        for t, (g, r0) in enumerate(tiles):
            kb = k_ref[g, pl.ds(row, w), :]
            s = lax.dot_general(qs_ref[g, r0:r0 + rb, :], kb, nt, preferred_element_type=F32)
            s = s + madd_ref[r0 % mrows:r0 % mrows + rb, :]
            if near:
                s = s + nb_ref[g, r0:r0 + rb, :]
            s_ref[t] = s
        for t, (g, r0) in enumerate(tiles):
            rs = slice(r0, r0 + rb)
            s = s_ref[t]
            m_new = jnp.broadcast_to(jnp.max(s, axis=1, keepdims=True), (rb, LANE))
            if not near:
                m_old = m_ref[g, rs, :]
                m_new = jnp.maximum(m_old, m_new)
            p = jnp.exp2(s - jnp.concatenate([m_new] * (w // LANE), axis=1))
            pv = jnp.dot(p.astype(BF16), v_ref[g, pl.ds(row, w), :], preferred_element_type=F32)
            acc_ref[g, rs, :] = pv if near else jnp.exp2(m_old - m_new) * acc_ref[g, rs, :] + pv
            m_ref[g, rs, :] = m_new

    attend(0, True)

    def far_body(j, carry):
        attend(j, False)
        return carry

    lax.fori_loop(1, nw, far_body, 0)

    for h in range(N_HEADS):
        g, jj = divmod(h, GQ)
        a = acc_ref[g, jj * tq:(jj + 1) * tq, :]
        o_ref[:, h * HEAD_DIM:(h + 1) * HEAD_DIM] = (
            a[:, 0:HEAD_DIM] / a[:, HEAD_DIM:HEAD_DIM + 1]).astype(o_ref.dtype)


def _rel_bucket(rel):
    half = NUM_BUCKETS // 2
    max_exact = half // 2
    base = jnp.where(rel > 0, half, 0)
    n = jnp.abs(rel)
    nf = jnp.maximum(n, 1).astype(jnp.float32)
    large = max_exact + jnp.floor(jnp.log(nf / max_exact) / math.log(MAX_DISTANCE / max_exact)
                                  * (half - max_exact)).astype(jnp.int32)
    large = jnp.minimum(large, half - 1)
    return base + jnp.where(n < max_exact, n, large)


def _near_bias(rel_bias, tq):
    r = jnp.arange(tq, dtype=jnp.int32)[:, None]
    c = jnp.arange(KEY_WINDOW, dtype=jnp.int32)[None, :]
    rel = (tq - KEY_WINDOW + c) - r
    tab = rel_bias.astype(F32)

    def pick(bucket):
        bucket = jnp.bitwise_and(bucket, NUM_BUCKETS - 1)
        d = bucket[..., None].astype(F32) - jnp.arange(NUM_BUCKETS, dtype=F32)
        return jnp.maximum(1.0 - jnp.abs(d), 0.0)

    far = _rel_bucket(jnp.full((1,), -(1 << 20), jnp.int32))
    far_row = jnp.dot(pick(far), tab, precision=lax.Precision.HIGHEST)
    nb = jnp.einsum('qcb,bh->hqc', pick(_rel_bucket(rel)), (tab - far_row) * LOG2E,
                    precision=lax.Precision.HIGHEST)
    return nb.reshape(N_KV_HEADS, GQ * tq, KEY_WINDOW)


def _dsa_mixer(projb, kiwi, past_k, past_v, past_ik, rel_bias, b, t):
    past = past_k.shape[1]
    n_keys = past + t
    topk = min(TOPK_MAX, n_keys // 4)
    tq = 128 if t % 128 == 0 else t
    assert KEY_WINDOW % tq == 0 and past % KEY_WINDOW == 0 and tq % 16 == 0 and n_keys >= KEY_WINDOW
    assert t % min(t, KEY_WINDOW) == 0
    front_pad = KEY_WINDOW - tq
    lp = front_pad + n_keys
    nq = t // tq
    nw_max = (n_keys + KEY_WINDOW - 1) // KEY_WINDOW
    rows = GQ * tq
    rb = min(rows, MAX_TILE_ROWS)
    keys_on_rows = tq % LANE == 0
    assert keys_on_rows or IDX_HEADS * tq <= LANE
    kvw = N_KV_HEADS * HEAD_DIM
    qw = N_HEADS * HEAD_DIM
    qiw = IDX_HEADS * IDX_DIM

    qrow = lambda bi, i: (bi * nq + i, 0)
    in_specs = [pl.BlockSpec((tq, qw), qrow),
                pl.BlockSpec((tq, qiw), lambda bi, i: (bi * nq + i, QI_OFF // qiw)),
                pl.BlockSpec((tq, LANE), qrow),
                pl.BlockSpec((t, kvw), lambda bi, i: (bi, K_OFF // kvw)),
                pl.BlockSpec((t, kvw), lambda bi, i: (bi, V_OFF // kvw)),
                pl.BlockSpec((t, LANE), lambda bi, i: (bi, KI_OFF // LANE))]
    args = [projb, projb, kiwi, projb, projb, projb]
    if past:
        in_specs += [pl.BlockSpec((None, past, kvw), lambda bi, i: (bi, 0, 0)),
                     pl.BlockSpec((None, past, kvw), lambda bi, i: (bi, 0, 0)),
                     pl.BlockSpec((None, past, IDX_DIM), lambda bi, i: (bi, 0, 0))]
        args += [past_k.reshape(b, past, kvw), past_v.reshape(b, past, kvw), past_ik]
    in_specs.append(pl.BlockSpec((N_KV_HEADS, rows, KEY_WINDOW), lambda bi, i: (0, 0, 0)))
    args.append(_near_bias(rel_bias, tq))

    kern = functools.partial(_attn_kernel, tq=tq, t=t, past=past, front_pad=front_pad, topk=topk)
    return pl.pallas_call(
        kern,
        out_shape=jax.ShapeDtypeStruct((b * t, qw), BF16),
        grid=(b, nq),
        in_specs=in_specs,
        out_specs=pl.BlockSpec((tq, qw), qrow),
        scratch_shapes=[pltpu.VMEM((N_KV_HEADS, lp, HEAD_DIM), BF16),
                        pltpu.VMEM((N_KV_HEADS, lp, LANE), BF16),
                        pltpu.VMEM((lp, IDX_DIM), BF16),
                        pltpu.VMEM((nw_max, KEY_WINDOW, tq) if keys_on_rows
                                   else (nw_max, tq, KEY_WINDOW), F32),
                        pltpu.VMEM((N_KV_HEADS, rows, HEAD_DIM), BF16),
                        pltpu.VMEM((IDX_HEADS // 2, 2 * tq, IDX_DIM) if keys_on_rows
                                   else (1, IDX_HEADS * tq, IDX_DIM), BF16),
                        pltpu.VMEM((8, 1) if keys_on_rows else (IDX_HEADS * tq, 1), F32),
                        pltpu.VMEM((max(rb, tq), KEY_WINDOW), F32),
                        pltpu.VMEM((N_KV_HEADS * rows // rb, rb, KEY_WINDOW), F32),
                        pltpu.VMEM((N_KV_HEADS, rows, LANE), F32),
                        pltpu.VMEM((N_KV_HEADS, rows, LANE), F32)],
        compiler_params=_params("arbitrary", "arbitrary"),
        name="dsa_attention",
    )(*args)


def _ssd_kernel(z_ref, xlo_ref, xhi_ref, dt_ref, cst_ref, st0_ref, cw_ref, cb_ref, dtb_ref, alog_ref,
                dexp_ref, nw_ref, expand_ref, y_ref, ncv_ref, stout_ref,
                xpad_ref, act_ref, yd_ref, st_ref, *, q):
    c = pl.program_id(1)
    half = CONV_DIM // 2

    @pl.when(c == 0)
    def _():
        xpad_ref[8 - (D_CONV - 1):8, :] = cst_ref[...]
        for h in range(SSM_HEADS):
            g, jj = divmod(h, HEADS_PER_GROUP)
            st_ref[g, :, jj * SSM_HEAD_DIM:(jj + 1) * SSM_HEAD_DIM] = st0_ref[h].T

    cc = 512
    for c0 in range(0, CONV_DIM, cc):
        cs = slice(c0, c0 + cc)
        src = xlo_ref[:, c0:c0 + cc] if c0 < half else xhi_ref[:, c0 - half:c0 - half + cc]
        xpad_ref[8:8 + q, cs] = src
        conv = cb_ref[:, cs] + xpad_ref[5:5 + q, cs] * cw_ref[0:1, cs]
        conv = conv + xpad_ref[6:6 + q, cs] * cw_ref[1:2, cs]
        conv = conv + xpad_ref[7:7 + q, cs] * cw_ref[2:3, cs]
        conv = conv + xpad_ref[8:8 + q, cs] * cw_ref[3:4, cs]
        act_ref[:, cs] = conv * _sigmoid(conv)
        xpad_ref[0:8, cs] = xpad_ref[q:q + 8, cs]

    hi = lax.Precision.HIGHEST
    x_dt = dt_ref[:, 0:SSM_HEADS] + dtb_ref[...]
    dt = jnp.maximum(x_dt, 0.0) + jnp.log1p(jnp.exp(-jnp.abs(x_dt)))
    a = dt * (-jnp.exp(alog_ref[...]))
    rr = lax.broadcasted_iota(jnp.int32, (q, q), 0)
    cl = lax.broadcasted_iota(jnp.int32, (q, q), 1)
    causal = rr >= cl
    tri = jnp.where(causal, 1.0, 0.0)
    a_cs = jnp.dot(tri, a, precision=hi, preferred_element_type=F32)
    eye = jnp.where(lax.broadcasted_iota(jnp.int32, (SSM_HEADS, SSM_HEADS), 0)
                    == lax.broadcasted_iota(jnp.int32, (SSM_HEADS, SSM_HEADS), 1), 1.0, 0.0)
    a_cs_t = lax.dot_general(eye, a_cs, (((1,), (1,)), ((), ())), precision=hi,
                             preferred_element_type=F32)
    a_end = a_cs[q - 1:q, :]
    per_head = jnp.concatenate([dt, jnp.exp(a_cs), jnp.exp(a_end - a_cs)], axis=0)
    top = per_head.astype(BF16)
    rest = (per_head - top.astype(F32)).astype(BF16)
    spread = jnp.dot(jnp.concatenate([top, rest], axis=0), expand_ref[...],
                     preferred_element_type=F32)
    dt_x = spread[0:q] + spread[3 * q:4 * q]
    e_x = spread[q:2 * q] + spread[4 * q:5 * q]
    dte_x = spread[2 * q:3 * q] + spread[5 * q:6 * q]
    dec_x = e_x[q - 1:q, :]

    for g in range(SSM_GROUPS):
        gs = slice(g * GROUP_W, (g + 1) * GROUP_W)
        bg = act_ref[:, D_INNER + g * D_STATE:D_INNER + (g + 1) * D_STATE].astype(BF16)
        cg = act_ref[:, D_INNER + (SSM_GROUPS + g) * D_STATE:
                     D_INNER + (SSM_GROUPS + g + 1) * D_STATE].astype(BF16)
        cb = lax.dot_general(cg, bg, (((1,), (1,)), ((), ())), preferred_element_type=F32)
        xs_g = act_ref[:, gs]
        xd_g = xs_g * dt_x[:, gs]
        st_g = st_ref[g]
        y_off = jnp.dot(cg, st_g.astype(BF16), preferred_element_type=F32) * e_x[:, gs]
        for jj in range(HEADS_PER_GROUP):
            h = g * HEADS_PER_GROUP + jj
            seg = a_cs[:, h:h + 1] - a_cs_t[h:h + 1, :]
            decay = jnp.where(causal, jnp.exp(seg), 0.0)
            hs = slice(jj * SSM_HEAD_DIM, (jj + 1) * SSM_HEAD_DIM)
            yd_ref[:, hs] = jnp.dot((cb * decay).astype(BF16), xd_g[:, hs].astype(BF16),
                                    preferred_element_type=F32)
        st_ref[g] = st_g * dec_x[:, gs] + lax.dot_general(
            bg, (xd_g * dte_x[:, gs]).astype(BF16), (((0,), (0,)), ((), ())),
            preferred_element_type=F32)
        y = yd_ref[...] + y_off + xs_g * dexp_ref[:, gs]
        zg = z_ref[:, gs]
        yg = y * (zg * _sigmoid(zg))
        ms = jnp.mean(yg * yg, axis=-1, keepdims=True)
        y_ref[:, gs] = (yg * lax.rsqrt(ms + EPS) * nw_ref[:, gs]).astype(y_ref.dtype)

    @pl.when(c == pl.num_programs(1) - 1)
    def _():
        ncv_ref[...] = xpad_ref[8 - (D_CONV - 1):8, :]
        for h in range(SSM_HEADS):
            g, jj = divmod(h, HEADS_PER_GROUP)
            stout_ref[h] = st_ref[g, :, jj * SSM_HEAD_DIM:(jj + 1) * SSM_HEAD_DIM].T


def _ssd_mixer(proj, conv_state, ssm_state, conv_w, conv_b, dt_bias, a_log, d_skip, norm_w, b, t):
    q = 128 if t % 128 == 0 else t
    assert q % 8 == 0 and t >= D_CONV - 1 and CONV_DIM == 2 * D_INNER
    nc = t // q
    expand = jnp.repeat(jnp.eye(SSM_HEADS, dtype=BF16), SSM_HEAD_DIM, axis=1)
    dexp = jnp.repeat(d_skip.astype(F32), SSM_HEAD_DIM)[None, :]
    row = lambda k: (lambda bi, c: (bi * nc + c, k))
    const2 = lambda bi, c: (0, 0)
    state_spec = pl.BlockSpec((None, SSM_HEADS, SSM_HEAD_DIM, D_STATE), lambda bi, c: (bi, 0, 0, 0))
    conv_spec = pl.BlockSpec((None, D_CONV - 1, CONV_DIM), lambda bi, c: (bi, 0, 0))
    y, new_conv, new_ssm = pl.pallas_call(
        functools.partial(_ssd_kernel, q=q),
        out_shape=[jax.ShapeDtypeStruct((b * t, D_INNER), BF16),
                   jax.ShapeDtypeStruct((b, D_CONV - 1, CONV_DIM), F32),
                   jax.ShapeDtypeStruct((b, SSM_HEADS, SSM_HEAD_DIM, D_STATE), F32)],
        grid=(b, nc),
        in_specs=[pl.BlockSpec((q, D_INNER), row(SSM_Z_OFF // D_INNER)),
                  pl.BlockSpec((q, D_INNER), row(SSM_XBC_OFF // D_INNER)),
                  pl.BlockSpec((q, D_INNER), row(SSM_XBC_OFF // D_INNER + 1)),
                  pl.BlockSpec((q, LANE), row(SSM_DT_OFF // LANE)),
                  conv_spec,
                  state_spec,
                  pl.BlockSpec((D_CONV, CONV_DIM), const2),
                  pl.BlockSpec((1, CONV_DIM), const2),
                  pl.BlockSpec((1, SSM_HEADS), const2),
                  pl.BlockSpec((1, SSM_HEADS), const2),
                  pl.BlockSpec((1, D_INNER), const2),
                  pl.BlockSpec((1, D_INNER), const2),
                  pl.BlockSpec((SSM_HEADS, D_INNER), const2)],
        out_specs=[pl.BlockSpec((q, D_INNER), row(0)), conv_spec, state_spec],
        scratch_shapes=[pltpu.VMEM((q + 8, CONV_DIM), F32),
                        pltpu.VMEM((q, CONV_DIM), F32),
                        pltpu.VMEM((q, GROUP_W), F32),
                        pltpu.VMEM((SSM_GROUPS, D_STATE, GROUP_W), F32)],
        compiler_params=_params("arbitrary", "arbitrary"),
        name="ssd_mixer",
    )(proj, proj, proj, proj, conv_state.astype(F32), ssm_state.astype(F32), conv_w.astype(F32),
      conv_b.astype(F32)[None, :], dt_bias.astype(F32)[None, :], a_log.astype(F32)[None, :], dexp,
      norm_w.astype(F32)[None, :], expand)
    return y, new_conv, new_ssm.astype(ssm_state.dtype)


def _tile_m(m):
    for tm in (1024, 512, 256, 128, 64, 32, 16, 8):
        if m % tm == 0:
            return tm
    raise ValueError(m)


def _pad_cols(w, n):
    return jnp.pad(w, ((0, 0), (0, n - w.shape[1])))


def _prep_weights(norm_mix, norm_ffn, norm_final, attn_w_in, attn_w_o, ssm_w_in, ssm_w_out,
                  mlp_w_up, mlp_w_down):
    col_scale = jnp.where(jnp.arange(ATTN_COLS) < K_OFF, ATTN_SCALE * LOG2E, 1.0).astype(F32)
    return dict(
        attn_in=_pad_cols(attn_w_in[0] * col_scale, _round_up(ATTN_COLS, LANE)).astype(BF16),
        attn_o=attn_w_o[0].astype(BF16),
        ssm_in=ssm_w_in[0].astype(BF16),
        ssm_out=ssm_w_out[0].astype(BF16),
        up=[mlp_w_up[i].astype(BF16) for i in range(2)],
        down=[mlp_w_down[i].astype(BF16) for i in range(2)],
        g_mix=[norm_mix[i].astype(F32)[None, :] for i in range(2)],
        g_ffn=[norm_ffn[i].astype(F32)[None, :] for i in range(2)],
        g_final=norm_final.astype(F32)[None, :],
    )


def _trunk(x, past_k, past_v, past_ik, conv_st, ssm_st, rel_bias, wts, ssm_conv_w, ssm_conv_b,
           ssm_dt_bias, ssm_a_log, ssm_d, ssm_norm):
    b, t, d = x.shape
    m = b * t
    tm = _tile_m(m)
    x2 = x.reshape(m, d).astype(F32)

    projb, k_new, v_new, ki_new, kiwi = _attn_proj(x2, wts["g_mix"][0], wts["attn_in"], tm=min(tm, 512))
    ao = _dsa_mixer(projb, kiwi, past_k, past_v, past_ik, rel_bias, b, t)
    x2 = _layer_tail(ao, wts["attn_o"], x2, wts["g_ffn"][0], wts["up"][0], wts["down"][0],
                     wts["g_final"], tm=min(tm, 512), tf=1024, final_norm=False)

    proj = _norm_matmul(x2, wts["g_mix"][1], wts["ssm_in"], tm=min(tm, 256), tn=wts["ssm_in"].shape[1])
    y, new_conv, new_ssm = _ssd_mixer(proj, conv_st, ssm_st, ssm_conv_w, ssm_conv_b, ssm_dt_bias,
                                      ssm_a_log, ssm_d, ssm_norm, b, t)
    x2 = _layer_tail(y, wts["ssm_out"], x2, wts["g_ffn"][1], wts["up"][1], wts["down"][1],
                     wts["g_final"], tm=min(tm, 512), tf=1024, final_norm=True)

    dt = x.dtype
    return (x2.reshape(b, t, d).astype(dt),
            k_new.reshape(1, b, t, N_KV_HEADS, HEAD_DIM).astype(dt),
            v_new.reshape(1, b, t, N_KV_HEADS, HEAD_DIM).astype(dt),
            ki_new.reshape(1, b, t, IDX_DIM).astype(dt), new_conv[None].astype(dt), new_ssm[None])


def kernel(x_prompt, x_sample, cache_k, cache_v, cache_idx_k, state_conv, state_ssm, rel_bias, norm_mix, norm_ffn, norm_final, attn_w_in, attn_w_o, ssm_w_in, ssm_conv_w, ssm_conv_b, ssm_dt_bias, ssm_a_log, ssm_d, ssm_norm, ssm_w_out, mlp_w_up, mlp_w_down):
    wts = _prep_weights(norm_mix, norm_ffn, norm_final, attn_w_in, attn_w_o, ssm_w_in, ssm_w_out,
                        mlp_w_up, mlp_w_down)
    bp = x_prompt.shape[0]
    dtp = x_prompt.dtype
    empty_k = jnp.zeros((bp, 0, N_KV_HEADS, HEAD_DIM), dtp)
    empty_ik = jnp.zeros((bp, 0, IDX_DIM), dtp)
    zero_conv = jnp.zeros((bp, D_CONV - 1, CONV_DIM), dtp)
    zero_ssm = jnp.zeros((bp, SSM_HEADS, SSM_HEAD_DIM, D_STATE), dtp)
    args = (rel_bias, wts, ssm_conv_w[0], ssm_conv_b[0], ssm_dt_bias[0], ssm_a_log[0], ssm_d[0],
            ssm_norm[0])
    yp, kp, vp, ikp, cp, sp = _trunk(x_prompt, empty_k, empty_k, empty_ik, zero_conv, zero_ssm, *args)
    ys, ks, vs, iks, cs, ss = _trunk(x_sample, cache_k[0], cache_v[0], cache_idx_k[0],
                                     state_conv[0], state_ssm[0], *args)
    return (yp, ys, kp, vp, ikp, cp, sp, ks, vs, iks, cs, ss)
```

```python
import functools
import math

import jax
import jax.numpy as jnp
from jax import lax
from jax.experimental import pallas as pl
from jax.experimental.pallas import tpu as pltpu

F32 = jnp.float32
BF16 = jnp.bfloat16

D_MODEL = 1024
CHUNK = 64
CHUNK_SHIFT = 6
N_HEADS = 16
HEAD_DIM = 64
N_KV_HEADS = 4
GQ = N_HEADS // N_KV_HEADS
IDX_HEADS = 8
IDX_DIM = 64
TOPK_MAX = 256
IDX_SCALE = (IDX_HEADS * IDX_DIM) ** -0.5
ATTN_SCALE = HEAD_DIM ** -0.5
NUM_BUCKETS = 32
MAX_DISTANCE = 128
D_INNER = 2 * D_MODEL
SSM_HEAD_DIM = 64
SSM_HEADS = D_INNER // SSM_HEAD_DIM
SSM_GROUPS = 8
HEADS_PER_GROUP = SSM_HEADS // SSM_GROUPS
GROUP_W = HEADS_PER_GROUP * SSM_HEAD_DIM
D_STATE = 128
D_CONV = 4
CONV_DIM = D_INNER + 2 * SSM_GROUPS * D_STATE
EPS = 1e-6

Q_OFF = 0
K_OFF = N_HEADS * HEAD_DIM
V_OFF = K_OFF + N_KV_HEADS * HEAD_DIM
QI_OFF = V_OFF + N_KV_HEADS * HEAD_DIM
KI_OFF = QI_OFF + IDX_HEADS * IDX_DIM
WI_OFF = KI_OFF + IDX_DIM
ATTN_COLS = WI_OFF + IDX_HEADS

SSM_Z_OFF = 0
SSM_XBC_OFF = D_INNER
SSM_DT_OFF = D_INNER + CONV_DIM
SSM_COLS = SSM_DT_OFF + SSM_HEADS

LANE = 128
KEY_WINDOW_SHIFT = 9
KEY_WINDOW = 1 << KEY_WINDOW_SHIFT
QUERY_ROWS = 256
MAX_TILE_ROWS = 512
CONV_COLS = 512
PROJ_ROWS = 512
SSM_PROJ_ROWS = 256
TAIL_ROWS = 512
MLP_CHUNK = 1024
LOG2E = 1.4426950408889634
VMEM_LIMIT = 56 * 1024 * 1024

NEG_BIG = -1e30
KEY_NEG_INF = -2139095041
MIN_NORMAL_KEY = 1 << 23
INT_MIN = -2147483648


def _round_up(n, m):
    return (n + m - 1) // m * m


def _rms(x, g):
    ms = jnp.mean(x * x, axis=-1, keepdims=True)
    return x * lax.rsqrt(ms + EPS) * g


def _sigmoid(x):
    return 1.0 / (1.0 + jnp.exp(-x))


def _params(*sem):
    return pltpu.CompilerParams(dimension_semantics=sem, vmem_limit_bytes=VMEM_LIMIT)


def _norm_matmul_kernel(x_ref, g_ref, w_ref, o_ref, xn_ref):
    @pl.when(pl.program_id(1) == 0)
    def _():
        xn_ref[...] = _rms(x_ref[...], g_ref[...]).astype(BF16)

    o_ref[...] = jnp.dot(xn_ref[...], w_ref[...], preferred_element_type=F32)


def _norm_matmul(x, g, w, *, tm, tn):
    m, k = x.shape
    n = w.shape[1]
    return pl.pallas_call(
        _norm_matmul_kernel,
        out_shape=jax.ShapeDtypeStruct((m, n), F32),
        grid=(m // tm, n // tn),
        in_specs=[pl.BlockSpec((tm, k), lambda i, j: (i, 0)),
                  pl.BlockSpec((1, k), lambda i, j: (0, 0)),
                  pl.BlockSpec((k, tn), lambda i, j: (0, j))],
        out_specs=pl.BlockSpec((tm, tn), lambda i, j: (i, j)),
        scratch_shapes=[pltpu.VMEM((tm, k), BF16)],
        compiler_params=_params("parallel", "arbitrary"),
        name="norm_matmul",
    )(x, g, w)


def _attn_proj_kernel(x_ref, g_ref, w_ref, pb_ref, k_ref, v_ref, ki_ref, kiwi_ref):
    xn = _rms(x_ref[...], g_ref[...]).astype(BF16)
    acc = jnp.dot(xn, w_ref[...], preferred_element_type=F32)
    pb_ref[...] = acc.astype(BF16)
    for g in range(N_KV_HEADS):
        k_ref[:, g, :] = acc[:, K_OFF + g * HEAD_DIM:K_OFF + (g + 1) * HEAD_DIM]
        v_ref[:, g, :] = acc[:, V_OFF + g * HEAD_DIM:V_OFF + (g + 1) * HEAD_DIM]
    ki_ref[...] = acc[:, KI_OFF:WI_OFF]
    kiwi_ref[...] = acc[:, KI_OFF:KI_OFF + LANE]


def _attn_proj(x, g, w, *, tm):
    m, k = x.shape
    n = w.shape[1]
    row = lambda i: (i, 0)
    kv_spec = pl.BlockSpec((tm, N_KV_HEADS, HEAD_DIM), lambda i: (i, 0, 0))
    return pl.pallas_call(
        _attn_proj_kernel,
        out_shape=[jax.ShapeDtypeStruct((m, n), BF16),
                   jax.ShapeDtypeStruct((m, N_KV_HEADS, HEAD_DIM), F32),
                   jax.ShapeDtypeStruct((m, N_KV_HEADS, HEAD_DIM), F32),
                   jax.ShapeDtypeStruct((m, IDX_DIM), F32),
                   jax.ShapeDtypeStruct((m, LANE), F32)],
        grid=(m // tm,),
        in_specs=[pl.BlockSpec((tm, k), row),
                  pl.BlockSpec((1, k), lambda i: (0, 0)),
                  pl.BlockSpec((k, n), lambda i: (0, 0))],
        out_specs=[pl.BlockSpec((tm, n), row), kv_spec, kv_spec,
                   pl.BlockSpec((tm, IDX_DIM), row), pl.BlockSpec((tm, LANE), row)],
        compiler_params=_params("parallel"),
        name="attn_proj",
    )(x, g, w)


def _tail_kernel(a_ref, wo_ref, r_ref, g_ref, wu_ref, wd_ref, gf_ref, o_ref, *, tf, final_norm):
    x1 = r_ref[...] + jnp.dot(a_ref[...], wo_ref[...], preferred_element_type=F32)
    xn = _rms(x1, g_ref[...]).astype(BF16)
    acc = x1
    for f0 in range(0, wu_ref.shape[1], tf):
        h = jnp.maximum(jnp.dot(xn, wu_ref[:, f0:f0 + tf], preferred_element_type=F32), 0.0)
        acc = acc + jnp.dot((h * h).astype(BF16), wd_ref[f0:f0 + tf, :], preferred_element_type=F32)
    if final_norm:
        acc = _rms(acc, gf_ref[...])
    o_ref[...] = acc


def _layer_tail(a, wo, resid, g, wu, wd, gf, *, tm, tf, final_norm):
    m, ka = a.shape
    d = wo.shape[1]
    ff = wu.shape[1]
    row = lambda i: (i, 0)
    whole = lambda shape: pl.BlockSpec(shape, lambda i: (0, 0), pipeline_mode=pl.Buffered(1))
    return pl.pallas_call(
        functools.partial(_tail_kernel, tf=tf, final_norm=final_norm),
        out_shape=jax.ShapeDtypeStruct((m, d), F32),
        grid=(m // tm,),
        in_specs=[pl.BlockSpec((tm, ka), row), whole((ka, d)), pl.BlockSpec((tm, d), row),
                  whole((1, d)), whole((d, ff)), whole((ff, d)), whole((1, d))],
        out_specs=pl.BlockSpec((tm, d), row),
        compiler_params=_params("parallel"),
        name="layer_tail",
    )(a, wo, resid, g, wu, wd, gf)


def _attn_kernel(*refs, tq, t, past, front_pad, topk):
    if past:
        (q_ref, qi_ref, kiwi_ref, kn_ref, vn_ref, kin_ref, pk_ref, pv_ref, pki_ref, nb_ref, o_ref,
         k_ref, v_ref, ki_ref, score_ref, qs_ref, qis_ref, wcol_ref, madd_ref, s_ref, m_ref, acc_ref) = refs
    else:
        (q_ref, qi_ref, kiwi_ref, kn_ref, vn_ref, kin_ref, nb_ref, o_ref,
         k_ref, v_ref, ki_ref, score_ref, qs_ref, qis_ref, wcol_ref, madd_ref, s_ref, m_ref, acc_ref) = refs
    w = KEY_WINDOW
    rows = GQ * tq
    rb = s_ref.shape[1]
    mrows = madd_ref.shape[0]
    keys_on_rows = score_ref.shape[1] == w
    kax = 0 if keys_on_rows else 1
    qshape = (1, tq) if keys_on_rows else (tq, 1)
    i = pl.program_id(1)
    q0 = past + i * tq
    lv = q0 + tq
    nw = jnp.right_shift(lv + (w - 1), KEY_WINDOW_SHIFT)
    qchunk = jnp.right_shift(q0 + lax.broadcasted_iota(jnp.int32, qshape, 1 - kax), CHUNK_SHIFT)
    kidx = lax.broadcasted_iota(jnp.int32, (w, 1) if keys_on_rows else (1, w), kax)
    nt = (((1,), (1,)), ((), ()))

    def win_row(j):
        return pl.multiple_of(lv + front_pad - w * (j + 1), tq)

    @pl.when(i == 0)
    def _():
        ones_col = jnp.where(lax.broadcasted_iota(jnp.int32, (1, LANE - HEAD_DIM), 1) == 0, 1.0, 0.0)

        def put(r0, n, kc, vc, kic):
            ones = jnp.broadcast_to(ones_col, (n, LANE - HEAD_DIM)).astype(BF16)
            for g in range(N_KV_HEADS):
                gs = slice(g * HEAD_DIM, (g + 1) * HEAD_DIM)
                k_ref[g, r0:r0 + n, :] = kc[:, gs]
                v_ref[g, r0:r0 + n, :] = jnp.concatenate([vc[:, gs], ones], axis=1)
            ki_ref[r0:r0 + n, :] = kic

        kvw = N_KV_HEADS * HEAD_DIM
        if front_pad:
            put(0, front_pad, jnp.zeros((front_pad, kvw), BF16), jnp.zeros((front_pad, kvw), BF16),
                jnp.zeros((front_pad, IDX_DIM), BF16))
        step = min(t, w)
        for c0 in range(0, past, w):
            put(front_pad + c0, w, pk_ref[c0:c0 + w, :].astype(BF16), pv_ref[c0:c0 + w, :].astype(BF16),
                pki_ref[c0:c0 + w, :].astype(BF16))
        for c0 in range(0, t, step):
            put(front_pad + past + c0, step, kn_ref[c0:c0 + step, :], vn_ref[c0:c0 + step, :],
                kin_ref[c0:c0 + step, 0:IDX_DIM])

    for h in range(N_HEADS):
        g, jj = divmod(h, GQ)
        qs_ref[g, jj * tq:(jj + 1) * tq, :] = q_ref[:, h * HEAD_DIM:(h + 1) * HEAD_DIM]

    if keys_on_rows:
        wi_t = kiwi_ref[...].T[IDX_DIM:IDX_DIM + IDX_HEADS, :]
        for hp in range(IDX_HEADS // 2):
            for u in range(2):
                h = 2 * hp + u
                qis_ref[hp, u * tq:(u + 1) * tq, :] = qi_ref[:, h * IDX_DIM:(h + 1) * IDX_DIM]
    else:
        wi = kiwi_ref[:, IDX_DIM:IDX_DIM + IDX_HEADS]
        for h in range(IDX_HEADS):
            qis_ref[0, h * tq:(h + 1) * tq, :] = qi_ref[:, h * IDX_DIM:(h + 1) * IDX_DIM]
            wcol_ref[h * tq:(h + 1) * tq, :] = wi[:, h:h + 1]

    def visible_scores(sc, kpos):
        vis = (kpos >= 0) & (jnp.right_shift(kpos, CHUNK_SHIFT) <= qchunk)
        return jnp.where(vis, sc, -jnp.inf)

    def score_body(j, carry):
        if keys_on_rows:
            for k0 in range(0, w, w // 2):
                kib = ki_ref[pl.ds(win_row(j) + k0, w // 2), :]
                sc = jnp.zeros((w // 2, tq), F32)
                for hp in range(IDX_HEADS // 2):
                    s2 = jnp.maximum(lax.dot_general(kib, qis_ref[hp], nt, preferred_element_type=F32), 0.0)
                    sc = sc + s2[:, 0:tq] * wi_t[2 * hp:2 * hp + 1, :]
                    sc = sc + s2[:, tq:2 * tq] * wi_t[2 * hp + 1:2 * hp + 2, :]
                kpos = (lv - w * (j + 1)) + kidx[k0:k0 + w // 2]
                score_ref[j, k0:k0 + w // 2, :] = visible_scores(sc * IDX_SCALE, kpos)
        else:
            kib = ki_ref[pl.ds(win_row(j), w), :]
            s8 = jnp.maximum(lax.dot_general(qis_ref[0], kib, nt, preferred_element_type=F32), 0.0)
            s8 = s8 * wcol_ref[...]
            sc = s8[0:tq]
            for h in range(1, IDX_HEADS):
                sc = sc + s8[h * tq:(h + 1) * tq]
            score_ref[j] = visible_scores(sc * IDX_SCALE, (lv - w * (j + 1)) + kidx)
        return carry

    lax.fori_loop(0, nw, score_body, 0)

    def key_to_f32(key):
        return pltpu.bitcast(key ^ (jnp.right_shift(key, 31) & 0x7FFFFFFF), F32)

    if keys_on_rows:
        part = (2 * 8, tq)

        def fold(c):
            return jnp.sum(c.reshape(w // part[0], part[0], tq), axis=0)
    else:
        part = (tq, LANE)

        def fold(c):
            return (c[:, 0:LANE] + c[:, LANE:2 * LANE]) + (c[:, 2 * LANE:3 * LANE] + c[:, 3 * LANE:4 * LANE])

    def total(acc):
        return jnp.sum(acc, axis=kax, keepdims=True)

    def count_ge(cand):
        cand_f = key_to_f32(jnp.where((cand > 0) & (cand < MIN_NORMAL_KEY), MIN_NORMAL_KEY, cand))

        def body(j, acc):
            return acc + fold(jnp.where(score_ref[j] >= cand_f, 1.0, 0.0))
        return total(lax.fori_loop(0, nw, body, jnp.zeros(part, F32)))

    def step(cand, carry):
        prefix, n_lo, n_hi = carry
        c = count_ge(cand)
        take = c >= float(topk)
        return jnp.where(take, cand, prefix), jnp.where(take, c, n_lo), jnp.where(take, n_hi, c)

    def search_body(bit, carry):
        return step(carry[0] + lax.shift_left(jnp.int32(1), 30 - bit), carry)

    n_all = (nw * w).astype(F32)
    start = (jnp.full(qshape, INT_MIN, jnp.int32), jnp.full(qshape, 1.0, F32) * n_all, jnp.zeros(qshape, F32))
    sign = step(jnp.zeros(qshape, jnp.int32), start)
    tstar, n_ge, n_gt = lax.fori_loop(0, 31, search_body, sign)

    need = float(topk) - n_gt
    n_eq = n_ge - n_gt
    bad = (n_eq > need) & (tstar > KEY_NEG_INF)
    n_bad = jnp.sum(jnp.where(bad, 1.0, 0.0))
    tstar_f = key_to_f32(tstar)

    @pl.when(n_bad > 0.0)
    def _():
        r_i = lax.broadcasted_iota(jnp.int32, (w, w), 0)
        c_i = lax.broadcasted_iota(jnp.int32, (w, w), 1)
        earlier = jnp.where(c_i < r_i if keys_on_rows else r_i < c_i, 1.0, 0.0).astype(BF16)

        def tie_body(t, seen):
            j = nw - 1 - t
            kt = score_ref[j]
            eq = kt == tstar_f
            eqf = jnp.where(eq, 1.0, 0.0)
            if keys_on_rows:
                inwin = jnp.dot(earlier, eqf.astype(BF16), preferred_element_type=F32)
            else:
                inwin = jnp.dot(eqf.astype(BF16), earlier, preferred_element_type=F32)
            score_ref[j] = jnp.where(eq & bad & (seen + inwin >= need), -jnp.inf, kt)
            return seen + total(eqf)

        lax.fori_loop(0, nw, tie_body, jnp.zeros(qshape, F32))

    teff = key_to_f32(jnp.maximum(tstar, KEY_NEG_INF + 1))

    def attend(j, near):
        row = win_row(j)
        tile = jnp.where(score_ref[j] >= teff, 0.0, NEG_BIG)
        if keys_on_rows:
            tile = tile.T
        for r in range(mrows // tq):
            madd_ref[r * tq:(r + 1) * tq, :] = tile
        tiles = [(g, r0) for g in range(N_KV_HEADS) for r0 in range(0, rows, rb)]

        for t, (g, r0) in enumerate(tiles):
            kb = k_ref[g, pl.ds(row, w), :]
            s = lax.dot_general(qs_ref[g, r0:r0 + rb, :], kb, nt, preferred_element_type=F32)
            s = s + madd_ref[r0 % mrows:r0 % mrows + rb, :]
            if near:
                s = s + nb_ref[g, r0:r0 + rb, :]
            s_ref[t] = s
        for t, (g, r0) in enumerate(tiles):
            rs = slice(r0, r0 + rb)
            s = s_ref[t]
            m_new = jnp.broadcast_to(jnp.max(s, axis=1, keepdims=True), (rb, LANE))
            if not near:
                m_old = m_ref[g, rs, :]
                m_new = jnp.maximum(m_old, m_new)
            p = jnp.exp2(s - jnp.concatenate([m_new] * (w // LANE), axis=1))
            pv = jnp.dot(p.astype(BF16), v_ref[g, pl.ds(row, w), :], preferred_element_type=F32)
            acc_ref[g, rs, :] = pv if near else jnp.exp2(m_old - m_new) * acc_ref[g, rs, :] + pv
            m_ref[g, rs, :] = m_new

    attend(0, True)

    def far_body(j, carry):
        attend(j, False)
        return carry

    lax.fori_loop(1, nw, far_body, 0)

    for h in range(N_HEADS):
        g, jj = divmod(h, GQ)
        a = acc_ref[g, jj * tq:(jj + 1) * tq, :]
        o_ref[:, h * HEAD_DIM:(h + 1) * HEAD_DIM] = (
            a[:, 0:HEAD_DIM] / a[:, HEAD_DIM:HEAD_DIM + 1]).astype(o_ref.dtype)


def _rel_bucket(rel):
    half = NUM_BUCKETS // 2
    max_exact = half // 2
    base = jnp.where(rel > 0, half, 0)
    n = jnp.abs(rel)
    nf = jnp.maximum(n, 1).astype(jnp.float32)
    large = max_exact + jnp.floor(jnp.log(nf / max_exact) / math.log(MAX_DISTANCE / max_exact)
                                  * (half - max_exact)).astype(jnp.int32)
    large = jnp.minimum(large, half - 1)
    return base + jnp.where(n < max_exact, n, large)


def _near_bias(rel_bias, tq):
    r = jnp.arange(tq, dtype=jnp.int32)[:, None]
    c = jnp.arange(KEY_WINDOW, dtype=jnp.int32)[None, :]
    rel = (tq - KEY_WINDOW + c) - r
    tab = rel_bias.astype(F32)

    def pick(bucket):
        bucket = jnp.bitwise_and(bucket, NUM_BUCKETS - 1)
        d = bucket[..., None].astype(F32) - jnp.arange(NUM_BUCKETS, dtype=F32)
        return jnp.maximum(1.0 - jnp.abs(d), 0.0)

    far = _rel_bucket(jnp.full((1,), -(1 << 20), jnp.int32))
    far_row = jnp.dot(pick(far), tab, precision=lax.Precision.HIGHEST)
    nb = jnp.einsum('qcb,bh->hqc', pick(_rel_bucket(rel)), (tab - far_row) * LOG2E,
                    precision=lax.Precision.HIGHEST)
    return nb.reshape(N_KV_HEADS, GQ * tq, KEY_WINDOW)


def _dsa_mixer(projb, kiwi, past_k, past_v, past_ik, rel_bias, b, t):
    past = past_k.shape[1]
    n_keys = past + t
    topk = min(TOPK_MAX, n_keys // 4)
    tq = QUERY_ROWS if t % QUERY_ROWS == 0 else t
    assert KEY_WINDOW % tq == 0 and past % KEY_WINDOW == 0 and tq % 16 == 0 and n_keys >= KEY_WINDOW
    assert t % min(t, KEY_WINDOW) == 0
    front_pad = KEY_WINDOW - tq
    lp = front_pad + n_keys
    nq = t // tq
    nw_max = (n_keys + KEY_WINDOW - 1) // KEY_WINDOW
    rows = GQ * tq
    rb = min(rows, MAX_TILE_ROWS)
    keys_on_rows = tq % LANE == 0
    assert keys_on_rows or IDX_HEADS * tq <= LANE
    kvw = N_KV_HEADS * HEAD_DIM
    qw = N_HEADS * HEAD_DIM
    qiw = IDX_HEADS * IDX_DIM

    qrow = lambda bi, i: (bi * nq + i, 0)
    in_specs = [pl.BlockSpec((tq, qw), qrow),
                pl.BlockSpec((tq, qiw), lambda bi, i: (bi * nq + i, QI_OFF // qiw)),
                pl.BlockSpec((tq, LANE), qrow),
                pl.BlockSpec((t, kvw), lambda bi, i: (bi, K_OFF // kvw), pipeline_mode=pl.Buffered(1)),
                pl.BlockSpec((t, kvw), lambda bi, i: (bi, V_OFF // kvw), pipeline_mode=pl.Buffered(1)),
                pl.BlockSpec((t, LANE), lambda bi, i: (bi, KI_OFF // LANE), pipeline_mode=pl.Buffered(1))]
    args = [projb, projb, kiwi, projb, projb, projb]
    if past:
        in_specs += [pl.BlockSpec((None, past, kvw), lambda bi, i: (bi, 0, 0)),
                     pl.BlockSpec((None, past, kvw), lambda bi, i: (bi, 0, 0)),
                     pl.BlockSpec((None, past, IDX_DIM), lambda bi, i: (bi, 0, 0))]
        args += [past_k.reshape(b, past, kvw), past_v.reshape(b, past, kvw), past_ik]
    in_specs.append(pl.BlockSpec((N_KV_HEADS, rows, KEY_WINDOW), lambda bi, i: (0, 0, 0),
                                 pipeline_mode=pl.Buffered(1)))
    args.append(_near_bias(rel_bias, tq))

    kern = functools.partial(_attn_kernel, tq=tq, t=t, past=past, front_pad=front_pad, topk=topk)
    return pl.pallas_call(
        kern,
        out_shape=jax.ShapeDtypeStruct((b * t, qw), BF16),
        grid=(b, nq),
        in_specs=in_specs,
        out_specs=pl.BlockSpec((tq, qw), qrow),
        scratch_shapes=[pltpu.VMEM((N_KV_HEADS, lp, HEAD_DIM), BF16),
                        pltpu.VMEM((N_KV_HEADS, lp, LANE), BF16),
                        pltpu.VMEM((lp, IDX_DIM), BF16),
                        pltpu.VMEM((nw_max, KEY_WINDOW, tq) if keys_on_rows
                                   else (nw_max, tq, KEY_WINDOW), F32),
                        pltpu.VMEM((N_KV_HEADS, rows, HEAD_DIM), BF16),
                        pltpu.VMEM((IDX_HEADS // 2, 2 * tq, IDX_DIM) if keys_on_rows
                                   else (1, IDX_HEADS * tq, IDX_DIM), BF16),
                        pltpu.VMEM((8, 1) if keys_on_rows else (IDX_HEADS * tq, 1), F32),
                        pltpu.VMEM((max(rb, tq), KEY_WINDOW), F32),
                        pltpu.VMEM((N_KV_HEADS * rows // rb, rb, KEY_WINDOW), F32),
                        pltpu.VMEM((N_KV_HEADS, rows, LANE), F32),
                        pltpu.VMEM((N_KV_HEADS, rows, LANE), F32)],
        compiler_params=_params("arbitrary", "arbitrary"),
        name="dsa_attention",
    )(*args)


def _ssd_kernel(z_ref, xlo_ref, xhi_ref, dt_ref, cst_ref, st0_ref, cw_ref, cb_ref, dtb_ref, alog_ref,
                dexp_ref, nw_ref, expand_ref, y_ref, ncv_ref, stout_ref,
                xpad_ref, act_ref, yd_ref, st_ref, *, q):
    c = pl.program_id(1)
    half = CONV_DIM // 2

    @pl.when(c == 0)
    def _():
        xpad_ref[8 - (D_CONV - 1):8, :] = cst_ref[...]
        for h in range(SSM_HEADS):
            g, jj = divmod(h, HEADS_PER_GROUP)
            st_ref[g, :, jj * SSM_HEAD_DIM:(jj + 1) * SSM_HEAD_DIM] = st0_ref[h].T

    cc = CONV_COLS
    for c0 in range(0, CONV_DIM, cc):
        cs = slice(c0, c0 + cc)
        src = xlo_ref[:, c0:c0 + cc] if c0 < half else xhi_ref[:, c0 - half:c0 - half + cc]
        xpad_ref[8:8 + q, cs] = src
        conv = cb_ref[:, cs] + xpad_ref[5:5 + q, cs] * cw_ref[0:1, cs]
        conv = conv + xpad_ref[6:6 + q, cs] * cw_ref[1:2, cs]
        conv = conv + xpad_ref[7:7 + q, cs] * cw_ref[2:3, cs]
        conv = conv + xpad_ref[8:8 + q, cs] * cw_ref[3:4, cs]
        act_ref[:, cs] = conv * _sigmoid(conv)
        xpad_ref[0:8, cs] = xpad_ref[q:q + 8, cs]

    hi = lax.Precision.HIGHEST
    x_dt = dt_ref[:, 0:SSM_HEADS] + dtb_ref[...]
    dt = jnp.maximum(x_dt, 0.0) + jnp.log1p(jnp.exp(-jnp.abs(x_dt)))
    a = dt * (-jnp.exp(alog_ref[...]))
    rr = lax.broadcasted_iota(jnp.int32, (q, q), 0)
    cl = lax.broadcasted_iota(jnp.int32, (q, q), 1)
    causal = rr >= cl
    tri = jnp.where(causal, 1.0, 0.0)
    a_cs = jnp.dot(tri, a, precision=hi, preferred_element_type=F32)
    eye = jnp.where(lax.broadcasted_iota(jnp.int32, (SSM_HEADS, SSM_HEADS), 0)
                    == lax.broadcasted_iota(jnp.int32, (SSM_HEADS, SSM_HEADS), 1), 1.0, 0.0)
    a_cs_t = lax.dot_general(eye, a_cs, (((1,), (1,)), ((), ())), precision=hi,
                             preferred_element_type=F32)
    a_end = a_cs[q - 1:q, :]
    per_head = jnp.concatenate([dt, jnp.exp(a_cs), jnp.exp(a_end - a_cs)], axis=0)
    top = per_head.astype(BF16)
    rest = (per_head - top.astype(F32)).astype(BF16)
    spread = jnp.dot(jnp.concatenate([top, rest], axis=0), expand_ref[...],
                     preferred_element_type=F32)
    dt_x = spread[0:q] + spread[3 * q:4 * q]
    e_x = spread[q:2 * q] + spread[4 * q:5 * q]
    dte_x = spread[2 * q:3 * q] + spread[5 * q:6 * q]
    dec_x = e_x[q - 1:q, :]

    for g in range(SSM_GROUPS):
        gs = slice(g * GROUP_W, (g + 1) * GROUP_W)
        bg = act_ref[:, D_INNER + g * D_STATE:D_INNER + (g + 1) * D_STATE].astype(BF16)
        cg = act_ref[:, D_INNER + (SSM_GROUPS + g) * D_STATE:
                     D_INNER + (SSM_GROUPS + g + 1) * D_STATE].astype(BF16)
        cb = lax.dot_general(cg, bg, (((1,), (1,)), ((), ())), preferred_element_type=F32)
        xs_g = act_ref[:, gs]
        xd_g = xs_g * dt_x[:, gs]
        st_g = st_ref[g]
        y_off = jnp.dot(cg, st_g.astype(BF16), preferred_element_type=F32) * e_x[:, gs]
        for jj in range(HEADS_PER_GROUP):
            h = g * HEADS_PER_GROUP + jj
            seg = a_cs[:, h:h + 1] - a_cs_t[h:h + 1, :]
            decay = jnp.where(causal, jnp.exp(seg), 0.0)
            hs = slice(jj * SSM_HEAD_DIM, (jj + 1) * SSM_HEAD_DIM)
            yd_ref[:, hs] = jnp.dot((cb * decay).astype(BF16), xd_g[:, hs].astype(BF16),
                                    preferred_element_type=F32)
        st_ref[g] = st_g * dec_x[:, gs] + lax.dot_general(
            bg, (xd_g * dte_x[:, gs]).astype(BF16), (((0,), (0,)), ((), ())),
            preferred_element_type=F32)
        y = yd_ref[...] + y_off + xs_g * dexp_ref[:, gs]
        zg = z_ref[:, gs]
        yg = y * (zg * _sigmoid(zg))
        ms = jnp.mean(yg * yg, axis=-1, keepdims=True)
        y_ref[:, gs] = (yg * lax.rsqrt(ms + EPS) * nw_ref[:, gs]).astype(y_ref.dtype)

    @pl.when(c == pl.num_programs(1) - 1)
    def _():
        ncv_ref[...] = xpad_ref[8 - (D_CONV - 1):8, :]
        for h in range(SSM_HEADS):
            g, jj = divmod(h, HEADS_PER_GROUP)
            stout_ref[h] = st_ref[g, :, jj * SSM_HEAD_DIM:(jj + 1) * SSM_HEAD_DIM].T


def _ssd_mixer(proj, conv_state, ssm_state, conv_w, conv_b, dt_bias, a_log, d_skip, norm_w, b, t):
    q = 128 if t % 128 == 0 else t
    assert q % 8 == 0 and t >= D_CONV - 1 and CONV_DIM == 2 * D_INNER
    nc = t // q
    expand = jnp.repeat(jnp.eye(SSM_HEADS, dtype=BF16), SSM_HEAD_DIM, axis=1)
    dexp = jnp.repeat(d_skip.astype(F32), SSM_HEAD_DIM)[None, :]
    row = lambda k: (lambda bi, c: (bi * nc + c, k))
    const2 = lambda bi, c: (0, 0)
    state_spec = pl.BlockSpec((None, SSM_HEADS, SSM_HEAD_DIM, D_STATE), lambda bi, c: (bi, 0, 0, 0))
    conv_spec = pl.BlockSpec((None, D_CONV - 1, CONV_DIM), lambda bi, c: (bi, 0, 0))
    y, new_conv, new_ssm = pl.pallas_call(
        functools.partial(_ssd_kernel, q=q),
        out_shape=[jax.ShapeDtypeStruct((b * t, D_INNER), BF16),
                   jax.ShapeDtypeStruct((b, D_CONV - 1, CONV_DIM), F32),
                   jax.ShapeDtypeStruct((b, SSM_HEADS, SSM_HEAD_DIM, D_STATE), F32)],
        grid=(b, nc),
        in_specs=[pl.BlockSpec((q, D_INNER), row(SSM_Z_OFF // D_INNER)),
                  pl.BlockSpec((q, D_INNER), row(SSM_XBC_OFF // D_INNER)),
                  pl.BlockSpec((q, D_INNER), row(SSM_XBC_OFF // D_INNER + 1)),
                  pl.BlockSpec((q, LANE), row(SSM_DT_OFF // LANE)),
                  conv_spec,
                  state_spec,
                  pl.BlockSpec((D_CONV, CONV_DIM), const2),
                  pl.BlockSpec((1, CONV_DIM), const2),
                  pl.BlockSpec((1, SSM_HEADS), const2),
                  pl.BlockSpec((1, SSM_HEADS), const2),
                  pl.BlockSpec((1, D_INNER), const2),
                  pl.BlockSpec((1, D_INNER), const2),
                  pl.BlockSpec((SSM_HEADS, D_INNER), const2)],
        out_specs=[pl.BlockSpec((q, D_INNER), row(0)), conv_spec, state_spec],
        scratch_shapes=[pltpu.VMEM((q + 8, CONV_DIM), F32),
                        pltpu.VMEM((q, CONV_DIM), F32),
                        pltpu.VMEM((q, GROUP_W), F32),
                        pltpu.VMEM((SSM_GROUPS, D_STATE, GROUP_W), F32)],
        compiler_params=_params("arbitrary", "arbitrary"),
        name="ssd_mixer",
    )(proj, proj, proj, proj, conv_state.astype(F32), ssm_state.astype(F32), conv_w.astype(F32),
      conv_b.astype(F32)[None, :], dt_bias.astype(F32)[None, :], a_log.astype(F32)[None, :], dexp,
      norm_w.astype(F32)[None, :], expand)
    return y, new_conv, new_ssm.astype(ssm_state.dtype)


def _row_tile(m, cap):
    tm = cap
    while m % tm:
        tm //= 2
    assert tm >= 8, (m, cap)
    return tm


def _pad_cols(w, n):
    return jnp.pad(w, ((0, 0), (0, n - w.shape[1])))


def _prep_weights(norm_mix, norm_ffn, norm_final, attn_w_in, attn_w_o, ssm_w_in, ssm_w_out,
                  mlp_w_up, mlp_w_down):
    col_scale = jnp.where(jnp.arange(ATTN_COLS) < K_OFF, ATTN_SCALE * LOG2E, 1.0).astype(F32)
    return dict(
        attn_in=_pad_cols(attn_w_in[0] * col_scale, _round_up(ATTN_COLS, LANE)).astype(BF16),
        attn_o=attn_w_o[0].astype(BF16),
        ssm_in=ssm_w_in[0].astype(BF16),
        ssm_out=ssm_w_out[0].astype(BF16),
        up=[mlp_w_up[i].astype(BF16) for i in range(2)],
        down=[mlp_w_down[i].astype(BF16) for i in range(2)],
        g_mix=[norm_mix[i].astype(F32)[None, :] for i in range(2)],
        g_ffn=[norm_ffn[i].astype(F32)[None, :] for i in range(2)],
        g_final=norm_final.astype(F32)[None, :],
    )


def _trunk(x, past_k, past_v, past_ik, conv_st, ssm_st, rel_bias, wts, ssm_conv_w, ssm_conv_b,
           ssm_dt_bias, ssm_a_log, ssm_d, ssm_norm):
    b, t, d = x.shape
    m = b * t
    x2 = x.reshape(m, d).astype(F32)
    tail_rows = _row_tile(m, TAIL_ROWS)

    projb, k_new, v_new, ki_new, kiwi = _attn_proj(x2, wts["g_mix"][0], wts["attn_in"],
                                                   tm=_row_tile(m, PROJ_ROWS))
    ao = _dsa_mixer(projb, kiwi, past_k, past_v, past_ik, rel_bias, b, t)
    x2 = _layer_tail(ao, wts["attn_o"], x2, wts["g_ffn"][0], wts["up"][0], wts["down"][0],
                     wts["g_final"], tm=tail_rows, tf=MLP_CHUNK, final_norm=False)

    proj = _norm_matmul(x2, wts["g_mix"][1], wts["ssm_in"], tm=_row_tile(m, SSM_PROJ_ROWS),
                        tn=wts["ssm_in"].shape[1])
    y, new_conv, new_ssm = _ssd_mixer(proj, conv_st, ssm_st, ssm_conv_w, ssm_conv_b, ssm_dt_bias,
                                      ssm_a_log, ssm_d, ssm_norm, b, t)
    x2 = _layer_tail(y, wts["ssm_out"], x2, wts["g_ffn"][1], wts["up"][1], wts["down"][1],
                     wts["g_final"], tm=tail_rows, tf=MLP_CHUNK, final_norm=True)

    dt = x.dtype
    return (x2.reshape(b, t, d).astype(dt),
            k_new.reshape(1, b, t, N_KV_HEADS, HEAD_DIM).astype(dt),
            v_new.reshape(1, b, t, N_KV_HEADS, HEAD_DIM).astype(dt),
            ki_new.reshape(1, b, t, IDX_DIM).astype(dt), new_conv[None].astype(dt), new_ssm[None])


def kernel(x_prompt, x_sample, cache_k, cache_v, cache_idx_k, state_conv, state_ssm, rel_bias, norm_mix, norm_ffn, norm_final, attn_w_in, attn_w_o, ssm_w_in, ssm_conv_w, ssm_conv_b, ssm_dt_bias, ssm_a_log, ssm_d, ssm_norm, ssm_w_out, mlp_w_up, mlp_w_down):
    wts = _prep_weights(norm_mix, norm_ffn, norm_final, attn_w_in, attn_w_o, ssm_w_in, ssm_w_out,
                        mlp_w_up, mlp_w_down)
    bp = x_prompt.shape[0]
    dtp = x_prompt.dtype
    empty_k = jnp.zeros((bp, 0, N_KV_HEADS, HEAD_DIM), dtp)
    empty_ik = jnp.zeros((bp, 0, IDX_DIM), dtp)
    zero_conv = jnp.zeros((bp, D_CONV - 1, CONV_DIM), dtp)
    zero_ssm = jnp.zeros((bp, SSM_HEADS, SSM_HEAD_DIM, D_STATE), dtp)
    args = (rel_bias, wts, ssm_conv_w[0], ssm_conv_b[0], ssm_dt_bias[0], ssm_a_log[0], ssm_d[0],
            ssm_norm[0])
    yp, kp, vp, ikp, cp, sp = _trunk(x_prompt, empty_k, empty_k, empty_ik, zero_conv, zero_ssm, *args)
    ys, ks, vs, iks, cs, ss = _trunk(x_sample, cache_k[0], cache_v[0], cache_idx_k[0],
                                     state_conv[0], state_ssm[0], *args)
    return (yp, ys, kp, vp, ikp, cp, sp, ks, vs, iks, cs, ss)
```

```python
import functools
import math

import jax
import jax.numpy as jnp
from jax import lax
from jax.experimental import pallas as pl
from jax.experimental.pallas import tpu as pltpu

F32 = jnp.float32
BF16 = jnp.bfloat16

D_MODEL = 1024
CHUNK = 64
CHUNK_SHIFT = 6
N_HEADS = 16
HEAD_DIM = 64
N_KV_HEADS = 4
GQ = N_HEADS // N_KV_HEADS
IDX_HEADS = 8
IDX_DIM = 64
TOPK_MAX = 256
IDX_SCALE = (IDX_HEADS * IDX_DIM) ** -0.5
ATTN_SCALE = HEAD_DIM ** -0.5
NUM_BUCKETS = 32
MAX_DISTANCE = 128
D_INNER = 2 * D_MODEL
SSM_HEAD_DIM = 64
SSM_HEADS = D_INNER // SSM_HEAD_DIM
SSM_GROUPS = 8
HEADS_PER_GROUP = SSM_HEADS // SSM_GROUPS
GROUP_W = HEADS_PER_GROUP * SSM_HEAD_DIM
D_STATE = 128
D_CONV = 4
CONV_DIM = D_INNER + 2 * SSM_GROUPS * D_STATE
EPS = 1e-6

Q_OFF = 0
K_OFF = N_HEADS * HEAD_DIM
V_OFF = K_OFF + N_KV_HEADS * HEAD_DIM
QI_OFF = V_OFF + N_KV_HEADS * HEAD_DIM
KI_OFF = QI_OFF + IDX_HEADS * IDX_DIM
WI_OFF = KI_OFF + IDX_DIM
ATTN_COLS = WI_OFF + IDX_HEADS

SSM_Z_OFF = 0
SSM_XBC_OFF = D_INNER
SSM_DT_OFF = D_INNER + CONV_DIM
SSM_COLS = SSM_DT_OFF + SSM_HEADS

LANE = 128
KEY_WINDOW_SHIFT = 9
KEY_WINDOW = 1 << KEY_WINDOW_SHIFT
QUERY_ROWS = 256
MAX_TILE_ROWS = 1024
CONV_COLS = 512
PROJ_ROWS = 512
SSM_PROJ_ROWS = 256
TAIL_ROWS = 512
MLP_CHUNK = 1024
LOG2E = 1.4426950408889634
VMEM_LIMIT = 56 * 1024 * 1024

NEG_BIG = -1e30
KEY_NEG_INF = -2139095041
MIN_NORMAL_KEY = 1 << 23
INT_MIN = -2147483648


def _round_up(n, m):
    return (n + m - 1) // m * m


def _rms(x, g):
    ms = jnp.mean(x * x, axis=-1, keepdims=True)
    return x * lax.rsqrt(ms + EPS) * g


def _silu(x):
    h = 0.5 * x
    return h + h * jnp.tanh(h)


def _params(*sem):
    return pltpu.CompilerParams(dimension_semantics=sem, vmem_limit_bytes=VMEM_LIMIT)


def _norm_matmul_kernel(x_ref, g_ref, w_ref, o_ref, xn_ref):
    @pl.when(pl.program_id(1) == 0)
    def _():
        xn_ref[...] = _rms(x_ref[...], g_ref[...]).astype(BF16)

    o_ref[...] = jnp.dot(xn_ref[...], w_ref[...], preferred_element_type=F32)


def _norm_matmul(x, g, w, *, tm, tn):
    m, k = x.shape
    n = w.shape[1]
    return pl.pallas_call(
        _norm_matmul_kernel,
        out_shape=jax.ShapeDtypeStruct((m, n), F32),
        grid=(m // tm, n // tn),
        in_specs=[pl.BlockSpec((tm, k), lambda i, j: (i, 0)),
                  pl.BlockSpec((1, k), lambda i, j: (0, 0)),
                  pl.BlockSpec((k, tn), lambda i, j: (0, j))],
        out_specs=pl.BlockSpec((tm, tn), lambda i, j: (i, j)),
        scratch_shapes=[pltpu.VMEM((tm, k), BF16)],
        compiler_params=_params("parallel", "arbitrary"),
        name="norm_matmul",
    )(x, g, w)


def _attn_proj_kernel(x_ref, g_ref, w_ref, pb_ref, k_ref, v_ref, ki_ref, kiwi_ref):
    xn = _rms(x_ref[...], g_ref[...]).astype(BF16)
    acc = jnp.dot(xn, w_ref[...], preferred_element_type=F32)
    pb_ref[...] = acc.astype(BF16)
    for g in range(N_KV_HEADS):
        k_ref[:, g, :] = acc[:, K_OFF + g * HEAD_DIM:K_OFF + (g + 1) * HEAD_DIM]
        v_ref[:, g, :] = acc[:, V_OFF + g * HEAD_DIM:V_OFF + (g + 1) * HEAD_DIM]
    ki_ref[...] = acc[:, KI_OFF:WI_OFF]
    kiwi_ref[...] = acc[:, KI_OFF:KI_OFF + LANE]


def _attn_proj(x, g, w, *, tm):
    m, k = x.shape
    n = w.shape[1]
    row = lambda i: (i, 0)
    kv_spec = pl.BlockSpec((tm, N_KV_HEADS, HEAD_DIM), lambda i: (i, 0, 0))
    return pl.pallas_call(
        _attn_proj_kernel,
        out_shape=[jax.ShapeDtypeStruct((m, n), BF16),
                   jax.ShapeDtypeStruct((m, N_KV_HEADS, HEAD_DIM), F32),
                   jax.ShapeDtypeStruct((m, N_KV_HEADS, HEAD_DIM), F32),
                   jax.ShapeDtypeStruct((m, IDX_DIM), F32),
                   jax.ShapeDtypeStruct((m, LANE), F32)],
        grid=(m // tm,),
        in_specs=[pl.BlockSpec((tm, k), row),
                  pl.BlockSpec((1, k), lambda i: (0, 0)),
                  pl.BlockSpec((k, n), lambda i: (0, 0))],
        out_specs=[pl.BlockSpec((tm, n), row), kv_spec, kv_spec,
                   pl.BlockSpec((tm, IDX_DIM), row), pl.BlockSpec((tm, LANE), row)],
        compiler_params=_params("parallel"),
        name="attn_proj",
    )(x, g, w)


def _tail_kernel(a_ref, wo_ref, r_ref, g_ref, wu_ref, wd_ref, gf_ref, o_ref, *, tf, final_norm):
    x1 = r_ref[...] + jnp.dot(a_ref[...], wo_ref[...], preferred_element_type=F32)
    xn = _rms(x1, g_ref[...]).astype(BF16)
    acc = x1
    for f0 in range(0, wu_ref.shape[1], tf):
        h = jnp.maximum(jnp.dot(xn, wu_ref[:, f0:f0 + tf], preferred_element_type=F32), 0.0)
        acc = acc + jnp.dot((h * h).astype(BF16), wd_ref[f0:f0 + tf, :], preferred_element_type=F32)
    if final_norm:
        acc = _rms(acc, gf_ref[...])
    o_ref[...] = acc


def _layer_tail(a, wo, resid, g, wu, wd, gf, *, tm, tf, final_norm):
    m, ka = a.shape
    d = wo.shape[1]
    ff = wu.shape[1]
    row = lambda i: (i, 0)
    whole = lambda shape: pl.BlockSpec(shape, lambda i: (0, 0), pipeline_mode=pl.Buffered(1))
    return pl.pallas_call(
        functools.partial(_tail_kernel, tf=tf, final_norm=final_norm),
        out_shape=jax.ShapeDtypeStruct((m, d), F32),
        grid=(m // tm,),
        in_specs=[pl.BlockSpec((tm, ka), row), whole((ka, d)), pl.BlockSpec((tm, d), row),
                  whole((1, d)), whole((d, ff)), whole((ff, d)), whole((1, d))],
        out_specs=pl.BlockSpec((tm, d), row),
        compiler_params=_params("parallel"),
        name="layer_tail",
    )(a, wo, resid, g, wu, wd, gf)


def _attn_kernel(*refs, tq, t, past, front_pad, topk):
    if past:
        (q_ref, qi_ref, kiwi_ref, kn_ref, vn_ref, kin_ref, pk_ref, pv_ref, pki_ref, nb_ref, o_ref,
         k_ref, v_ref, ki_ref, score_ref, qs_ref, qis_ref, wcol_ref, madd_ref, s_ref, m_ref, acc_ref) = refs
    else:
        (q_ref, qi_ref, kiwi_ref, kn_ref, vn_ref, kin_ref, nb_ref, o_ref,
         k_ref, v_ref, ki_ref, score_ref, qs_ref, qis_ref, wcol_ref, madd_ref, s_ref, m_ref, acc_ref) = refs
    w = KEY_WINDOW
    rows = GQ * tq
    rb = s_ref.shape[1]
    mrows = madd_ref.shape[0]
    keys_on_rows = score_ref.shape[1] == w
    kax = 0 if keys_on_rows else 1
    qshape = (1, tq) if keys_on_rows else (tq, 1)
    i = pl.program_id(1)
    q0 = past + i * tq
    lv = q0 + tq
    nw = jnp.right_shift(lv + (w - 1), KEY_WINDOW_SHIFT)
    qchunk = jnp.right_shift(q0 + lax.broadcasted_iota(jnp.int32, qshape, 1 - kax), CHUNK_SHIFT)
    kidx = lax.broadcasted_iota(jnp.int32, (w, 1) if keys_on_rows else (1, w), kax)
    nt = (((1,), (1,)), ((), ()))

    def win_row(j):
        return pl.multiple_of(lv + front_pad - w * (j + 1), tq)

    @pl.when(i == 0)
    def _():
        ones_col = jnp.where(lax.broadcasted_iota(jnp.int32, (1, LANE - HEAD_DIM), 1) == 0, 1.0, 0.0)

        def put(r0, n, kc, vc, kic):
            ones = jnp.broadcast_to(ones_col, (n, LANE - HEAD_DIM)).astype(BF16)
            for g in range(N_KV_HEADS):
                gs = slice(g * HEAD_DIM, (g + 1) * HEAD_DIM)
                k_ref[g, r0:r0 + n, :] = kc[:, gs]
                v_ref[g, r0:r0 + n, :] = jnp.concatenate([vc[:, gs], ones], axis=1)
            ki_ref[r0:r0 + n, :] = kic

        kvw = N_KV_HEADS * HEAD_DIM
        if front_pad:
            put(0, front_pad, jnp.zeros((front_pad, kvw), BF16), jnp.zeros((front_pad, kvw), BF16),
                jnp.zeros((front_pad, IDX_DIM), BF16))
        step = min(t, w)
        for c0 in range(0, past, w):
            put(front_pad + c0, w, pk_ref[c0:c0 + w, :].astype(BF16), pv_ref[c0:c0 + w, :].astype(BF16),
                pki_ref[c0:c0 + w, :].astype(BF16))
        for c0 in range(0, t, step):
            put(front_pad + past + c0, step, kn_ref[c0:c0 + step, :], vn_ref[c0:c0 + step, :],
                kin_ref[c0:c0 + step, 0:IDX_DIM])

    for h in range(N_HEADS):
        g, jj = divmod(h, GQ)
        qs_ref[g, jj * tq:(jj + 1) * tq, :] = q_ref[:, h * HEAD_DIM:(h + 1) * HEAD_DIM]

    if keys_on_rows:
        wi_t = kiwi_ref[...].T[IDX_DIM:IDX_DIM + IDX_HEADS, :]
        for hp in range(IDX_HEADS // 2):
            for u in range(2):
                h = 2 * hp + u
                qis_ref[hp, u * tq:(u + 1) * tq, :] = qi_ref[:, h * IDX_DIM:(h + 1) * IDX_DIM]
    else:
        wi = kiwi_ref[:, IDX_DIM:IDX_DIM + IDX_HEADS]
        for h in range(IDX_HEADS):
            qis_ref[0, h * tq:(h + 1) * tq, :] = qi_ref[:, h * IDX_DIM:(h + 1) * IDX_DIM]
            wcol_ref[h * tq:(h + 1) * tq, :] = wi[:, h:h + 1]

    def visible_scores(sc, kpos):
        vis = (kpos >= 0) & (jnp.right_shift(kpos, CHUNK_SHIFT) <= qchunk)
        return jnp.where(vis, sc, -jnp.inf)

    def score_body(j, carry):
        if keys_on_rows:
            for k0 in range(0, w, w // 2):
                kib = ki_ref[pl.ds(win_row(j) + k0, w // 2), :]
                sc = jnp.zeros((w // 2, tq), F32)
                for hp in range(IDX_HEADS // 2):
                    s2 = jnp.maximum(lax.dot_general(kib, qis_ref[hp], nt, preferred_element_type=F32), 0.0)
                    sc = sc + s2[:, 0:tq] * wi_t[2 * hp:2 * hp + 1, :]
                    sc = sc + s2[:, tq:2 * tq] * wi_t[2 * hp + 1:2 * hp + 2, :]
                kpos = (lv - w * (j + 1)) + kidx[k0:k0 + w // 2]
                score_ref[j, k0:k0 + w // 2, :] = visible_scores(sc * IDX_SCALE, kpos)
        else:
            kib = ki_ref[pl.ds(win_row(j), w), :]
            s8 = jnp.maximum(lax.dot_general(qis_ref[0], kib, nt, preferred_element_type=F32), 0.0)
            s8 = s8 * wcol_ref[...]
            sc = s8[0:tq]
            for h in range(1, IDX_HEADS):
                sc = sc + s8[h * tq:(h + 1) * tq]
            score_ref[j] = visible_scores(sc * IDX_SCALE, (lv - w * (j + 1)) + kidx)
        return carry

    lax.fori_loop(0, nw, score_body, 0)

    def key_to_f32(key):
        return pltpu.bitcast(key ^ (jnp.right_shift(key, 31) & 0x7FFFFFFF), F32)

    if keys_on_rows:
        part = (2 * 8, tq)

        def fold(c):
            return jnp.sum(c.reshape(w // part[0], part[0], tq), axis=0)
    else:
        part = (tq, LANE)

        def fold(c):
            return (c[:, 0:LANE] + c[:, LANE:2 * LANE]) + (c[:, 2 * LANE:3 * LANE] + c[:, 3 * LANE:4 * LANE])

    def total(acc):
        return jnp.sum(acc, axis=kax, keepdims=True)

    def count_ge(cand):
        cand_f = key_to_f32(jnp.where((cand > 0) & (cand < MIN_NORMAL_KEY), MIN_NORMAL_KEY, cand))

        def body(j, acc):
            return acc + fold(jnp.where(score_ref[j] >= cand_f, 1.0, 0.0))
        return total(lax.fori_loop(0, nw, body, jnp.zeros(part, F32)))

    def step(cand, carry):
        prefix, n_lo, n_hi = carry
        c = count_ge(cand)
        take = c >= float(topk)
        return jnp.where(take, cand, prefix), jnp.where(take, c, n_lo), jnp.where(take, n_hi, c)

    def search_body(bit, carry):
        return step(carry[0] + lax.shift_left(jnp.int32(1), 30 - bit), carry)

    n_all = (nw * w).astype(F32)
    start = (jnp.full(qshape, INT_MIN, jnp.int32), jnp.full(qshape, 1.0, F32) * n_all, jnp.zeros(qshape, F32))
    sign = step(jnp.zeros(qshape, jnp.int32), start)
    tstar, n_ge, n_gt = lax.fori_loop(0, 31, search_body, sign)

    need = float(topk) - n_gt
    n_eq = n_ge - n_gt
    bad = (n_eq > need) & (tstar > KEY_NEG_INF)
    n_bad = jnp.sum(jnp.where(bad, 1.0, 0.0))
    tstar_f = key_to_f32(tstar)

    @pl.when(n_bad > 0.0)
    def _():
        r_i = lax.broadcasted_iota(jnp.int32, (w, w), 0)
        c_i = lax.broadcasted_iota(jnp.int32, (w, w), 1)
        earlier = jnp.where(c_i < r_i if keys_on_rows else r_i < c_i, 1.0, 0.0).astype(BF16)

        def tie_body(t, seen):
            j = nw - 1 - t
            kt = score_ref[j]
            eq = kt == tstar_f
            eqf = jnp.where(eq, 1.0, 0.0)
            if keys_on_rows:
                inwin = jnp.dot(earlier, eqf.astype(BF16), preferred_element_type=F32)
            else:
                inwin = jnp.dot(eqf.astype(BF16), earlier, preferred_element_type=F32)
            score_ref[j] = jnp.where(eq & bad & (seen + inwin >= need), -jnp.inf, kt)
            return seen + total(eqf)

        lax.fori_loop(0, nw, tie_body, jnp.zeros(qshape, F32))

    teff = key_to_f32(jnp.maximum(tstar, KEY_NEG_INF + 1))

    def attend(j, near):
        row = win_row(j)
        tile = jnp.where(score_ref[j] >= teff, 0.0, NEG_BIG)
        if keys_on_rows:
            tile = tile.T
        for r in range(mrows // tq):
            madd_ref[r * tq:(r + 1) * tq, :] = tile
        tiles = [(g, r0) for g in range(N_KV_HEADS) for r0 in range(0, rows, rb)]

        for t, (g, r0) in enumerate(tiles):
            kb = k_ref[g, pl.ds(row, w), :]
            s = lax.dot_general(qs_ref[g, r0:r0 + rb, :], kb, nt, preferred_element_type=F32)
            s = s + madd_ref[r0 % mrows:r0 % mrows + rb, :]
            if near:
                s = s + nb_ref[g, r0:r0 + rb, :]
            s_ref[t] = s
        for t, (g, r0) in enumerate(tiles):
            rs = slice(r0, r0 + rb)
            s = s_ref[t]
            m_new = jnp.broadcast_to(jnp.max(s, axis=1, keepdims=True), (rb, LANE))
            if not near:
                m_old = m_ref[g, rs, :]
                m_new = jnp.maximum(m_old, m_new)
            p = jnp.exp2(s - jnp.concatenate([m_new] * (w // LANE), axis=1))
            pv = jnp.dot(p.astype(BF16), v_ref[g, pl.ds(row, w), :], preferred_element_type=F32)
            acc_ref[g, rs, :] = pv if near else jnp.exp2(m_old - m_new) * acc_ref[g, rs, :] + pv
            m_ref[g, rs, :] = m_new

    attend(0, True)

    def far_body(j, carry):
        attend(j, False)
        return carry

    lax.fori_loop(1, nw, far_body, 0)

    for h in range(N_HEADS):
        g, jj = divmod(h, GQ)
        a = acc_ref[g, jj * tq:(jj + 1) * tq, :]
        o_ref[:, h * HEAD_DIM:(h + 1) * HEAD_DIM] = (
            a[:, 0:HEAD_DIM] / a[:, HEAD_DIM:HEAD_DIM + 1]).astype(o_ref.dtype)


def _rel_bucket(rel):
    half = NUM_BUCKETS // 2
    max_exact = half // 2
    base = jnp.where(rel > 0, half, 0)
    n = jnp.abs(rel)
    nf = jnp.maximum(n, 1).astype(jnp.float32)
    large = max_exact + jnp.floor(jnp.log(nf / max_exact) / math.log(MAX_DISTANCE / max_exact)
                                  * (half - max_exact)).astype(jnp.int32)
    large = jnp.minimum(large, half - 1)
    return base + jnp.where(n < max_exact, n, large)


def _near_bias(rel_bias, tq):
    r = jnp.arange(tq, dtype=jnp.int32)[:, None]
    c = jnp.arange(KEY_WINDOW, dtype=jnp.int32)[None, :]
    rel = (tq - KEY_WINDOW + c) - r
    tab = rel_bias.astype(F32)

    def pick(bucket):
        bucket = jnp.bitwise_and(bucket, NUM_BUCKETS - 1)
        d = bucket[..., None].astype(F32) - jnp.arange(NUM_BUCKETS, dtype=F32)
        return jnp.maximum(1.0 - jnp.abs(d), 0.0)

    far = _rel_bucket(jnp.full((1,), -(1 << 20), jnp.int32))
    far_row = jnp.dot(pick(far), tab, precision=lax.Precision.HIGHEST)
    nb = jnp.einsum('qcb,bh->hqc', pick(_rel_bucket(rel)), (tab - far_row) * LOG2E,
                    precision=lax.Precision.HIGHEST)
    return nb.reshape(N_KV_HEADS, GQ * tq, KEY_WINDOW)


def _dsa_mixer(projb, kiwi, past_k, past_v, past_ik, rel_bias, b, t):
    past = past_k.shape[1]
    n_keys = past + t
    topk = min(TOPK_MAX, n_keys // 4)
    tq = QUERY_ROWS if t % QUERY_ROWS == 0 else t
    assert KEY_WINDOW % tq == 0 and past % KEY_WINDOW == 0 and tq % 16 == 0 and n_keys >= KEY_WINDOW
    assert t % min(t, KEY_WINDOW) == 0
    front_pad = KEY_WINDOW - tq
    lp = front_pad + n_keys
    nq = t // tq
    nw_max = (n_keys + KEY_WINDOW - 1) // KEY_WINDOW
    rows = GQ * tq
    rb = min(rows, MAX_TILE_ROWS)
    keys_on_rows = tq % LANE == 0
    assert keys_on_rows or IDX_HEADS * tq <= LANE
    kvw = N_KV_HEADS * HEAD_DIM
    qw = N_HEADS * HEAD_DIM
    qiw = IDX_HEADS * IDX_DIM

    qrow = lambda bi, i: (bi * nq + i, 0)
    stream_buf = pl.Buffered(1 if nq > 1 else 2)
    in_specs = [pl.BlockSpec((tq, qw), qrow),
                pl.BlockSpec((tq, qiw), lambda bi, i: (bi * nq + i, QI_OFF // qiw)),
                pl.BlockSpec((tq, LANE), qrow),
                pl.BlockSpec((t, kvw), lambda bi, i: (bi, K_OFF // kvw), pipeline_mode=stream_buf),
                pl.BlockSpec((t, kvw), lambda bi, i: (bi, V_OFF // kvw), pipeline_mode=stream_buf),
                pl.BlockSpec((t, LANE), lambda bi, i: (bi, KI_OFF // LANE), pipeline_mode=stream_buf)]
    args = [projb, projb, kiwi, projb, projb, projb]
    if past:
        in_specs += [pl.BlockSpec((None, past, kvw), lambda bi, i: (bi, 0, 0)),
                     pl.BlockSpec((None, past, kvw), lambda bi, i: (bi, 0, 0)),
                     pl.BlockSpec((None, past, IDX_DIM), lambda bi, i: (bi, 0, 0))]
        args += [past_k.reshape(b, past, kvw), past_v.reshape(b, past, kvw), past_ik]
    in_specs.append(pl.BlockSpec((N_KV_HEADS, rows, KEY_WINDOW), lambda bi, i: (0, 0, 0),
                                 pipeline_mode=pl.Buffered(1)))
    args.append(_near_bias(rel_bias, tq))

    kern = functools.partial(_attn_kernel, tq=tq, t=t, past=past, front_pad=front_pad, topk=topk)
    return pl.pallas_call(
        kern,
        out_shape=jax.ShapeDtypeStruct((b * t, qw), BF16),
        grid=(b, nq),
        in_specs=in_specs,
        out_specs=pl.BlockSpec((tq, qw), qrow),
        scratch_shapes=[pltpu.VMEM((N_KV_HEADS, lp, HEAD_DIM), BF16),
                        pltpu.VMEM((N_KV_HEADS, lp, LANE), BF16),
                        pltpu.VMEM((lp, IDX_DIM), BF16),
                        pltpu.VMEM((nw_max, KEY_WINDOW, tq) if keys_on_rows
                                   else (nw_max, tq, KEY_WINDOW), F32),
                        pltpu.VMEM((N_KV_HEADS, rows, HEAD_DIM), BF16),
                        pltpu.VMEM((IDX_HEADS // 2, 2 * tq, IDX_DIM) if keys_on_rows
                                   else (1, IDX_HEADS * tq, IDX_DIM), BF16),
                        pltpu.VMEM((8, 1) if keys_on_rows else (IDX_HEADS * tq, 1), F32),
                        pltpu.VMEM((max(rb, tq), KEY_WINDOW), F32),
                        pltpu.VMEM((N_KV_HEADS * rows // rb, rb, KEY_WINDOW), F32),
                        pltpu.VMEM((N_KV_HEADS, rows, LANE), F32),
                        pltpu.VMEM((N_KV_HEADS, rows, LANE), F32)],
        compiler_params=_params("arbitrary", "arbitrary"),
        name="dsa_attention",
    )(*args)


def _ssd_kernel(z_ref, xlo_ref, xhi_ref, dt_ref, cst_ref, st0_ref, cw_ref, cb_ref, dtb_ref, alog_ref,
                dexp_ref, nw_ref, expand_ref, y_ref, ncv_ref, stout_ref,
                xpad_ref, act_ref, yd_ref, st_ref, *, q):
    c = pl.program_id(1)
    half = CONV_DIM // 2

    @pl.when(c == 0)
    def _():
        xpad_ref[8 - (D_CONV - 1):8, :] = cst_ref[...]
        for h in range(SSM_HEADS):
            g, jj = divmod(h, HEADS_PER_GROUP)
            st_ref[g, :, jj * SSM_HEAD_DIM:(jj + 1) * SSM_HEAD_DIM] = st0_ref[h].T

    cc = CONV_COLS
    for c0 in range(0, CONV_DIM, cc):
        cs = slice(c0, c0 + cc)
        src = xlo_ref[:, c0:c0 + cc] if c0 < half else xhi_ref[:, c0 - half:c0 - half + cc]
        xpad_ref[8:8 + q, cs] = src
        conv = cb_ref[:, cs] + xpad_ref[5:5 + q, cs] * cw_ref[0:1, cs]
        conv = conv + xpad_ref[6:6 + q, cs] * cw_ref[1:2, cs]
        conv = conv + xpad_ref[7:7 + q, cs] * cw_ref[2:3, cs]
        conv = conv + xpad_ref[8:8 + q, cs] * cw_ref[3:4, cs]
        act_ref[:, cs] = _silu(conv)
        xpad_ref[0:8, cs] = xpad_ref[q:q + 8, cs]

    hi = lax.Precision.HIGHEST
    x_dt = dt_ref[:, 0:SSM_HEADS] + dtb_ref[...]
    dt = jnp.maximum(x_dt, 0.0) + jnp.log1p(jnp.exp(-jnp.abs(x_dt)))
    a = dt * (-jnp.exp(alog_ref[...]))
    rr = lax.broadcasted_iota(jnp.int32, (q, q), 0)
    cl = lax.broadcasted_iota(jnp.int32, (q, q), 1)
    causal = rr >= cl
    tri = jnp.where(causal, 1.0, 0.0)
    a_cs = jnp.dot(tri, a, precision=hi, preferred_element_type=F32)
    eye = jnp.where(lax.broadcasted_iota(jnp.int32, (SSM_HEADS, SSM_HEADS), 0)
                    == lax.broadcasted_iota(jnp.int32, (SSM_HEADS, SSM_HEADS), 1), 1.0, 0.0)
    a_cs_t = lax.dot_general(eye, a_cs, (((1,), (1,)), ((), ())), precision=hi,
                             preferred_element_type=F32)
    a_end = a_cs[q - 1:q, :]
    per_head = jnp.concatenate([dt, jnp.exp(a_cs), jnp.exp(a_end - a_cs)], axis=0)
    top = per_head.astype(BF16)
    rest = (per_head - top.astype(F32)).astype(BF16)
    spread = jnp.dot(jnp.concatenate([top, rest], axis=0), expand_ref[...],
                     preferred_element_type=F32)
    dt_x = spread[0:q] + spread[3 * q:4 * q]
    e_x = spread[q:2 * q] + spread[4 * q:5 * q]
    dte_x = spread[2 * q:3 * q] + spread[5 * q:6 * q]
    dec_x = e_x[q - 1:q, :]

    for g in range(SSM_GROUPS):
        gs = slice(g * GROUP_W, (g + 1) * GROUP_W)
        bg = act_ref[:, D_INNER + g * D_STATE:D_INNER + (g + 1) * D_STATE].astype(BF16)
        cg = act_ref[:, D_INNER + (SSM_GROUPS + g) * D_STATE:
                     D_INNER + (SSM_GROUPS + g + 1) * D_STATE].astype(BF16)
        cb = lax.dot_general(cg, bg, (((1,), (1,)), ((), ())), preferred_element_type=F32)
        xs_g = act_ref[:, gs]
        xd_g = xs_g * dt_x[:, gs]
        st_g = st_ref[g]
        y_off = jnp.dot(cg, st_g.astype(BF16), preferred_element_type=F32) * e_x[:, gs]
        for jj in range(HEADS_PER_GROUP):
            h = g * HEADS_PER_GROUP + jj
            seg = a_cs[:, h:h + 1] - a_cs_t[h:h + 1, :]
            decay = jnp.where(causal, jnp.exp(seg), 0.0)
            hs = slice(jj * SSM_HEAD_DIM, (jj + 1) * SSM_HEAD_DIM)
            yd_ref[:, hs] = jnp.dot((cb * decay).astype(BF16), xd_g[:, hs].astype(BF16),
                                    preferred_element_type=F32)
        st_ref[g] = st_g * dec_x[:, gs] + lax.dot_general(
            bg, (xd_g * dte_x[:, gs]).astype(BF16), (((0,), (0,)), ((), ())),
            preferred_element_type=F32)
        y = yd_ref[...] + y_off + xs_g * dexp_ref[:, gs]
        zg = z_ref[:, gs]
        yg = y * _silu(zg)
        ms = jnp.mean(yg * yg, axis=-1, keepdims=True)
        y_ref[:, gs] = (yg * lax.rsqrt(ms + EPS) * nw_ref[:, gs]).astype(y_ref.dtype)

    @pl.when(c == pl.num_programs(1) - 1)
    def _():
        ncv_ref[...] = xpad_ref[8 - (D_CONV - 1):8, :]
        for h in range(SSM_HEADS):
            g, jj = divmod(h, HEADS_PER_GROUP)
            stout_ref[h] = st_ref[g, :, jj * SSM_HEAD_DIM:(jj + 1) * SSM_HEAD_DIM].T


def _ssd_mixer(proj, conv_state, ssm_state, conv_w, conv_b, dt_bias, a_log, d_skip, norm_w, b, t):
    q = 128 if t % 128 == 0 else t
    assert q % 8 == 0 and t >= D_CONV - 1 and CONV_DIM == 2 * D_INNER
    nc = t // q
    expand = jnp.repeat(jnp.eye(SSM_HEADS, dtype=BF16), SSM_HEAD_DIM, axis=1)
    dexp = jnp.repeat(d_skip.astype(F32), SSM_HEAD_DIM)[None, :]
    row = lambda k: (lambda bi, c: (bi * nc + c, k))
    const2 = lambda bi, c: (0, 0)
    state_spec = pl.BlockSpec((None, SSM_HEADS, SSM_HEAD_DIM, D_STATE), lambda bi, c: (bi, 0, 0, 0))
    conv_spec = pl.BlockSpec((None, D_CONV - 1, CONV_DIM), lambda bi, c: (bi, 0, 0))
    y, new_conv, new_ssm = pl.pallas_call(
        functools.partial(_ssd_kernel, q=q),
        out_shape=[jax.ShapeDtypeStruct((b * t, D_INNER), BF16),
                   jax.ShapeDtypeStruct((b, D_CONV - 1, CONV_DIM), F32),
                   jax.ShapeDtypeStruct((b, SSM_HEADS, SSM_HEAD_DIM, D_STATE), F32)],
        grid=(b, nc),
        in_specs=[pl.BlockSpec((q, D_INNER), row(SSM_Z_OFF // D_INNER)),
                  pl.BlockSpec((q, D_INNER), row(SSM_XBC_OFF // D_INNER)),
                  pl.BlockSpec((q, D_INNER), row(SSM_XBC_OFF // D_INNER + 1)),
                  pl.BlockSpec((q, LANE), row(SSM_DT_OFF // LANE)),
                  conv_spec,
                  state_spec,
                  pl.BlockSpec((D_CONV, CONV_DIM), const2),
                  pl.BlockSpec((1, CONV_DIM), const2),
                  pl.BlockSpec((1, SSM_HEADS), const2),
                  pl.BlockSpec((1, SSM_HEADS), const2),
                  pl.BlockSpec((1, D_INNER), const2),
                  pl.BlockSpec((1, D_INNER), const2),
                  pl.BlockSpec((SSM_HEADS, D_INNER), const2)],
        out_specs=[pl.BlockSpec((q, D_INNER), row(0)), conv_spec, state_spec],
        scratch_shapes=[pltpu.VMEM((q + 8, CONV_DIM), F32),
                        pltpu.VMEM((q, CONV_DIM), F32),
                        pltpu.VMEM((q, GROUP_W), F32),
                        pltpu.VMEM((SSM_GROUPS, D_STATE, GROUP_W), F32)],
        compiler_params=_params("arbitrary", "arbitrary"),
        name="ssd_mixer",
    )(proj, proj, proj, proj, conv_state.astype(F32), ssm_state.astype(F32), conv_w.astype(F32),
      conv_b.astype(F32)[None, :], dt_bias.astype(F32)[None, :], a_log.astype(F32)[None, :], dexp,
      norm_w.astype(F32)[None, :], expand)
    return y, new_conv, new_ssm.astype(ssm_state.dtype)


def _row_tile(m, cap):
    tm = cap
    while m % tm:
        tm //= 2
    assert tm >= 8, (m, cap)
    return tm


def _pad_cols(w, n):
    return jnp.pad(w, ((0, 0), (0, n - w.shape[1])))


def _prep_weights(norm_mix, norm_ffn, norm_final, attn_w_in, attn_w_o, ssm_w_in, ssm_w_out,
                  mlp_w_up, mlp_w_down):
    col_scale = jnp.where(jnp.arange(ATTN_COLS) < K_OFF, ATTN_SCALE * LOG2E, 1.0).astype(F32)
    return dict(
        attn_in=_pad_cols(attn_w_in[0] * col_scale, _round_up(ATTN_COLS, LANE)).astype(BF16),
        attn_o=attn_w_o[0].astype(BF16),
        ssm_in=ssm_w_in[0].astype(BF16),
        ssm_out=ssm_w_out[0].astype(BF16),
        up=[mlp_w_up[i].astype(BF16) for i in range(2)],
        down=[mlp_w_down[i].astype(BF16) for i in range(2)],
        g_mix=[norm_mix[i].astype(F32)[None, :] for i in range(2)],
        g_ffn=[norm_ffn[i].astype(F32)[None, :] for i in range(2)],
        g_final=norm_final.astype(F32)[None, :],
    )


def _trunk(x, past_k, past_v, past_ik, conv_st, ssm_st, rel_bias, wts, ssm_conv_w, ssm_conv_b,
           ssm_dt_bias, ssm_a_log, ssm_d, ssm_norm):
    b, t, d = x.shape
    m = b * t
    x2 = x.reshape(m, d).astype(F32)
    tail_rows = _row_tile(m, TAIL_ROWS)

    projb, k_new, v_new, ki_new, kiwi = _attn_proj(x2, wts["g_mix"][0], wts["attn_in"],
                                                   tm=_row_tile(m, PROJ_ROWS))
    ao = _dsa_mixer(projb, kiwi, past_k, past_v, past_ik, rel_bias, b, t)
    x2 = _layer_tail(ao, wts["attn_o"], x2, wts["g_ffn"][0], wts["up"][0], wts["down"][0],
                     wts["g_final"], tm=tail_rows, tf=MLP_CHUNK, final_norm=False)

    proj = _norm_matmul(x2, wts["g_mix"][1], wts["ssm_in"], tm=_row_tile(m, SSM_PROJ_ROWS),
                        tn=wts["ssm_in"].shape[1])
    y, new_conv, new_ssm = _ssd_mixer(proj, conv_st, ssm_st, ssm_conv_w, ssm_conv_b, ssm_dt_bias,
                                      ssm_a_log, ssm_d, ssm_norm, b, t)
    x2 = _layer_tail(y, wts["ssm_out"], x2, wts["g_ffn"][1], wts["up"][1], wts["down"][1],
                     wts["g_final"], tm=tail_rows, tf=MLP_CHUNK, final_norm=True)

    dt = x.dtype
    return (x2.reshape(b, t, d).astype(dt),
            k_new.reshape(1, b, t, N_KV_HEADS, HEAD_DIM).astype(dt),
            v_new.reshape(1, b, t, N_KV_HEADS, HEAD_DIM).astype(dt),
            ki_new.reshape(1, b, t, IDX_DIM).astype(dt), new_conv[None].astype(dt), new_ssm[None])


def kernel(x_prompt, x_sample, cache_k, cache_v, cache_idx_k, state_conv, state_ssm, rel_bias, norm_mix, norm_ffn, norm_final, attn_w_in, attn_w_o, ssm_w_in, ssm_conv_w, ssm_conv_b, ssm_dt_bias, ssm_a_log, ssm_d, ssm_norm, ssm_w_out, mlp_w_up, mlp_w_down):
    wts = _prep_weights(norm_mix, norm_ffn, norm_final, attn_w_in, attn_w_o, ssm_w_in, ssm_w_out,
                        mlp_w_up, mlp_w_down)
    bp = x_prompt.shape[0]
    dtp = x_prompt.dtype
    empty_k = jnp.zeros((bp, 0, N_KV_HEADS, HEAD_DIM), dtp)
    empty_ik = jnp.zeros((bp, 0, IDX_DIM), dtp)
    zero_conv = jnp.zeros((bp, D_CONV - 1, CONV_DIM), dtp)
    zero_ssm = jnp.zeros((bp, SSM_HEADS, SSM_HEAD_DIM, D_STATE), dtp)
    args = (rel_bias, wts, ssm_conv_w[0], ssm_conv_b[0], ssm_dt_bias[0], ssm_a_log[0], ssm_d[0],
            ssm_norm[0])
    yp, kp, vp, ikp, cp, sp = _trunk(x_prompt, empty_k, empty_k, empty_ik, zero_conv, zero_ssm, *args)
    ys, ks, vs, iks, cs, ss = _trunk(x_sample, cache_k[0], cache_v[0], cache_idx_k[0],
                                     state_conv[0], state_ssm[0], *args)
    return (yp, ys, kp, vp, ikp, cp, sp, ks, vs, iks, cs, ss)
```

```python
import functools
import math

import jax
import jax.numpy as jnp
from jax import lax
from jax.experimental import pallas as pl
from jax.experimental.pallas import tpu as pltpu

F32 = jnp.float32
BF16 = jnp.bfloat16

D_MODEL = 1024
CHUNK = 64
CHUNK_SHIFT = 6
N_HEADS = 16
HEAD_DIM = 64
N_KV_HEADS = 4
GQ = N_HEADS // N_KV_HEADS
IDX_HEADS = 8
IDX_DIM = 64
TOPK_MAX = 256
IDX_SCALE = (IDX_HEADS * IDX_DIM) ** -0.5
ATTN_SCALE = HEAD_DIM ** -0.5
NUM_BUCKETS = 32
MAX_DISTANCE = 128
D_INNER = 2 * D_MODEL
SSM_HEAD_DIM = 64
SSM_HEADS = D_INNER // SSM_HEAD_DIM
SSM_GROUPS = 8
HEADS_PER_GROUP = SSM_HEADS // SSM_GROUPS
GROUP_W = HEADS_PER_GROUP * SSM_HEAD_DIM
D_STATE = 128
D_CONV = 4
CONV_DIM = D_INNER + 2 * SSM_GROUPS * D_STATE
EPS = 1e-6

Q_OFF = 0
K_OFF = N_HEADS * HEAD_DIM
V_OFF = K_OFF + N_KV_HEADS * HEAD_DIM
QI_OFF = V_OFF + N_KV_HEADS * HEAD_DIM
KI_OFF = QI_OFF + IDX_HEADS * IDX_DIM
WI_OFF = KI_OFF + IDX_DIM
ATTN_COLS = WI_OFF + IDX_HEADS

SSM_Z_OFF = 0
SSM_XBC_OFF = D_INNER
SSM_DT_OFF = D_INNER + CONV_DIM
SSM_COLS = SSM_DT_OFF + SSM_HEADS

LANE = 128
KEY_WINDOW_SHIFT = 9
KEY_WINDOW = 1 << KEY_WINDOW_SHIFT
QUERY_ROWS = 256
MAX_TILE_ROWS = 1024
CONV_COLS = 512
PROJ_ROWS = 512
SSM_PROJ_ROWS = 256
TAIL_ROWS = 512
MLP_CHUNK = 1024
LOG2E = 1.4426950408889634
VMEM_LIMIT = 56 * 1024 * 1024

NEG_BIG = -1e30
KEY_NEG_INF = -2139095041
MIN_NORMAL_KEY = 1 << 23
INT_MIN = -2147483648


def _round_up(n, m):
    return (n + m - 1) // m * m


def _rms(x, g):
    ms = jnp.mean(x * x, axis=-1, keepdims=True)
    return x * lax.rsqrt(ms + EPS) * g


def _silu(x):
    h = 0.5 * x
    return h + h * jnp.tanh(h)


def _params(*sem):
    return pltpu.CompilerParams(dimension_semantics=sem, vmem_limit_bytes=VMEM_LIMIT)


def _norm_matmul_kernel(x_ref, g_ref, w_ref, o_ref, xn_ref):
    @pl.when(pl.program_id(1) == 0)
    def _():
        xn_ref[...] = _rms(x_ref[...], g_ref[...]).astype(BF16)

    o_ref[...] = jnp.dot(xn_ref[...], w_ref[...], preferred_element_type=F32)


def _norm_matmul(x, g, w, *, tm, tn):
    m, k = x.shape
    n = w.shape[1]
    return pl.pallas_call(
        _norm_matmul_kernel,
        out_shape=jax.ShapeDtypeStruct((m, n), F32),
        grid=(m // tm, n // tn),
        in_specs=[pl.BlockSpec((tm, k), lambda i, j: (i, 0)),
                  pl.BlockSpec((1, k), lambda i, j: (0, 0)),
                  pl.BlockSpec((k, tn), lambda i, j: (0, j))],
        out_specs=pl.BlockSpec((tm, tn), lambda i, j: (i, j)),
        scratch_shapes=[pltpu.VMEM((tm, k), BF16)],
        compiler_params=_params("parallel", "arbitrary"),
        name="norm_matmul",
    )(x, g, w)


def _attn_proj_kernel(x_ref, g_ref, w_ref, pb_ref, k_ref, v_ref, ki_ref, kiwi_ref):
    xn = _rms(x_ref[...], g_ref[...]).astype(BF16)
    acc = jnp.dot(xn, w_ref[...], preferred_element_type=F32)
    pb_ref[...] = acc.astype(BF16)
    for g in range(N_KV_HEADS):
        k_ref[:, g, :] = acc[:, K_OFF + g * HEAD_DIM:K_OFF + (g + 1) * HEAD_DIM]
        v_ref[:, g, :] = acc[:, V_OFF + g * HEAD_DIM:V_OFF + (g + 1) * HEAD_DIM]
    ki_ref[...] = acc[:, KI_OFF:WI_OFF]
    kiwi_ref[...] = acc[:, KI_OFF:KI_OFF + LANE]


def _attn_proj(x, g, w, *, tm):
    m, k = x.shape
    n = w.shape[1]
    row = lambda i: (i, 0)
    kv_spec = pl.BlockSpec((tm, N_KV_HEADS, HEAD_DIM), lambda i: (i, 0, 0))
    return pl.pallas_call(
        _attn_proj_kernel,
        out_shape=[jax.ShapeDtypeStruct((m, n), BF16),
                   jax.ShapeDtypeStruct((m, N_KV_HEADS, HEAD_DIM), F32),
                   jax.ShapeDtypeStruct((m, N_KV_HEADS, HEAD_DIM), F32),
                   jax.ShapeDtypeStruct((m, IDX_DIM), F32),
                   jax.ShapeDtypeStruct((m, LANE), F32)],
        grid=(m // tm,),
        in_specs=[pl.BlockSpec((tm, k), row),
                  pl.BlockSpec((1, k), lambda i: (0, 0)),
                  pl.BlockSpec((k, n), lambda i: (0, 0))],
        out_specs=[pl.BlockSpec((tm, n), row), kv_spec, kv_spec,
                   pl.BlockSpec((tm, IDX_DIM), row), pl.BlockSpec((tm, LANE), row)],
        compiler_params=_params("parallel"),
        name="attn_proj",
    )(x, g, w)


def _tail_kernel(a_ref, wo_ref, r_ref, g_ref, wu_ref, wd_ref, gf_ref, o_ref, *, tf, final_norm):
    x1 = r_ref[...] + jnp.dot(a_ref[...], wo_ref[...], preferred_element_type=F32)
    xn = _rms(x1, g_ref[...]).astype(BF16)
    acc = x1
    for f0 in range(0, wu_ref.shape[1], tf):
        h = jnp.maximum(jnp.dot(xn, wu_ref[:, f0:f0 + tf], preferred_element_type=F32), 0.0)
        acc = acc + jnp.dot((h * h).astype(BF16), wd_ref[f0:f0 + tf, :], preferred_element_type=F32)
    if final_norm:
        acc = _rms(acc, gf_ref[...])
    o_ref[...] = acc


def _layer_tail(a, wo, resid, g, wu, wd, gf, *, tm, tf, final_norm):
    m, ka = a.shape
    d = wo.shape[1]
    ff = wu.shape[1]
    row = lambda i: (i, 0)
    whole = lambda shape: pl.BlockSpec(shape, lambda i: (0, 0), pipeline_mode=pl.Buffered(1))
    return pl.pallas_call(
        functools.partial(_tail_kernel, tf=tf, final_norm=final_norm),
        out_shape=jax.ShapeDtypeStruct((m, d), F32),
        grid=(m // tm,),
        in_specs=[pl.BlockSpec((tm, ka), row), whole((ka, d)), pl.BlockSpec((tm, d), row),
                  whole((1, d)), whole((d, ff)), whole((ff, d)), whole((1, d))],
        out_specs=pl.BlockSpec((tm, d), row),
        compiler_params=_params("parallel"),
        name="layer_tail",
    )(a, wo, resid, g, wu, wd, gf)


def _attn_kernel(*refs, tq, t, past, front_pad, topk):
    if past:
        (q_ref, qi_ref, kiwi_ref, kn_ref, vn_ref, kin_ref, pk_ref, pv_ref, pki_ref, nb_ref, o_ref,
         k_ref, v_ref, ki_ref, score_ref, qs_ref, qis_ref, wcol_ref, madd_ref, s_ref, m_ref, acc_ref) = refs
    else:
        (q_ref, qi_ref, kiwi_ref, kn_ref, vn_ref, kin_ref, nb_ref, o_ref,
         k_ref, v_ref, ki_ref, score_ref, qs_ref, qis_ref, wcol_ref, madd_ref, s_ref, m_ref, acc_ref) = refs
    w = KEY_WINDOW
    rows = GQ * tq
    rb = s_ref.shape[1]
    mrows = madd_ref.shape[0]
    keys_on_rows = score_ref.shape[1] == w
    kax = 0 if keys_on_rows else 1
    qshape = (1, tq) if keys_on_rows else (tq, 1)
    i = pl.program_id(1)
    q0 = past + i * tq
    lv = q0 + tq
    if t == tq:
        nw = (past + tq + w - 1) >> KEY_WINDOW_SHIFT
        n_all = float(nw * w)
        window_loop = functools.partial(lax.fori_loop, unroll=True)
    else:
        nw = jnp.right_shift(lv + (w - 1), KEY_WINDOW_SHIFT)
        n_all = (nw * w).astype(F32)
        window_loop = lax.fori_loop
    qchunk = jnp.right_shift(q0 + lax.broadcasted_iota(jnp.int32, qshape, 1 - kax), CHUNK_SHIFT)
    kidx = lax.broadcasted_iota(jnp.int32, (w, 1) if keys_on_rows else (1, w), kax)
    nt = (((1,), (1,)), ((), ()))

    def win_row(j):
        return pl.multiple_of(lv + front_pad - w * (j + 1), tq)

    @pl.when(i == 0)
    def _():
        ones_col = jnp.where(lax.broadcasted_iota(jnp.int32, (1, LANE - HEAD_DIM), 1) == 0, 1.0, 0.0)

        def put(r0, n, kc, vc, kic):
            ones = jnp.broadcast_to(ones_col, (n, LANE - HEAD_DIM)).astype(BF16)
            for g in range(N_KV_HEADS):
                gs = slice(g * HEAD_DIM, (g + 1) * HEAD_DIM)
                k_ref[g, r0:r0 + n, :] = kc[:, gs]
                v_ref[g, r0:r0 + n, :] = jnp.concatenate([vc[:, gs], ones], axis=1)
            ki_ref[r0:r0 + n, :] = kic

        kvw = N_KV_HEADS * HEAD_DIM
        if front_pad:
            put(0, front_pad, jnp.zeros((front_pad, kvw), BF16), jnp.zeros((front_pad, kvw), BF16),
                jnp.zeros((front_pad, IDX_DIM), BF16))
        step = min(t, w)
        for c0 in range(0, past, w):
            put(front_pad + c0, w, pk_ref[c0:c0 + w, :].astype(BF16), pv_ref[c0:c0 + w, :].astype(BF16),
                pki_ref[c0:c0 + w, :].astype(BF16))
        for c0 in range(0, t, step):
            put(front_pad + past + c0, step, kn_ref[c0:c0 + step, :], vn_ref[c0:c0 + step, :],
                kin_ref[c0:c0 + step, 0:IDX_DIM])

    for h in range(N_HEADS):
        g, jj = divmod(h, GQ)
        qs_ref[g, jj * tq:(jj + 1) * tq, :] = q_ref[:, h * HEAD_DIM:(h + 1) * HEAD_DIM]

    if keys_on_rows:
        wi_t = kiwi_ref[...].T[IDX_DIM:IDX_DIM + IDX_HEADS, :]
        for hp in range(IDX_HEADS // 2):
            for u in range(2):
                h = 2 * hp + u
                qis_ref[hp, u * tq:(u + 1) * tq, :] = qi_ref[:, h * IDX_DIM:(h + 1) * IDX_DIM]
    else:
        wi = kiwi_ref[:, IDX_DIM:IDX_DIM + IDX_HEADS]
        for h in range(IDX_HEADS):
            qis_ref[0, h * tq:(h + 1) * tq, :] = qi_ref[:, h * IDX_DIM:(h + 1) * IDX_DIM]
            wcol_ref[h * tq:(h + 1) * tq, :] = wi[:, h:h + 1]

    def visible_scores(sc, kpos):
        vis = (kpos >= 0) & (jnp.right_shift(kpos, CHUNK_SHIFT) <= qchunk)
        return jnp.where(vis, sc, -jnp.inf)

    def score_body(j, carry):
        if keys_on_rows:
            for k0 in range(0, w, w // 2):
                kib = ki_ref[pl.ds(win_row(j) + k0, w // 2), :]
                sc = jnp.zeros((w // 2, tq), F32)
                for hp in range(IDX_HEADS // 2):
                    s2 = jnp.maximum(lax.dot_general(kib, qis_ref[hp], nt, preferred_element_type=F32), 0.0)
                    sc = sc + s2[:, 0:tq] * wi_t[2 * hp:2 * hp + 1, :]
                    sc = sc + s2[:, tq:2 * tq] * wi_t[2 * hp + 1:2 * hp + 2, :]
                kpos = (lv - w * (j + 1)) + kidx[k0:k0 + w // 2]
                score_ref[j, k0:k0 + w // 2, :] = visible_scores(sc * IDX_SCALE, kpos)
        else:
            kib = ki_ref[pl.ds(win_row(j), w), :]
            s8 = jnp.maximum(lax.dot_general(qis_ref[0], kib, nt, preferred_element_type=F32), 0.0)
            s8 = s8 * wcol_ref[...]
            sc = s8[0:tq]
            for h in range(1, IDX_HEADS):
                sc = sc + s8[h * tq:(h + 1) * tq]
            score_ref[j] = visible_scores(sc * IDX_SCALE, (lv - w * (j + 1)) + kidx)
        return carry

    window_loop(0, nw, score_body, 0)

    def key_to_f32(key):
        return pltpu.bitcast(key ^ (jnp.right_shift(key, 31) & 0x7FFFFFFF), F32)

    if keys_on_rows:
        part = (2 * 8, tq)

        def fold(c):
            return jnp.sum(c.reshape(w // part[0], part[0], tq), axis=0)
    else:
        part = (tq, LANE)

        def fold(c):
            return (c[:, 0:LANE] + c[:, LANE:2 * LANE]) + (c[:, 2 * LANE:3 * LANE] + c[:, 3 * LANE:4 * LANE])

    def total(acc):
        return jnp.sum(acc, axis=kax, keepdims=True)

    def count_ge(cand):
        cand_f = key_to_f32(jnp.where((cand > 0) & (cand < MIN_NORMAL_KEY), MIN_NORMAL_KEY, cand))

        def body(j, acc):
            return acc + fold(jnp.where(score_ref[j] >= cand_f, 1.0, 0.0))
        return total(window_loop(0, nw, body, jnp.zeros(part, F32)))

    def step(cand, carry):
        prefix, n_lo, n_hi = carry
        c = count_ge(cand)
        take = c >= float(topk)
        return jnp.where(take, cand, prefix), jnp.where(take, c, n_lo), jnp.where(take, n_hi, c)

    def search_body(bit, carry):
        return step(carry[0] + lax.shift_left(jnp.int32(1), 30 - bit), carry)

    start = (jnp.full(qshape, INT_MIN, jnp.int32), jnp.full(qshape, 1.0, F32) * n_all, jnp.zeros(qshape, F32))
    sign = step(jnp.zeros(qshape, jnp.int32), start)
    tstar, n_ge, n_gt = lax.fori_loop(0, 31, search_body, sign)

    need = float(topk) - n_gt
    n_eq = n_ge - n_gt
    bad = (n_eq > need) & (tstar > KEY_NEG_INF)
    n_bad = jnp.sum(jnp.where(bad, 1.0, 0.0))
    tstar_f = key_to_f32(tstar)

    @pl.when(n_bad > 0.0)
    def _():
        r_i = lax.broadcasted_iota(jnp.int32, (w, w), 0)
        c_i = lax.broadcasted_iota(jnp.int32, (w, w), 1)
        earlier = jnp.where(c_i < r_i if keys_on_rows else r_i < c_i, 1.0, 0.0).astype(BF16)

        def tie_body(t, seen):
            j = nw - 1 - t
            kt = score_ref[j]
            eq = kt == tstar_f
            eqf = jnp.where(eq, 1.0, 0.0)
            if keys_on_rows:
                inwin = jnp.dot(earlier, eqf.astype(BF16), preferred_element_type=F32)
            else:
                inwin = jnp.dot(eqf.astype(BF16), earlier, preferred_element_type=F32)
            score_ref[j] = jnp.where(eq & bad & (seen + inwin >= need), -jnp.inf, kt)
            return seen + total(eqf)

        lax.fori_loop(0, nw, tie_body, jnp.zeros(qshape, F32))

    teff = key_to_f32(jnp.maximum(tstar, KEY_NEG_INF + 1))

    def attend(j, near):
        row = win_row(j)
        tile = jnp.where(score_ref[j] >= teff, 0.0, NEG_BIG)
        if keys_on_rows:
            tile = tile.T
        for r in range(mrows // tq):
            madd_ref[r * tq:(r + 1) * tq, :] = tile
        tiles = [(g, r0) for g in range(N_KV_HEADS) for r0 in range(0, rows, rb)]

        for t, (g, r0) in enumerate(tiles):
            kb = k_ref[g, pl.ds(row, w), :]
            s = lax.dot_general(qs_ref[g, r0:r0 + rb, :], kb, nt, preferred_element_type=F32)
            s = s + madd_ref[r0 % mrows:r0 % mrows + rb, :]
            if near:
                s = s + nb_ref[g, r0:r0 + rb, :]
            s_ref[t] = s
        for t, (g, r0) in enumerate(tiles):
            rs = slice(r0, r0 + rb)
            s = s_ref[t]
            m_new = jnp.broadcast_to(jnp.max(s, axis=1, keepdims=True), (rb, LANE))
            if not near:
                m_old = m_ref[g, rs, :]
                m_new = jnp.maximum(m_old, m_new)
            p = jnp.exp2(s - jnp.concatenate([m_new] * (w // LANE), axis=1))
            pv = jnp.dot(p.astype(BF16), v_ref[g, pl.ds(row, w), :], preferred_element_type=F32)
            acc_ref[g, rs, :] = pv if near else jnp.exp2(m_old - m_new) * acc_ref[g, rs, :] + pv
            m_ref[g, rs, :] = m_new

    attend(0, True)

    def far_body(j, carry):
        attend(j, False)
        return carry

    window_loop(1, nw, far_body, 0)

    for h in range(N_HEADS):
        g, jj = divmod(h, GQ)
        a = acc_ref[g, jj * tq:(jj + 1) * tq, :]
        o_ref[:, h * HEAD_DIM:(h + 1) * HEAD_DIM] = (
            a[:, 0:HEAD_DIM] / a[:, HEAD_DIM:HEAD_DIM + 1]).astype(o_ref.dtype)


def _rel_bucket(rel):
    half = NUM_BUCKETS // 2
    max_exact = half // 2
    base = jnp.where(rel > 0, half, 0)
    n = jnp.abs(rel)
    nf = jnp.maximum(n, 1).astype(jnp.float32)
    large = max_exact + jnp.floor(jnp.log(nf / max_exact) / math.log(MAX_DISTANCE / max_exact)
                                  * (half - max_exact)).astype(jnp.int32)
    large = jnp.minimum(large, half - 1)
    return base + jnp.where(n < max_exact, n, large)


def _near_bias(rel_bias, tq):
    r = jnp.arange(tq, dtype=jnp.int32)[:, None]
    c = jnp.arange(KEY_WINDOW, dtype=jnp.int32)[None, :]
    rel = (tq - KEY_WINDOW + c) - r
    tab = rel_bias.astype(F32)

    def pick(bucket):
        bucket = jnp.bitwise_and(bucket, NUM_BUCKETS - 1)
        d = bucket[..., None].astype(F32) - jnp.arange(NUM_BUCKETS, dtype=F32)
        return jnp.maximum(1.0 - jnp.abs(d), 0.0)

    far = _rel_bucket(jnp.full((1,), -(1 << 20), jnp.int32))
    far_row = jnp.dot(pick(far), tab, precision=lax.Precision.HIGHEST)
    nb = jnp.einsum('qcb,bh->hqc', pick(_rel_bucket(rel)), (tab - far_row) * LOG2E,
                    precision=lax.Precision.HIGHEST)
    return nb.reshape(N_KV_HEADS, GQ * tq, KEY_WINDOW)


def _dsa_mixer(projb, kiwi, past_k, past_v, past_ik, rel_bias, b, t):
    past = past_k.shape[1]
    n_keys = past + t
    topk = min(TOPK_MAX, n_keys // 4)
    tq = QUERY_ROWS if t % QUERY_ROWS == 0 else t
    assert KEY_WINDOW % tq == 0 and past % KEY_WINDOW == 0 and tq % 16 == 0 and n_keys >= KEY_WINDOW
    assert t % min(t, KEY_WINDOW) == 0
    front_pad = KEY_WINDOW - tq
    lp = front_pad + n_keys
    nq = t // tq
    nw_max = (n_keys + KEY_WINDOW - 1) // KEY_WINDOW
    rows = GQ * tq
    rb = min(rows, MAX_TILE_ROWS)
    keys_on_rows = tq % LANE == 0
    assert keys_on_rows or IDX_HEADS * tq <= LANE
    kvw = N_KV_HEADS * HEAD_DIM
    qw = N_HEADS * HEAD_DIM
    qiw = IDX_HEADS * IDX_DIM

    qrow = lambda bi, i: (bi * nq + i, 0)
    stream_buf = pl.Buffered(1 if nq > 1 else 2)
    in_specs = [pl.BlockSpec((tq, qw), qrow),
                pl.BlockSpec((tq, qiw), lambda bi, i: (bi * nq + i, QI_OFF // qiw)),
                pl.BlockSpec((tq, LANE), qrow),
                pl.BlockSpec((t, kvw), lambda bi, i: (bi, K_OFF // kvw), pipeline_mode=stream_buf),
                pl.BlockSpec((t, kvw), lambda bi, i: (bi, V_OFF // kvw), pipeline_mode=stream_buf),
                pl.BlockSpec((t, LANE), lambda bi, i: (bi, KI_OFF // LANE), pipeline_mode=stream_buf)]
    args = [projb, projb, kiwi, projb, projb, projb]
    if past:
        in_specs += [pl.BlockSpec((None, past, kvw), lambda bi, i: (bi, 0, 0)),
                     pl.BlockSpec((None, past, kvw), lambda bi, i: (bi, 0, 0)),
                     pl.BlockSpec((None, past, IDX_DIM), lambda bi, i: (bi, 0, 0))]
        args += [past_k.reshape(b, past, kvw), past_v.reshape(b, past, kvw), past_ik]
    in_specs.append(pl.BlockSpec((N_KV_HEADS, rows, KEY_WINDOW), lambda bi, i: (0, 0, 0),
                                 pipeline_mode=pl.Buffered(1)))
    args.append(_near_bias(rel_bias, tq))

    kern = functools.partial(_attn_kernel, tq=tq, t=t, past=past, front_pad=front_pad, topk=topk)
    return pl.pallas_call(
        kern,
        out_shape=jax.ShapeDtypeStruct((b * t, qw), BF16),
        grid=(b, nq),
        in_specs=in_specs,
        out_specs=pl.BlockSpec((tq, qw), qrow),
        scratch_shapes=[pltpu.VMEM((N_KV_HEADS, lp, HEAD_DIM), BF16),
                        pltpu.VMEM((N_KV_HEADS, lp, LANE), BF16),
                        pltpu.VMEM((lp, IDX_DIM), BF16),
                        pltpu.VMEM((nw_max, KEY_WINDOW, tq) if keys_on_rows
                                   else (nw_max, tq, KEY_WINDOW), F32),
                        pltpu.VMEM((N_KV_HEADS, rows, HEAD_DIM), BF16),
                        pltpu.VMEM((IDX_HEADS // 2, 2 * tq, IDX_DIM) if keys_on_rows
                                   else (1, IDX_HEADS * tq, IDX_DIM), BF16),
                        pltpu.VMEM((8, 1) if keys_on_rows else (IDX_HEADS * tq, 1), F32),
                        pltpu.VMEM((max(rb, tq), KEY_WINDOW), F32),
                        pltpu.VMEM((N_KV_HEADS * rows // rb, rb, KEY_WINDOW), F32),
                        pltpu.VMEM((N_KV_HEADS, rows, LANE), F32),
                        pltpu.VMEM((N_KV_HEADS, rows, LANE), F32)],
        compiler_params=_params("arbitrary", "arbitrary"),
        name="dsa_attention",
    )(*args)


def _ssd_kernel(z_ref, xlo_ref, xhi_ref, dt_ref, cst_ref, st0_ref, cw_ref, cb_ref, dtb_ref, alog_ref,
                dexp_ref, nw_ref, expand_ref, y_ref, ncv_ref, stout_ref,
                xpad_ref, act_ref, yd_ref, st_ref, *, q):
    c = pl.program_id(1)
    half = CONV_DIM // 2

    @pl.when(c == 0)
    def _():
        xpad_ref[8 - (D_CONV - 1):8, :] = cst_ref[...]
        for g in range(SSM_GROUPS):
            hs = slice(g * HEADS_PER_GROUP, (g + 1) * HEADS_PER_GROUP)
            st_ref[g] = st0_ref[hs].reshape(GROUP_W, D_STATE).T

    cc = CONV_COLS
    for c0 in range(0, CONV_DIM, cc):
        cs = slice(c0, c0 + cc)
        src = xlo_ref[:, c0:c0 + cc] if c0 < half else xhi_ref[:, c0 - half:c0 - half + cc]
        xpad_ref[8:8 + q, cs] = src
        conv = cb_ref[:, cs] + xpad_ref[5:5 + q, cs] * cw_ref[0:1, cs]
        conv = conv + xpad_ref[6:6 + q, cs] * cw_ref[1:2, cs]
        conv = conv + xpad_ref[7:7 + q, cs] * cw_ref[2:3, cs]
        conv = conv + xpad_ref[8:8 + q, cs] * cw_ref[3:4, cs]
        act_ref[:, cs] = _silu(conv)
        xpad_ref[0:8, cs] = xpad_ref[q:q + 8, cs]

    hi = lax.Precision.HIGHEST
    x_dt = dt_ref[:, 0:SSM_HEADS] + dtb_ref[...]
    dt = jnp.maximum(x_dt, 0.0) + jnp.log1p(jnp.exp(-jnp.abs(x_dt)))
    a = dt * (-jnp.exp(alog_ref[...]))
    rr = lax.broadcasted_iota(jnp.int32, (q, q), 0)
    cl = lax.broadcasted_iota(jnp.int32, (q, q), 1)
    causal = rr >= cl
    tri = jnp.where(causal, 1.0, 0.0)
    a_cs = jnp.dot(tri, a, precision=hi, preferred_element_type=F32)
    eye = jnp.where(lax.broadcasted_iota(jnp.int32, (SSM_HEADS, SSM_HEADS), 0)
                    == lax.broadcasted_iota(jnp.int32, (SSM_HEADS, SSM_HEADS), 1), 1.0, 0.0)
    a_cs_t = lax.dot_general(eye, a_cs, (((1,), (1,)), ((), ())), precision=hi,
                             preferred_element_type=F32)
    a_end = a_cs[q - 1:q, :]
    per_head = jnp.concatenate([dt, jnp.exp(a_cs), jnp.exp(a_end - a_cs)], axis=0)
    top = per_head.astype(BF16)
    rest = (per_head - top.astype(F32)).astype(BF16)
    spread = jnp.dot(jnp.concatenate([top, rest], axis=0), expand_ref[...],
                     preferred_element_type=F32)
    dt_x = spread[0:q] + spread[3 * q:4 * q]
    e_x = spread[q:2 * q] + spread[4 * q:5 * q]
    dte_x = spread[2 * q:3 * q] + spread[5 * q:6 * q]
    dec_x = e_x[q - 1:q, :]

    for g in range(SSM_GROUPS):
        gs = slice(g * GROUP_W, (g + 1) * GROUP_W)
        bg = act_ref[:, D_INNER + g * D_STATE:D_INNER + (g + 1) * D_STATE].astype(BF16)
        cg = act_ref[:, D_INNER + (SSM_GROUPS + g) * D_STATE:
                     D_INNER + (SSM_GROUPS + g + 1) * D_STATE].astype(BF16)
        cb = lax.dot_general(cg, bg, (((1,), (1,)), ((), ())), preferred_element_type=F32)
        xs_g = act_ref[:, gs]
        xd_g = xs_g * dt_x[:, gs]
        st_g = st_ref[g]
        y_off = jnp.dot(cg, st_g.astype(BF16), preferred_element_type=F32) * e_x[:, gs]
        for jj in range(HEADS_PER_GROUP):
            h = g * HEADS_PER_GROUP + jj
            seg = a_cs[:, h:h + 1] - a_cs_t[h:h + 1, :]
            decay = jnp.where(causal, jnp.exp(seg), 0.0)
            hs = slice(jj * SSM_HEAD_DIM, (jj + 1) * SSM_HEAD_DIM)
            yd_ref[:, hs] = jnp.dot((cb * decay).astype(BF16), xd_g[:, hs].astype(BF16),
                                    preferred_element_type=F32)
        st_ref[g] = st_g * dec_x[:, gs] + lax.dot_general(
            bg, (xd_g * dte_x[:, gs]).astype(BF16), (((0,), (0,)), ((), ())),
            preferred_element_type=F32)
        y = yd_ref[...] + y_off + xs_g * dexp_ref[:, gs]
        zg = z_ref[:, gs]
        yg = y * _silu(zg)
        ms = jnp.mean(yg * yg, axis=-1, keepdims=True)
        y_ref[:, gs] = (yg * lax.rsqrt(ms + EPS) * nw_ref[:, gs]).astype(y_ref.dtype)

    @pl.when(c == pl.num_programs(1) - 1)
    def _():
        ncv_ref[...] = xpad_ref[8 - (D_CONV - 1):8, :]
        for g in range(SSM_GROUPS):
            hs = slice(g * HEADS_PER_GROUP, (g + 1) * HEADS_PER_GROUP)
            stout_ref[hs] = st_ref[g].T.reshape(HEADS_PER_GROUP, SSM_HEAD_DIM, D_STATE)


def _ssd_mixer(proj, conv_state, ssm_state, conv_w, conv_b, dt_bias, a_log, d_skip, norm_w, b, t):
    q = 128 if t % 128 == 0 else t
    assert q % 8 == 0 and t >= D_CONV - 1 and CONV_DIM == 2 * D_INNER
    nc = t // q
    expand = jnp.repeat(jnp.eye(SSM_HEADS, dtype=BF16), SSM_HEAD_DIM, axis=1)
    dexp = jnp.repeat(d_skip.astype(F32), SSM_HEAD_DIM)[None, :]
    row = lambda k: (lambda bi, c: (bi * nc + c, k))
    const2 = lambda bi, c: (0, 0)
    state_spec = pl.BlockSpec((None, SSM_HEADS, SSM_HEAD_DIM, D_STATE), lambda bi, c: (bi, 0, 0, 0))
    conv_spec = pl.BlockSpec((None, D_CONV - 1, CONV_DIM), lambda bi, c: (bi, 0, 0))
    y, new_conv, new_ssm = pl.pallas_call(
        functools.partial(_ssd_kernel, q=q),
        out_shape=[jax.ShapeDtypeStruct((b * t, D_INNER), BF16),
                   jax.ShapeDtypeStruct((b, D_CONV - 1, CONV_DIM), F32),
                   jax.ShapeDtypeStruct((b, SSM_HEADS, SSM_HEAD_DIM, D_STATE), F32)],
        grid=(b, nc),
        in_specs=[pl.BlockSpec((q, D_INNER), row(SSM_Z_OFF // D_INNER)),
                  pl.BlockSpec((q, D_INNER), row(SSM_XBC_OFF // D_INNER)),
                  pl.BlockSpec((q, D_INNER), row(SSM_XBC_OFF // D_INNER + 1)),
                  pl.BlockSpec((q, LANE), row(SSM_DT_OFF // LANE)),
                  conv_spec,
                  state_spec,
                  pl.BlockSpec((D_CONV, CONV_DIM), const2),
                  pl.BlockSpec((1, CONV_DIM), const2),
                  pl.BlockSpec((1, SSM_HEADS), const2),
                  pl.BlockSpec((1, SSM_HEADS), const2),
                  pl.BlockSpec((1, D_INNER), const2),
                  pl.BlockSpec((1, D_INNER), const2),
                  pl.BlockSpec((SSM_HEADS, D_INNER), const2)],
        out_specs=[pl.BlockSpec((q, D_INNER), row(0)), conv_spec, state_spec],
        scratch_shapes=[pltpu.VMEM((q + 8, CONV_DIM), F32),
                        pltpu.VMEM((q, CONV_DIM), F32),
                        pltpu.VMEM((q, GROUP_W), F32),
                        pltpu.VMEM((SSM_GROUPS, D_STATE, GROUP_W), F32)],
        compiler_params=_params("arbitrary", "arbitrary"),
        name="ssd_mixer",
    )(proj, proj, proj, proj, conv_state.astype(F32), ssm_state.astype(F32), conv_w.astype(F32),
      conv_b.astype(F32)[None, :], dt_bias.astype(F32)[None, :], a_log.astype(F32)[None, :], dexp,
      norm_w.astype(F32)[None, :], expand)
    return y, new_conv, new_ssm.astype(ssm_state.dtype)


def _row_tile(m, cap):
    tm = cap
    while m % tm:
        tm //= 2
    assert tm >= 8, (m, cap)
    return tm


def _pad_cols(w, n):
    return jnp.pad(w, ((0, 0), (0, n - w.shape[1])))


def _prep_weights(norm_mix, norm_ffn, norm_final, attn_w_in, attn_w_o, ssm_w_in, ssm_w_out,
                  mlp_w_up, mlp_w_down):
    col_scale = jnp.where(jnp.arange(ATTN_COLS) < K_OFF, ATTN_SCALE * LOG2E, 1.0).astype(F32)
    return dict(
        attn_in=_pad_cols(attn_w_in[0] * col_scale, _round_up(ATTN_COLS, LANE)).astype(BF16),
        attn_o=attn_w_o[0].astype(BF16),
        ssm_in=ssm_w_in[0].astype(BF16),
        ssm_out=ssm_w_out[0].astype(BF16),
        up=[mlp_w_up[i].astype(BF16) for i in range(2)],
        down=[mlp_w_down[i].astype(BF16) for i in range(2)],
        g_mix=[norm_mix[i].astype(F32)[None, :] for i in range(2)],
        g_ffn=[norm_ffn[i].astype(F32)[None, :] for i in range(2)],
        g_final=norm_final.astype(F32)[None, :],
    )


def _trunk(x, past_k, past_v, past_ik, conv_st, ssm_st, rel_bias, wts, ssm_conv_w, ssm_conv_b,
           ssm_dt_bias, ssm_a_log, ssm_d, ssm_norm):
    b, t, d = x.shape
    m = b * t
    x2 = x.reshape(m, d).astype(F32)
    tail_rows = _row_tile(m, TAIL_ROWS)

    projb, k_new, v_new, ki_new, kiwi = _attn_proj(x2, wts["g_mix"][0], wts["attn_in"],
                                                   tm=_row_tile(m, PROJ_ROWS))
    ao = _dsa_mixer(projb, kiwi, past_k, past_v, past_ik, rel_bias, b, t)
    x2 = _layer_tail(ao, wts["attn_o"], x2, wts["g_ffn"][0], wts["up"][0], wts["down"][0],
                     wts["g_final"], tm=tail_rows, tf=MLP_CHUNK, final_norm=False)

    proj = _norm_matmul(x2, wts["g_mix"][1], wts["ssm_in"], tm=_row_tile(m, SSM_PROJ_ROWS),
                        tn=wts["ssm_in"].shape[1])
    y, new_conv, new_ssm = _ssd_mixer(proj, conv_st, ssm_st, ssm_conv_w, ssm_conv_b, ssm_dt_bias,
                                      ssm_a_log, ssm_d, ssm_norm, b, t)
    x2 = _layer_tail(y, wts["ssm_out"], x2, wts["g_ffn"][1], wts["up"][1], wts["down"][1],
                     wts["g_final"], tm=tail_rows, tf=MLP_CHUNK, final_norm=True)

    dt = x.dtype
    return (x2.reshape(b, t, d).astype(dt),
            k_new.reshape(1, b, t, N_KV_HEADS, HEAD_DIM).astype(dt),
            v_new.reshape(1, b, t, N_KV_HEADS, HEAD_DIM).astype(dt),
            ki_new.reshape(1, b, t, IDX_DIM).astype(dt), new_conv[None].astype(dt), new_ssm[None])


def kernel(x_prompt, x_sample, cache_k, cache_v, cache_idx_k, state_conv, state_ssm, rel_bias, norm_mix, norm_ffn, norm_final, attn_w_in, attn_w_o, ssm_w_in, ssm_conv_w, ssm_conv_b, ssm_dt_bias, ssm_a_log, ssm_d, ssm_norm, ssm_w_out, mlp_w_up, mlp_w_down):
    wts = _prep_weights(norm_mix, norm_ffn, norm_final, attn_w_in, attn_w_o, ssm_w_in, ssm_w_out,
                        mlp_w_up, mlp_w_down)
    bp = x_prompt.shape[0]
    dtp = x_prompt.dtype
    empty_k = jnp.zeros((bp, 0, N_KV_HEADS, HEAD_DIM), dtp)
    empty_ik = jnp.zeros((bp, 0, IDX_DIM), dtp)
    zero_conv = jnp.zeros((bp, D_CONV - 1, CONV_DIM), dtp)
    zero_ssm = jnp.zeros((bp, SSM_HEADS, SSM_HEAD_DIM, D_STATE), dtp)
    args = (rel_bias, wts, ssm_conv_w[0], ssm_conv_b[0], ssm_dt_bias[0], ssm_a_log[0], ssm_d[0],
            ssm_norm[0])
    yp, kp, vp, ikp, cp, sp = _trunk(x_prompt, empty_k, empty_k, empty_ik, zero_conv, zero_ssm, *args)
    ys, ks, vs, iks, cs, ss = _trunk(x_sample, cache_k[0], cache_v[0], cache_idx_k[0],
                                     state_conv[0], state_ssm[0], *args)
    return (yp, ys, kp, vp, ikp, cp, sp, ks, vs, iks, cs, ss)
```

```python
import functools
import math

import jax
import jax.numpy as jnp
from jax import lax
from jax.experimental import pallas as pl
from jax.experimental.pallas import tpu as pltpu

F32 = jnp.float32
BF16 = jnp.bfloat16

D_MODEL = 1024
CHUNK = 64
CHUNK_SHIFT = 6
N_HEADS = 16
HEAD_DIM = 64
N_KV_HEADS = 4
GQ = N_HEADS // N_KV_HEADS
IDX_HEADS = 8
IDX_DIM = 64
TOPK_MAX = 256
IDX_SCALE = (IDX_HEADS * IDX_DIM) ** -0.5
ATTN_SCALE = HEAD_DIM ** -0.5
NUM_BUCKETS = 32
MAX_DISTANCE = 128
D_INNER = 2 * D_MODEL
SSM_HEAD_DIM = 64
SSM_HEADS = D_INNER // SSM_HEAD_DIM
SSM_GROUPS = 8
HEADS_PER_GROUP = SSM_HEADS // SSM_GROUPS
GROUP_W = HEADS_PER_GROUP * SSM_HEAD_DIM
D_STATE = 128
D_CONV = 4
CONV_DIM = D_INNER + 2 * SSM_GROUPS * D_STATE
EPS = 1e-6

Q_OFF = 0
K_OFF = N_HEADS * HEAD_DIM
V_OFF = K_OFF + N_KV_HEADS * HEAD_DIM
QI_OFF = V_OFF + N_KV_HEADS * HEAD_DIM
KI_OFF = QI_OFF + IDX_HEADS * IDX_DIM
WI_OFF = KI_OFF + IDX_DIM
ATTN_COLS = WI_OFF + IDX_HEADS

SSM_Z_OFF = 0
SSM_XBC_OFF = D_INNER
SSM_DT_OFF = D_INNER + CONV_DIM
SSM_COLS = SSM_DT_OFF + SSM_HEADS

LANE = 128
KEY_WINDOW_SHIFT = 9
KEY_WINDOW = 1 << KEY_WINDOW_SHIFT
QUERY_ROWS = 256
MAX_TILE_ROWS = 1024
CONV_COLS = 512
PROJ_ROWS = 512
SSM_PROJ_ROWS = 512
TAIL_ROWS = 512
MLP_CHUNK = 1024
LOG2E = 1.4426950408889634
VMEM_LIMIT = 56 * 1024 * 1024

NEG_BIG = -1e30
KEY_NEG_INF = -2139095041
MIN_NORMAL_KEY = 1 << 23
INT_MIN = -2147483648


def _round_up(n, m):
    return (n + m - 1) // m * m


def _rms(x, g):
    ms = jnp.mean(x * x, axis=-1, keepdims=True)
    return x * lax.rsqrt(ms + EPS) * g


def _silu(x):
    h = 0.5 * x
    return h + h * jnp.tanh(h)


def _params(*sem):
    return pltpu.CompilerParams(dimension_semantics=sem, vmem_limit_bytes=VMEM_LIMIT)


def _norm_matmul_kernel(x_ref, g_ref, w_ref, o_ref, xn_ref):
    @pl.when(pl.program_id(1) == 0)
    def _():
        xn_ref[...] = _rms(x_ref[...], g_ref[...]).astype(BF16)

    o_ref[...] = jnp.dot(xn_ref[...], w_ref[...], preferred_element_type=F32)


def _norm_matmul(x, g, w, *, tm, tn):
    m, k = x.shape
    n = w.shape[1]
    return pl.pallas_call(
        _norm_matmul_kernel,
        out_shape=jax.ShapeDtypeStruct((m, n), F32),
        grid=(m // tm, n // tn),
        in_specs=[pl.BlockSpec((tm, k), lambda i, j: (i, 0)),
                  pl.BlockSpec((1, k), lambda i, j: (0, 0)),
                  pl.BlockSpec((k, tn), lambda i, j: (0, j), pipeline_mode=pl.Buffered(1 if tn == n else 2))],
        out_specs=pl.BlockSpec((tm, tn), lambda i, j: (i, j)),
        scratch_shapes=[pltpu.VMEM((tm, k), BF16)],
        compiler_params=_params("parallel", "arbitrary"),
        name="norm_matmul",
    )(x, g, w)


def _attn_proj_kernel(x_ref, g_ref, w_ref, pb_ref, k_ref, v_ref, ki_ref, kiwi_ref):
    xn = _rms(x_ref[...], g_ref[...]).astype(BF16)
    acc = jnp.dot(xn, w_ref[...], preferred_element_type=F32)
    pb_ref[...] = acc.astype(BF16)
    for g in range(N_KV_HEADS):
        k_ref[:, g, :] = acc[:, K_OFF + g * HEAD_DIM:K_OFF + (g + 1) * HEAD_DIM]
        v_ref[:, g, :] = acc[:, V_OFF + g * HEAD_DIM:V_OFF + (g + 1) * HEAD_DIM]
    ki_ref[...] = acc[:, KI_OFF:WI_OFF]
    kiwi_ref[...] = acc[:, KI_OFF:KI_OFF + LANE]


def _attn_proj(x, g, w, *, tm):
    m, k = x.shape
    n = w.shape[1]
    row = lambda i: (i, 0)
    kv_spec = pl.BlockSpec((tm, N_KV_HEADS, HEAD_DIM), lambda i: (i, 0, 0))
    return pl.pallas_call(
        _attn_proj_kernel,
        out_shape=[jax.ShapeDtypeStruct((m, n), BF16),
                   jax.ShapeDtypeStruct((m, N_KV_HEADS, HEAD_DIM), F32),
                   jax.ShapeDtypeStruct((m, N_KV_HEADS, HEAD_DIM), F32),
                   jax.ShapeDtypeStruct((m, IDX_DIM), F32),
                   jax.ShapeDtypeStruct((m, LANE), F32)],
        grid=(m // tm,),
        in_specs=[pl.BlockSpec((tm, k), row),
                  pl.BlockSpec((1, k), lambda i: (0, 0)),
                  pl.BlockSpec((k, n), lambda i: (0, 0))],
        out_specs=[pl.BlockSpec((tm, n), row), kv_spec, kv_spec,
                   pl.BlockSpec((tm, IDX_DIM), row), pl.BlockSpec((tm, LANE), row)],
        compiler_params=_params("parallel"),
        name="attn_proj",
    )(x, g, w)


def _tail_kernel(a_ref, wo_ref, r_ref, g_ref, wu_ref, wd_ref, gf_ref, o_ref, *, tf, final_norm):
    x1 = r_ref[...] + jnp.dot(a_ref[...], wo_ref[...], preferred_element_type=F32)
    xn = _rms(x1, g_ref[...]).astype(BF16)
    acc = x1
    for f0 in range(0, wu_ref.shape[1], tf):
        h = jnp.maximum(jnp.dot(xn, wu_ref[:, f0:f0 + tf], preferred_element_type=F32), 0.0)
        acc = acc + jnp.dot((h * h).astype(BF16), wd_ref[f0:f0 + tf, :], preferred_element_type=F32)
    if final_norm:
        acc = _rms(acc, gf_ref[...])
    o_ref[...] = acc


def _layer_tail(a, wo, resid, g, wu, wd, gf, *, tm, tf, final_norm):
    m, ka = a.shape
    d = wo.shape[1]
    ff = wu.shape[1]
    row = lambda i: (i, 0)
    whole = lambda shape: pl.BlockSpec(shape, lambda i: (0, 0), pipeline_mode=pl.Buffered(1))
    return pl.pallas_call(
        functools.partial(_tail_kernel, tf=tf, final_norm=final_norm),
        out_shape=jax.ShapeDtypeStruct((m, d), F32),
        grid=(m // tm,),
        in_specs=[pl.BlockSpec((tm, ka), row), whole((ka, d)), pl.BlockSpec((tm, d), row),
                  whole((1, d)), whole((d, ff)), whole((ff, d)), whole((1, d))],
        out_specs=pl.BlockSpec((tm, d), row),
        compiler_params=_params("parallel"),
        name="layer_tail",
    )(a, wo, resid, g, wu, wd, gf)


def _attn_kernel(*refs, tq, t, past, front_pad, topk):
    if past:
        (q_ref, qi_ref, kiwi_ref, kn_ref, vn_ref, kin_ref, pk_ref, pv_ref, pki_ref, nb_ref, o_ref,
         k_ref, v_ref, ki_ref, score_ref, qs_ref, qis_ref, wcol_ref, madd_ref, s_ref, m_ref, acc_ref) = refs
    else:
        (q_ref, qi_ref, kiwi_ref, kn_ref, vn_ref, kin_ref, nb_ref, o_ref,
         k_ref, v_ref, ki_ref, score_ref, qs_ref, qis_ref, wcol_ref, madd_ref, s_ref, m_ref, acc_ref) = refs
    w = KEY_WINDOW
    rows = GQ * tq
    rb = s_ref.shape[1]
    mrows = madd_ref.shape[0]
    keys_on_rows = score_ref.shape[1] == w
    kax = 0 if keys_on_rows else 1
    qshape = (1, tq) if keys_on_rows else (tq, 1)
    i = pl.program_id(1)
    q0 = past + i * tq
    lv = q0 + tq
    if t == tq:
        nw = (past + tq + w - 1) >> KEY_WINDOW_SHIFT
        n_all = float(nw * w)
        window_loop = functools.partial(lax.fori_loop, unroll=True)
    else:
        nw = jnp.right_shift(lv + (w - 1), KEY_WINDOW_SHIFT)
        n_all = (nw * w).astype(F32)
        window_loop = lax.fori_loop
    qchunk = jnp.right_shift(q0 + lax.broadcasted_iota(jnp.int32, qshape, 1 - kax), CHUNK_SHIFT)
    kidx = lax.broadcasted_iota(jnp.int32, (w, 1) if keys_on_rows else (1, w), kax)
    nt = (((1,), (1,)), ((), ()))

    def win_row(j):
        return pl.multiple_of(lv + front_pad - w * (j + 1), tq)

    @pl.when(i == 0)
    def _():
        ones_col = jnp.where(lax.broadcasted_iota(jnp.int32, (1, LANE - HEAD_DIM), 1) == 0, 1.0, 0.0)

        def put(r0, n, kc, vc, kic):
            ones = jnp.broadcast_to(ones_col, (n, LANE - HEAD_DIM)).astype(BF16)
            for g in range(N_KV_HEADS):
                gs = slice(g * HEAD_DIM, (g + 1) * HEAD_DIM)
                k_ref[g, r0:r0 + n, :] = kc[:, gs]
                v_ref[g, r0:r0 + n, :] = jnp.concatenate([vc[:, gs], ones], axis=1)
            ki_ref[r0:r0 + n, :] = kic

        kvw = N_KV_HEADS * HEAD_DIM
        if front_pad:
            put(0, front_pad, jnp.zeros((front_pad, kvw), BF16), jnp.zeros((front_pad, kvw), BF16),
                jnp.zeros((front_pad, IDX_DIM), BF16))
        step = min(t, w)
        for c0 in range(0, past, w):
            put(front_pad + c0, w, pk_ref[c0:c0 + w, :].astype(BF16), pv_ref[c0:c0 + w, :].astype(BF16),
                pki_ref[c0:c0 + w, :].astype(BF16))
        for c0 in range(0, t, step):
            put(front_pad + past + c0, step, kn_ref[c0:c0 + step, :], vn_ref[c0:c0 + step, :],
                kin_ref[c0:c0 + step, 0:IDX_DIM])

    for h in range(N_HEADS):
        g, jj = divmod(h, GQ)
        qs_ref[g, jj * tq:(jj + 1) * tq, :] = q_ref[:, h * HEAD_DIM:(h + 1) * HEAD_DIM]

    if keys_on_rows:
        wi_t = kiwi_ref[...].T[IDX_DIM:IDX_DIM + IDX_HEADS, :]
        for hp in range(IDX_HEADS // 2):
            for u in range(2):
                h = 2 * hp + u
                qis_ref[hp, u * tq:(u + 1) * tq, :] = qi_ref[:, h * IDX_DIM:(h + 1) * IDX_DIM]
    else:
        wi = kiwi_ref[:, IDX_DIM:IDX_DIM + IDX_HEADS]
        for h in range(IDX_HEADS):
            qis_ref[0, h * tq:(h + 1) * tq, :] = qi_ref[:, h * IDX_DIM:(h + 1) * IDX_DIM]
            wcol_ref[h * tq:(h + 1) * tq, :] = wi[:, h:h + 1]

    def visible_scores(sc, kpos):
        vis = (kpos >= 0) & (jnp.right_shift(kpos, CHUNK_SHIFT) <= qchunk)
        return jnp.where(vis, sc, -jnp.inf)

    def score_body(j, carry):
        if keys_on_rows:
            for k0 in range(0, w, w // 2):
                kib = ki_ref[pl.ds(win_row(j) + k0, w // 2), :]
                sc = jnp.zeros((w // 2, tq), F32)
                for hp in range(IDX_HEADS // 2):
                    s2 = jnp.maximum(lax.dot_general(kib, qis_ref[hp], nt, preferred_element_type=F32), 0.0)
                    sc = sc + s2[:, 0:tq] * wi_t[2 * hp:2 * hp + 1, :]
                    sc = sc + s2[:, tq:2 * tq] * wi_t[2 * hp + 1:2 * hp + 2, :]
                kpos = (lv - w * (j + 1)) + kidx[k0:k0 + w // 2]
                score_ref[j, k0:k0 + w // 2, :] = visible_scores(sc * IDX_SCALE, kpos)
        else:
            kib = ki_ref[pl.ds(win_row(j), w), :]
            s8 = jnp.maximum(lax.dot_general(qis_ref[0], kib, nt, preferred_element_type=F32), 0.0)
            s8 = s8 * wcol_ref[...]
            sc = s8[0:tq]
            for h in range(1, IDX_HEADS):
                sc = sc + s8[h * tq:(h + 1) * tq]
            score_ref[j] = visible_scores(sc * IDX_SCALE, (lv - w * (j + 1)) + kidx)
        return carry

    window_loop(0, nw, score_body, 0)

    def key_to_f32(key):
        return pltpu.bitcast(key ^ (jnp.right_shift(key, 31) & 0x7FFFFFFF), F32)

    if keys_on_rows:
        part = (2 * 8, tq)

        def fold(c):
            return jnp.sum(c.reshape(w // part[0], part[0], tq), axis=0)
    else:
        part = (tq, LANE)

        def fold(c):
            return (c[:, 0:LANE] + c[:, LANE:2 * LANE]) + (c[:, 2 * LANE:3 * LANE] + c[:, 3 * LANE:4 * LANE])

    def total(acc):
        return jnp.sum(acc, axis=kax, keepdims=True)

    def count_ge(cand):
        cand_f = key_to_f32(jnp.where((cand > 0) & (cand < MIN_NORMAL_KEY), MIN_NORMAL_KEY, cand))

        def body(j, acc):
            return acc + fold(jnp.where(score_ref[j] >= cand_f, 1.0, 0.0))
        return total(window_loop(0, nw, body, jnp.zeros(part, F32)))

    def step(cand, carry):
        prefix, n_lo, n_hi = carry
        c = count_ge(cand)
        take = c >= float(topk)
        return jnp.where(take, cand, prefix), jnp.where(take, c, n_lo), jnp.where(take, n_hi, c)

    def search_body(bit, carry):
        return step(carry[0] + lax.shift_left(jnp.int32(1), 30 - bit), carry)

    start = (jnp.full(qshape, INT_MIN, jnp.int32), jnp.full(qshape, 1.0, F32) * n_all, jnp.zeros(qshape, F32))
    sign = step(jnp.zeros(qshape, jnp.int32), start)
    tstar, n_ge, n_gt = lax.fori_loop(0, 31, search_body, sign)

    need = float(topk) - n_gt
    n_eq = n_ge - n_gt
    bad = (n_eq > need) & (tstar > KEY_NEG_INF)
    n_bad = jnp.sum(jnp.where(bad, 1.0, 0.0))
    tstar_f = key_to_f32(tstar)

    @pl.when(n_bad > 0.0)
    def _():
        r_i = lax.broadcasted_iota(jnp.int32, (w, w), 0)
        c_i = lax.broadcasted_iota(jnp.int32, (w, w), 1)
        earlier = jnp.where(c_i < r_i if keys_on_rows else r_i < c_i, 1.0, 0.0).astype(BF16)

        def tie_body(t, seen):
            j = nw - 1 - t
            kt = score_ref[j]
            eq = kt == tstar_f
            eqf = jnp.where(eq, 1.0, 0.0)
            if keys_on_rows:
                inwin = jnp.dot(earlier, eqf.astype(BF16), preferred_element_type=F32)
            else:
                inwin = jnp.dot(eqf.astype(BF16), earlier, preferred_element_type=F32)
            score_ref[j] = jnp.where(eq & bad & (seen + inwin >= need), -jnp.inf, kt)
            return seen + total(eqf)

        lax.fori_loop(0, nw, tie_body, jnp.zeros(qshape, F32))

    teff = key_to_f32(jnp.maximum(tstar, KEY_NEG_INF + 1))

    def attend(j, near):
        row = win_row(j)
        tile = jnp.where(score_ref[j] >= teff, 0.0, NEG_BIG)
        if keys_on_rows:
            tile = tile.T
        for r in range(mrows // tq):
            madd_ref[r * tq:(r + 1) * tq, :] = tile
        tiles = [(g, r0) for g in range(N_KV_HEADS) for r0 in range(0, rows, rb)]

        for t, (g, r0) in enumerate(tiles):
            kb = k_ref[g, pl.ds(row, w), :]
            s = lax.dot_general(qs_ref[g, r0:r0 + rb, :], kb, nt, preferred_element_type=F32)
            s = s + madd_ref[r0 % mrows:r0 % mrows + rb, :]
            if near:
                s = s + nb_ref[g, r0:r0 + rb, :]
            s_ref[t] = s
        for t, (g, r0) in enumerate(tiles):
            rs = slice(r0, r0 + rb)
            s = s_ref[t]
            m_new = jnp.broadcast_to(jnp.max(s, axis=1, keepdims=True), (rb, LANE))
            if not near:
                m_old = m_ref[g, rs, :]
                m_new = jnp.maximum(m_old, m_new)
            p = jnp.exp2(s - jnp.concatenate([m_new] * (w // LANE), axis=1))
            pv = jnp.dot(p.astype(BF16), v_ref[g, pl.ds(row, w), :], preferred_element_type=F32)
            acc_ref[g, rs, :] = pv if near else jnp.exp2(m_old - m_new) * acc_ref[g, rs, :] + pv
            m_ref[g, rs, :] = m_new

    attend(0, True)

    def far_body(j, carry):
        attend(j, False)
        return carry

    window_loop(1, nw, far_body, 0)

    for h in range(N_HEADS):
        g, jj = divmod(h, GQ)
        a = acc_ref[g, jj * tq:(jj + 1) * tq, :]
        o_ref[:, h * HEAD_DIM:(h + 1) * HEAD_DIM] = (
            a[:, 0:HEAD_DIM] / a[:, HEAD_DIM:HEAD_DIM + 1]).astype(o_ref.dtype)


def _rel_bucket(rel):
    half = NUM_BUCKETS // 2
    max_exact = half // 2
    base = jnp.where(rel > 0, half, 0)
    n = jnp.abs(rel)
    nf = jnp.maximum(n, 1).astype(jnp.float32)
    large = max_exact + jnp.floor(jnp.log(nf / max_exact) / math.log(MAX_DISTANCE / max_exact)
                                  * (half - max_exact)).astype(jnp.int32)
    large = jnp.minimum(large, half - 1)
    return base + jnp.where(n < max_exact, n, large)


def _near_bias(rel_bias, tq):
    r = jnp.arange(tq, dtype=jnp.int32)[:, None]
    c = jnp.arange(KEY_WINDOW, dtype=jnp.int32)[None, :]
    rel = (tq - KEY_WINDOW + c) - r
    tab = rel_bias.astype(F32)

    def pick(bucket):
        bucket = jnp.bitwise_and(bucket, NUM_BUCKETS - 1)
        d = bucket[..., None].astype(F32) - jnp.arange(NUM_BUCKETS, dtype=F32)
        return jnp.maximum(1.0 - jnp.abs(d), 0.0)

    far = _rel_bucket(jnp.full((1,), -(1 << 20), jnp.int32))
    far_row = jnp.dot(pick(far), tab, precision=lax.Precision.HIGHEST)
    nb = jnp.einsum('qcb,bh->hqc', pick(_rel_bucket(rel)), (tab - far_row) * LOG2E,
                    precision=lax.Precision.HIGHEST)
    return nb.reshape(N_KV_HEADS, GQ * tq, KEY_WINDOW)


def _dsa_mixer(projb, kiwi, past_k, past_v, past_ik, rel_bias, b, t):
    past = past_k.shape[1]
    n_keys = past + t
    topk = min(TOPK_MAX, n_keys // 4)
    tq = QUERY_ROWS if t % QUERY_ROWS == 0 else t
    assert KEY_WINDOW % tq == 0 and past % KEY_WINDOW == 0 and tq % 16 == 0 and n_keys >= KEY_WINDOW
    assert t % min(t, KEY_WINDOW) == 0
    front_pad = KEY_WINDOW - tq
    lp = front_pad + n_keys
    nq = t // tq
    nw_max = (n_keys + KEY_WINDOW - 1) // KEY_WINDOW
    rows = GQ * tq
    rb = min(rows, MAX_TILE_ROWS)
    keys_on_rows = tq % LANE == 0
    assert keys_on_rows or IDX_HEADS * tq <= LANE
    kvw = N_KV_HEADS * HEAD_DIM
    qw = N_HEADS * HEAD_DIM
    qiw = IDX_HEADS * IDX_DIM

    qrow = lambda bi, i: (bi * nq + i, 0)
    stream_buf = pl.Buffered(1 if nq > 1 else 2)
    in_specs = [pl.BlockSpec((tq, qw), qrow),
                pl.BlockSpec((tq, qiw), lambda bi, i: (bi * nq + i, QI_OFF // qiw)),
                pl.BlockSpec((tq, LANE), qrow),
                pl.BlockSpec((t, kvw), lambda bi, i: (bi, K_OFF // kvw), pipeline_mode=stream_buf),
                pl.BlockSpec((t, kvw), lambda bi, i: (bi, V_OFF // kvw), pipeline_mode=stream_buf),
                pl.BlockSpec((t, LANE), lambda bi, i: (bi, KI_OFF // LANE), pipeline_mode=stream_buf)]
    args = [projb, projb, kiwi, projb, projb, projb]
    if past:
        in_specs += [pl.BlockSpec((None, past, kvw), lambda bi, i: (bi, 0, 0)),
                     pl.BlockSpec((None, past, kvw), lambda bi, i: (bi, 0, 0)),
                     pl.BlockSpec((None, past, IDX_DIM), lambda bi, i: (bi, 0, 0))]
        args += [past_k.reshape(b, past, kvw), past_v.reshape(b, past, kvw), past_ik]
    in_specs.append(pl.BlockSpec((N_KV_HEADS, rows, KEY_WINDOW), lambda bi, i: (0, 0, 0),
                                 pipeline_mode=pl.Buffered(1)))
    args.append(_near_bias(rel_bias, tq))

    kern = functools.partial(_attn_kernel, tq=tq, t=t, past=past, front_pad=front_pad, topk=topk)
    return pl.pallas_call(
        kern,
        out_shape=jax.ShapeDtypeStruct((b * t, qw), BF16),
        grid=(b, nq),
        in_specs=in_specs,
        out_specs=pl.BlockSpec((tq, qw), qrow),
        scratch_shapes=[pltpu.VMEM((N_KV_HEADS, lp, HEAD_DIM), BF16),
                        pltpu.VMEM((N_KV_HEADS, lp, LANE), BF16),
                        pltpu.VMEM((lp, IDX_DIM), BF16),
                        pltpu.VMEM((nw_max, KEY_WINDOW, tq) if keys_on_rows
                                   else (nw_max, tq, KEY_WINDOW), F32),
                        pltpu.VMEM((N_KV_HEADS, rows, HEAD_DIM), BF16),
                        pltpu.VMEM((IDX_HEADS // 2, 2 * tq, IDX_DIM) if keys_on_rows
                                   else (1, IDX_HEADS * tq, IDX_DIM), BF16),
                        pltpu.VMEM((8, 1) if keys_on_rows else (IDX_HEADS * tq, 1), F32),
                        pltpu.VMEM((max(rb, tq), KEY_WINDOW), F32),
                        pltpu.VMEM((N_KV_HEADS * rows // rb, rb, KEY_WINDOW), F32),
                        pltpu.VMEM((N_KV_HEADS, rows, LANE), F32),
                        pltpu.VMEM((N_KV_HEADS, rows, LANE), F32)],
        compiler_params=_params("arbitrary", "arbitrary"),
        name="dsa_attention",
    )(*args)


def _ssd_kernel(z_ref, xlo_ref, xhi_ref, dt_ref, cst_ref, st0_ref, cw_ref, cb_ref, dtb_ref, alog_ref,
                dexp_ref, nw_ref, expand_ref, y_ref, ncv_ref, stout_ref,
                xpad_ref, act_ref, yd_ref, st_ref, *, q):
    c = pl.program_id(1)
    half = CONV_DIM // 2

    @pl.when(c == 0)
    def _():
        xpad_ref[8 - (D_CONV - 1):8, :] = cst_ref[...]
        for g in range(SSM_GROUPS):
            hs = slice(g * HEADS_PER_GROUP, (g + 1) * HEADS_PER_GROUP)
            st_ref[g] = st0_ref[hs].reshape(GROUP_W, D_STATE).T

    cc = CONV_COLS
    for c0 in range(0, CONV_DIM, cc):
        cs = slice(c0, c0 + cc)
        src = xlo_ref[:, c0:c0 + cc] if c0 < half else xhi_ref[:, c0 - half:c0 - half + cc]
        xpad_ref[8:8 + q, cs] = src
        conv = cb_ref[:, cs] + xpad_ref[5:5 + q, cs] * cw_ref[0:1, cs]
        conv = conv + xpad_ref[6:6 + q, cs] * cw_ref[1:2, cs]
        conv = conv + xpad_ref[7:7 + q, cs] * cw_ref[2:3, cs]
        conv = conv + xpad_ref[8:8 + q, cs] * cw_ref[3:4, cs]
        act_ref[:, cs] = _silu(conv)
        xpad_ref[0:8, cs] = xpad_ref[q:q + 8, cs]

    hi = lax.Precision.HIGHEST
    x_dt = dt_ref[:, 0:SSM_HEADS] + dtb_ref[...]
    dt = jnp.maximum(x_dt, 0.0) + jnp.log1p(jnp.exp(-jnp.abs(x_dt)))
    a = dt * (-jnp.exp(alog_ref[...]))
    rr = lax.broadcasted_iota(jnp.int32, (q, q), 0)
    cl = lax.broadcasted_iota(jnp.int32, (q, q), 1)
    causal = rr >= cl
    tri = jnp.where(causal, 1.0, 0.0)
    a_cs = jnp.dot(tri, a, precision=hi, preferred_element_type=F32)
    eye = jnp.where(lax.broadcasted_iota(jnp.int32, (SSM_HEADS, SSM_HEADS), 0)
                    == lax.broadcasted_iota(jnp.int32, (SSM_HEADS, SSM_HEADS), 1), 1.0, 0.0)
    a_cs_t = lax.dot_general(eye, a_cs, (((1,), (1,)), ((), ())), precision=hi,
                             preferred_element_type=F32)
    a_end = a_cs[q - 1:q, :]
    per_head = jnp.concatenate([dt, jnp.exp(a_cs), jnp.exp(a_end - a_cs)], axis=0)
    top = per_head.astype(BF16)
    rest = (per_head - top.astype(F32)).astype(BF16)
    spread = jnp.dot(jnp.concatenate([top, rest], axis=0), expand_ref[...],
                     preferred_element_type=F32)
    dt_x = spread[0:q] + spread[3 * q:4 * q]
    e_x = spread[q:2 * q] + spread[4 * q:5 * q]
    dte_x = spread[2 * q:3 * q] + spread[5 * q:6 * q]
    dec_x = e_x[q - 1:q, :]

    for g in range(SSM_GROUPS):
        gs = slice(g * GROUP_W, (g + 1) * GROUP_W)
        bg = act_ref[:, D_INNER + g * D_STATE:D_INNER + (g + 1) * D_STATE].astype(BF16)
        cg = act_ref[:, D_INNER + (SSM_GROUPS + g) * D_STATE:
                     D_INNER + (SSM_GROUPS + g + 1) * D_STATE].astype(BF16)
        cb = lax.dot_general(cg, bg, (((1,), (1,)), ((), ())), preferred_element_type=F32)
        xs_g = act_ref[:, gs]
        xd_g = xs_g * dt_x[:, gs]
        st_g = st_ref[g]
        y_off = jnp.dot(cg, st_g.astype(BF16), preferred_element_type=F32) * e_x[:, gs]
        for jj in range(HEADS_PER_GROUP):
            h = g * HEADS_PER_GROUP + jj
            seg = a_cs[:, h:h + 1] - a_cs_t[h:h + 1, :]
            decay = jnp.where(causal, jnp.exp(seg), 0.0)
            hs = slice(jj * SSM_HEAD_DIM, (jj + 1) * SSM_HEAD_DIM)
            yd_ref[:, hs] = jnp.dot((cb * decay).astype(BF16), xd_g[:, hs].astype(BF16),
                                    preferred_element_type=F32)
        st_ref[g] = st_g * dec_x[:, gs] + lax.dot_general(
            bg, (xd_g * dte_x[:, gs]).astype(BF16), (((0,), (0,)), ((), ())),
            preferred_element_type=F32)
        y = yd_ref[...] + y_off + xs_g * dexp_ref[:, gs]
        zg = z_ref[:, gs]
        yg = y * _silu(zg)
        ms = jnp.mean(yg * yg, axis=-1, keepdims=True)
        y_ref[:, gs] = (yg * lax.rsqrt(ms + EPS) * nw_ref[:, gs]).astype(y_ref.dtype)

    @pl.when(c == pl.num_programs(1) - 1)
    def _():
        ncv_ref[...] = xpad_ref[8 - (D_CONV - 1):8, :]
        for g in range(SSM_GROUPS):
            hs = slice(g * HEADS_PER_GROUP, (g + 1) * HEADS_PER_GROUP)
            stout_ref[hs] = st_ref[g].T.reshape(HEADS_PER_GROUP, SSM_HEAD_DIM, D_STATE)


def _ssd_mixer(proj, conv_state, ssm_state, conv_w, conv_b, dt_bias, a_log, d_skip, norm_w, b, t):
    q = 128 if t % 128 == 0 else t
    assert q % 8 == 0 and t >= D_CONV - 1 and CONV_DIM == 2 * D_INNER
    nc = t // q
    expand = jnp.repeat(jnp.eye(SSM_HEADS, dtype=BF16), SSM_HEAD_DIM, axis=1)
    dexp = jnp.repeat(d_skip.astype(F32), SSM_HEAD_DIM)[None, :]
    row = lambda k: (lambda bi, c: (bi * nc + c, k))
    const2 = lambda bi, c: (0, 0)
    state_spec = pl.BlockSpec((None, SSM_HEADS, SSM_HEAD_DIM, D_STATE), lambda bi, c: (bi, 0, 0, 0))
    conv_spec = pl.BlockSpec((None, D_CONV - 1, CONV_DIM), lambda bi, c: (bi, 0, 0))
    y, new_conv, new_ssm = pl.pallas_call(
        functools.partial(_ssd_kernel, q=q),
        out_shape=[jax.ShapeDtypeStruct((b * t, D_INNER), BF16),
                   jax.ShapeDtypeStruct((b, D_CONV - 1, CONV_DIM), F32),
                   jax.ShapeDtypeStruct((b, SSM_HEADS, SSM_HEAD_DIM, D_STATE), F32)],
        grid=(b, nc),
        in_specs=[pl.BlockSpec((q, D_INNER), row(SSM_Z_OFF // D_INNER)),
                  pl.BlockSpec((q, D_INNER), row(SSM_XBC_OFF // D_INNER)),
                  pl.BlockSpec((q, D_INNER), row(SSM_XBC_OFF // D_INNER + 1)),
                  pl.BlockSpec((q, LANE), row(SSM_DT_OFF // LANE)),
                  conv_spec,
                  state_spec,
                  pl.BlockSpec((D_CONV, CONV_DIM), const2),
                  pl.BlockSpec((1, CONV_DIM), const2),
                  pl.BlockSpec((1, SSM_HEADS), const2),
                  pl.BlockSpec((1, SSM_HEADS), const2),
                  pl.BlockSpec((1, D_INNER), const2),
                  pl.BlockSpec((1, D_INNER), const2),
                  pl.BlockSpec((SSM_HEADS, D_INNER), const2)],
        out_specs=[pl.BlockSpec((q, D_INNER), row(0)), conv_spec, state_spec],
        scratch_shapes=[pltpu.VMEM((q + 8, CONV_DIM), F32),
                        pltpu.VMEM((q, CONV_DIM), F32),
                        pltpu.VMEM((q, GROUP_W), F32),
                        pltpu.VMEM((SSM_GROUPS, D_STATE, GROUP_W), F32)],
        compiler_params=_params("arbitrary", "arbitrary"),
        name="ssd_mixer",
    )(proj, proj, proj, proj, conv_state.astype(F32), ssm_state.astype(F32), conv_w.astype(F32),
      conv_b.astype(F32)[None, :], dt_bias.astype(F32)[None, :], a_log.astype(F32)[None, :], dexp,
      norm_w.astype(F32)[None, :], expand)
    return y, new_conv, new_ssm.astype(ssm_state.dtype)


def _row_tile(m, cap):
    tm = cap
    while m % tm:
        tm //= 2
    assert tm >= 8, (m, cap)
    return tm


def _pad_cols(w, n):
    return jnp.pad(w, ((0, 0), (0, n - w.shape[1])))


def _prep_weights(norm_mix, norm_ffn, norm_final, attn_w_in, attn_w_o, ssm_w_in, ssm_w_out,
                  mlp_w_up, mlp_w_down):
    col_scale = jnp.where(jnp.arange(ATTN_COLS) < K_OFF, ATTN_SCALE * LOG2E, 1.0).astype(F32)
    return dict(
        attn_in=_pad_cols(attn_w_in[0] * col_scale, _round_up(ATTN_COLS, LANE)).astype(BF16),
        attn_o=attn_w_o[0].astype(BF16),
        ssm_in=ssm_w_in[0].astype(BF16),
        ssm_out=ssm_w_out[0].astype(BF16),
        up=[mlp_w_up[i].astype(BF16) for i in range(2)],
        down=[mlp_w_down[i].astype(BF16) for i in range(2)],
        g_mix=[norm_mix[i].astype(F32)[None, :] for i in range(2)],
        g_ffn=[norm_ffn[i].astype(F32)[None, :] for i in range(2)],
        g_final=norm_final.astype(F32)[None, :],
    )


def _trunk(x, past_k, past_v, past_ik, conv_st, ssm_st, rel_bias, wts, ssm_conv_w, ssm_conv_b,
           ssm_dt_bias, ssm_a_log, ssm_d, ssm_norm):
    b, t, d = x.shape
    m = b * t
    x2 = x.reshape(m, d).astype(F32)
    tail_rows = _row_tile(m, TAIL_ROWS)

    projb, k_new, v_new, ki_new, kiwi = _attn_proj(x2, wts["g_mix"][0], wts["attn_in"],
                                                   tm=_row_tile(m, PROJ_ROWS))
    ao = _dsa_mixer(projb, kiwi, past_k, past_v, past_ik, rel_bias, b, t)
    x2 = _layer_tail(ao, wts["attn_o"], x2, wts["g_ffn"][0], wts["up"][0], wts["down"][0],
                     wts["g_final"], tm=tail_rows, tf=MLP_CHUNK, final_norm=False)

    proj = _norm_matmul(x2, wts["g_mix"][1], wts["ssm_in"], tm=_row_tile(m, SSM_PROJ_ROWS),
                        tn=wts["ssm_in"].shape[1])
    y, new_conv, new_ssm = _ssd_mixer(proj, conv_st, ssm_st, ssm_conv_w, ssm_conv_b, ssm_dt_bias,
                                      ssm_a_log, ssm_d, ssm_norm, b, t)
    x2 = _layer_tail(y, wts["ssm_out"], x2, wts["g_ffn"][1], wts["up"][1], wts["down"][1],
                     wts["g_final"], tm=tail_rows, tf=MLP_CHUNK, final_norm=True)

    dt = x.dtype
    return (x2.reshape(b, t, d).astype(dt),
            k_new.reshape(1, b, t, N_KV_HEADS, HEAD_DIM).astype(dt),
            v_new.reshape(1, b, t, N_KV_HEADS, HEAD_DIM).astype(dt),
            ki_new.reshape(1, b, t, IDX_DIM).astype(dt), new_conv[None].astype(dt), new_ssm[None])


def kernel(x_prompt, x_sample, cache_k, cache_v, cache_idx_k, state_conv, state_ssm, rel_bias, norm_mix, norm_ffn, norm_final, attn_w_in, attn_w_o, ssm_w_in, ssm_conv_w, ssm_conv_b, ssm_dt_bias, ssm_a_log, ssm_d, ssm_norm, ssm_w_out, mlp_w_up, mlp_w_down):
    wts = _prep_weights(norm_mix, norm_ffn, norm_final, attn_w_in, attn_w_o, ssm_w_in, ssm_w_out,
                        mlp_w_up, mlp_w_down)
    bp = x_prompt.shape[0]
    dtp = x_prompt.dtype
    empty_k = jnp.zeros((bp, 0, N_KV_HEADS, HEAD_DIM), dtp)
    empty_ik = jnp.zeros((bp, 0, IDX_DIM), dtp)
    zero_conv = jnp.zeros((bp, D_CONV - 1, CONV_DIM), dtp)
    zero_ssm = jnp.zeros((bp, SSM_HEADS, SSM_HEAD_DIM, D_STATE), dtp)
    args = (rel_bias, wts, ssm_conv_w[0], ssm_conv_b[0], ssm_dt_bias[0], ssm_a_log[0], ssm_d[0],
            ssm_norm[0])
    yp, kp, vp, ikp, cp, sp = _trunk(x_prompt, empty_k, empty_k, empty_ik, zero_conv, zero_ssm, *args)
    ys, ks, vs, iks, cs, ss = _trunk(x_sample, cache_k[0], cache_v[0], cache_idx_k[0],
                                     state_conv[0], state_ssm[0], *args)
    return (yp, ys, kp, vp, ikp, cp, sp, ks, vs, iks, cs, ss)
```

```python
import functools
import math

import jax
import jax.numpy as jnp
from jax import lax
from jax.experimental import pallas as pl
from jax.experimental.pallas import tpu as pltpu

F32 = jnp.float32
BF16 = jnp.bfloat16

D_MODEL = 1024
CHUNK_SHIFT = 6
N_HEADS = 16
HEAD_DIM = 64
N_KV_HEADS = 4
GQ = N_HEADS // N_KV_HEADS
IDX_HEADS = 8
IDX_DIM = 64
TOPK_MAX = 256
IDX_SCALE = (IDX_HEADS * IDX_DIM) ** -0.5
ATTN_SCALE = HEAD_DIM ** -0.5
NUM_BUCKETS = 32
MAX_DISTANCE = 128
D_INNER = 2 * D_MODEL
SSM_HEAD_DIM = 64
SSM_HEADS = D_INNER // SSM_HEAD_DIM
SSM_GROUPS = 8
HEADS_PER_GROUP = SSM_HEADS // SSM_GROUPS
GROUP_W = HEADS_PER_GROUP * SSM_HEAD_DIM
D_STATE = 128
D_CONV = 4
CONV_DIM = D_INNER + 2 * SSM_GROUPS * D_STATE
EPS = 1e-6

K_OFF = N_HEADS * HEAD_DIM
V_OFF = K_OFF + N_KV_HEADS * HEAD_DIM
QI_OFF = V_OFF + N_KV_HEADS * HEAD_DIM
KI_OFF = QI_OFF + IDX_HEADS * IDX_DIM
WI_OFF = KI_OFF + IDX_DIM
ATTN_COLS = WI_OFF + IDX_HEADS

SSM_Z_OFF = 0
SSM_XBC_OFF = D_INNER
SSM_DT_OFF = D_INNER + CONV_DIM

LANE = 128
KEY_WINDOW_SHIFT = 9
KEY_WINDOW = 1 << KEY_WINDOW_SHIFT
QUERY_ROWS = 256
MAX_TILE_ROWS = 1024
CONV_COLS = 512
PROJ_ROWS = 512
SSM_PROJ_ROWS = 512
TAIL_ROWS = 512
MLP_CHUNK = 1024
LOG2E = 1.4426950408889634
VMEM_LIMIT = 56 * 1024 * 1024

NEG_BIG = -1e30
KEY_NEG_INF = -2139095041
MIN_NORMAL_KEY = 1 << 23
INT_MIN = -2147483648


def _round_up(n, m):
    return (n + m - 1) // m * m


def _rms(x, g):
    ms = jnp.mean(x * x, axis=-1, keepdims=True)
    return x * lax.rsqrt(ms + EPS) * g


def _silu(x):
    h = 0.5 * x
    return h + h * jnp.tanh(h)


def _params(*sem):
    return pltpu.CompilerParams(dimension_semantics=sem, vmem_limit_bytes=VMEM_LIMIT)


def _norm_matmul_kernel(x_ref, g_ref, w_ref, o_ref, xn_ref):
    @pl.when(pl.program_id(1) == 0)
    def _():
        xn_ref[...] = _rms(x_ref[...], g_ref[...]).astype(BF16)

    o_ref[...] = jnp.dot(xn_ref[...], w_ref[...], preferred_element_type=F32)


def _norm_matmul(x, g, w, *, tm, tn):
    m, k = x.shape
    n = w.shape[1]
    return pl.pallas_call(
        _norm_matmul_kernel,
        out_shape=jax.ShapeDtypeStruct((m, n), F32),
        grid=(m // tm, n // tn),
        in_specs=[pl.BlockSpec((tm, k), lambda i, j: (i, 0)),
                  pl.BlockSpec((1, k), lambda i, j: (0, 0)),
                  pl.BlockSpec((k, tn), lambda i, j: (0, j), pipeline_mode=pl.Buffered(1 if tn == n else 2))],
        out_specs=pl.BlockSpec((tm, tn), lambda i, j: (i, j)),
        scratch_shapes=[pltpu.VMEM((tm, k), BF16)],
        compiler_params=_params("parallel", "arbitrary"),
        name="norm_matmul",
    )(x, g, w)


def _attn_proj_kernel(x_ref, g_ref, w_ref, pb_ref, k_ref, v_ref, ki_ref, kiwi_ref):
    xn = _rms(x_ref[...], g_ref[...]).astype(BF16)
    acc = jnp.dot(xn, w_ref[...], preferred_element_type=F32)
    pb_ref[...] = acc.astype(BF16)
    k_ref[...] = acc[:, K_OFF:V_OFF].reshape(k_ref.shape)
    v_ref[...] = acc[:, V_OFF:QI_OFF].reshape(v_ref.shape)
    ki_ref[...] = acc[:, KI_OFF:WI_OFF]
    kiwi_ref[...] = acc[:, KI_OFF:KI_OFF + LANE]


def _attn_proj(x, g, w, *, tm):
    m, k = x.shape
    n = w.shape[1]
    row = lambda i: (i, 0)
    kv_spec = pl.BlockSpec((tm, N_KV_HEADS, HEAD_DIM), lambda i: (i, 0, 0))
    return pl.pallas_call(
        _attn_proj_kernel,
        out_shape=[jax.ShapeDtypeStruct((m, n), BF16),
                   jax.ShapeDtypeStruct((m, N_KV_HEADS, HEAD_DIM), F32),
                   jax.ShapeDtypeStruct((m, N_KV_HEADS, HEAD_DIM), F32),
                   jax.ShapeDtypeStruct((m, IDX_DIM), F32),
                   jax.ShapeDtypeStruct((m, LANE), F32)],
        grid=(m // tm,),
        in_specs=[pl.BlockSpec((tm, k), row),
                  pl.BlockSpec((1, k), lambda i: (0, 0)),
                  pl.BlockSpec((k, n), lambda i: (0, 0))],
        out_specs=[pl.BlockSpec((tm, n), row), kv_spec, kv_spec,
                   pl.BlockSpec((tm, IDX_DIM), row), pl.BlockSpec((tm, LANE), row)],
        compiler_params=_params("parallel"),
        name="attn_proj",
    )(x, g, w)


def _tail_kernel(a_ref, wo_ref, r_ref, g_ref, wu_ref, wd_ref, gf_ref, o_ref, *, tf, final_norm):
    x1 = r_ref[...] + jnp.dot(a_ref[...], wo_ref[...], preferred_element_type=F32)
    xn = _rms(x1, g_ref[...]).astype(BF16)
    acc = x1
    for f0 in range(0, wu_ref.shape[1], tf):
        h = jnp.maximum(jnp.dot(xn, wu_ref[:, f0:f0 + tf], preferred_element_type=F32), 0.0)
        acc = acc + jnp.dot((h * h).astype(BF16), wd_ref[f0:f0 + tf, :], preferred_element_type=F32)
    if final_norm:
        acc = _rms(acc, gf_ref[...])
    o_ref[...] = acc


def _layer_tail(a, wo, resid, g, wu, wd, gf, *, tm, tf, final_norm):
    m, ka = a.shape
    d = wo.shape[1]
    ff = wu.shape[1]
    row = lambda i: (i, 0)
    whole = lambda shape: pl.BlockSpec(shape, lambda i: (0, 0), pipeline_mode=pl.Buffered(1))
    return pl.pallas_call(
        functools.partial(_tail_kernel, tf=tf, final_norm=final_norm),
        out_shape=jax.ShapeDtypeStruct((m, d), F32),
        grid=(m // tm,),
        in_specs=[pl.BlockSpec((tm, ka), row), whole((ka, d)), pl.BlockSpec((tm, d), row),
                  whole((1, d)), whole((d, ff)), whole((ff, d)), whole((1, d))],
        out_specs=pl.BlockSpec((tm, d), row),
        compiler_params=_params("parallel"),
        name="layer_tail",
    )(a, wo, resid, g, wu, wd, gf)


def _attn_kernel(*refs, tq, t, past, front_pad, topk):
    if past:
        (q_ref, qi_ref, kiwi_ref, kn_ref, vn_ref, kin_ref, pk_ref, pv_ref, pki_ref, nb_ref, o_ref,
         k_ref, v_ref, ki_ref, score_ref, qs_ref, qis_ref, wcol_ref, madd_ref, s_ref, m_ref, acc_ref) = refs
    else:
        (q_ref, qi_ref, kiwi_ref, kn_ref, vn_ref, kin_ref, nb_ref, o_ref,
         k_ref, v_ref, ki_ref, score_ref, qs_ref, qis_ref, wcol_ref, madd_ref, s_ref, m_ref, acc_ref) = refs
    w = KEY_WINDOW
    rows = GQ * tq
    rb = s_ref.shape[1]
    mrows = madd_ref.shape[0]
    keys_on_rows = score_ref.shape[1] == w
    kax = 0 if keys_on_rows else 1
    qshape = (1, tq) if keys_on_rows else (tq, 1)
    i = pl.program_id(1)
    q0 = past + i * tq
    lv = q0 + tq
    if t == tq:
        nw = (past + tq + w - 1) >> KEY_WINDOW_SHIFT
        n_all = float(nw * w)
        window_loop = functools.partial(lax.fori_loop, unroll=True)
    else:
        nw = jnp.right_shift(lv + (w - 1), KEY_WINDOW_SHIFT)
        n_all = (nw * w).astype(F32)
        window_loop = lax.fori_loop
    qchunk = jnp.right_shift(q0 + lax.broadcasted_iota(jnp.int32, qshape, 1 - kax), CHUNK_SHIFT)
    kidx = lax.broadcasted_iota(jnp.int32, (w, 1) if keys_on_rows else (1, w), kax)
    nt = (((1,), (1,)), ((), ()))

    def win_row(j):
        return pl.multiple_of(lv + front_pad - w * (j + 1), tq)

    @pl.when(i == 0)
    def _():
        ones_col = jnp.where(lax.broadcasted_iota(jnp.int32, (1, LANE - HEAD_DIM), 1) == 0, 1.0, 0.0)

        def put(r0, n, kc, vc, kic):
            ones = jnp.broadcast_to(ones_col, (n, LANE - HEAD_DIM)).astype(BF16)
            for g in range(N_KV_HEADS):
                gs = slice(g * HEAD_DIM, (g + 1) * HEAD_DIM)
                k_ref[g, r0:r0 + n, :] = kc[:, gs]
                v_ref[g, r0:r0 + n, :] = jnp.concatenate([vc[:, gs], ones], axis=1)
            ki_ref[r0:r0 + n, :] = kic

        kvw = N_KV_HEADS * HEAD_DIM
        if front_pad:
            put(0, front_pad, jnp.zeros((front_pad, kvw), BF16), jnp.zeros((front_pad, kvw), BF16),
                jnp.zeros((front_pad, IDX_DIM), BF16))
        step = min(t, w)
        for c0 in range(0, past, w):
            put(front_pad + c0, w, pk_ref[c0:c0 + w, :].astype(BF16), pv_ref[c0:c0 + w, :].astype(BF16),
                pki_ref[c0:c0 + w, :].astype(BF16))
        for c0 in range(0, t, step):
            put(front_pad + past + c0, step, kn_ref[c0:c0 + step, :], vn_ref[c0:c0 + step, :],
                kin_ref[c0:c0 + step, 0:IDX_DIM])

    for h in range(N_HEADS):
        g, jj = divmod(h, GQ)
        qs_ref[g, jj * tq:(jj + 1) * tq, :] = q_ref[:, h * HEAD_DIM:(h + 1) * HEAD_DIM]

    if keys_on_rows:
        wi_t = kiwi_ref[...].T[IDX_DIM:IDX_DIM + IDX_HEADS, :]
        for hp in range(IDX_HEADS // 2):
            for u in range(2):
                h = 2 * hp + u
                qis_ref[hp, u * tq:(u + 1) * tq, :] = qi_ref[:, h * IDX_DIM:(h + 1) * IDX_DIM]
    else:
        wi = kiwi_ref[:, IDX_DIM:IDX_DIM + IDX_HEADS]
        for h in range(IDX_HEADS):
            qis_ref[0, h * tq:(h + 1) * tq, :] = qi_ref[:, h * IDX_DIM:(h + 1) * IDX_DIM]
            wcol_ref[h * tq:(h + 1) * tq, :] = wi[:, h:h + 1]

    def visible_scores(sc, kpos):
        vis = (kpos >= 0) & (jnp.right_shift(kpos, CHUNK_SHIFT) <= qchunk)
        return jnp.where(vis, sc, -jnp.inf)

    def score_body(j, carry):
        if keys_on_rows:
            for k0 in range(0, w, w // 2):
                kib = ki_ref[pl.ds(win_row(j) + k0, w // 2), :]
                sc = jnp.zeros((w // 2, tq), F32)
                for hp in range(IDX_HEADS // 2):
                    s2 = jnp.maximum(lax.dot_general(kib, qis_ref[hp], nt, preferred_element_type=F32), 0.0)
                    sc = sc + s2[:, 0:tq] * wi_t[2 * hp:2 * hp + 1, :]
                    sc = sc + s2[:, tq:2 * tq] * wi_t[2 * hp + 1:2 * hp + 2, :]
                kpos = (lv - w * (j + 1)) + kidx[k0:k0 + w // 2]
                score_ref[j, k0:k0 + w // 2, :] = visible_scores(sc * IDX_SCALE, kpos)
        else:
            kib = ki_ref[pl.ds(win_row(j), w), :]
            s8 = jnp.maximum(lax.dot_general(qis_ref[0], kib, nt, preferred_element_type=F32), 0.0)
            s8 = s8 * wcol_ref[...]
            sc = s8[0:tq]
            for h in range(1, IDX_HEADS):
                sc = sc + s8[h * tq:(h + 1) * tq]
            score_ref[j] = visible_scores(sc * IDX_SCALE, (lv - w * (j + 1)) + kidx)
        return carry

    window_loop(0, nw, score_body, 0)

    def key_to_f32(key):
        return pltpu.bitcast(key ^ (jnp.right_shift(key, 31) & 0x7FFFFFFF), F32)

    if keys_on_rows:
        part = (2 * 8, tq)

        def fold(c):
            return jnp.sum(c.reshape(w // part[0], part[0], tq), axis=0)
    else:
        part = (tq, LANE)

        def fold(c):
            return (c[:, 0:LANE] + c[:, LANE:2 * LANE]) + (c[:, 2 * LANE:3 * LANE] + c[:, 3 * LANE:4 * LANE])

    def total(acc):
        return jnp.sum(acc, axis=kax, keepdims=True)

    def count_ge(cand):
        cand_f = key_to_f32(jnp.where((cand > 0) & (cand < MIN_NORMAL_KEY), MIN_NORMAL_KEY, cand))

        def body(j, acc):
            return acc + fold(jnp.where(score_ref[j] >= cand_f, 1.0, 0.0))
        return total(window_loop(0, nw, body, jnp.zeros(part, F32)))

    def step(cand, carry):
        prefix, n_lo, n_hi = carry
        c = count_ge(cand)
        take = c >= float(topk)
        return jnp.where(take, cand, prefix), jnp.where(take, c, n_lo), jnp.where(take, n_hi, c)

    def search_body(bit, carry):
        return step(carry[0] + lax.shift_left(jnp.int32(1), 30 - bit), carry)

    start = (jnp.full(qshape, INT_MIN, jnp.int32), jnp.full(qshape, 1.0, F32) * n_all, jnp.zeros(qshape, F32))
    sign = step(jnp.zeros(qshape, jnp.int32), start)
    tstar, n_ge, n_gt = lax.fori_loop(0, 31, search_body, sign)

    need = float(topk) - n_gt
    n_eq = n_ge - n_gt
    bad = (n_eq > need) & (tstar > KEY_NEG_INF)
    n_bad = jnp.sum(jnp.where(bad, 1.0, 0.0))
    tstar_f = key_to_f32(tstar)

    @pl.when(n_bad > 0.0)
    def _():
        r_i = lax.broadcasted_iota(jnp.int32, (w, w), 0)
        c_i = lax.broadcasted_iota(jnp.int32, (w, w), 1)
        earlier = jnp.where(c_i < r_i if keys_on_rows else r_i < c_i, 1.0, 0.0).astype(BF16)

        def tie_body(t, seen):
            j = nw - 1 - t
            kt = score_ref[j]
            eq = kt == tstar_f
            eqf = jnp.where(eq, 1.0, 0.0)
            if keys_on_rows:
                inwin = jnp.dot(earlier, eqf.astype(BF16), preferred_element_type=F32)
            else:
                inwin = jnp.dot(eqf.astype(BF16), earlier, preferred_element_type=F32)
            score_ref[j] = jnp.where(eq & bad & (seen + inwin >= need), -jnp.inf, kt)
            return seen + total(eqf)

        lax.fori_loop(0, nw, tie_body, jnp.zeros(qshape, F32))

    teff = key_to_f32(jnp.maximum(tstar, KEY_NEG_INF + 1))

    def attend(j, near):
        row = win_row(j)
        tile = jnp.where(score_ref[j] >= teff, 0.0, NEG_BIG)
        if keys_on_rows:
            tile = tile.T
        for r in range(mrows // tq):
            madd_ref[r * tq:(r + 1) * tq, :] = tile
        tiles = [(g, r0) for g in range(N_KV_HEADS) for r0 in range(0, rows, rb)]

        for t, (g, r0) in enumerate(tiles):
            kb = k_ref[g, pl.ds(row, w), :]
            s = lax.dot_general(qs_ref[g, r0:r0 + rb, :], kb, nt, preferred_element_type=F32)
            s = s + madd_ref[r0 % mrows:r0 % mrows + rb, :]
            if near:
                s = s + nb_ref[g, r0:r0 + rb, :]
            s_ref[t] = s
        for t, (g, r0) in enumerate(tiles):
            rs = slice(r0, r0 + rb)
            s = s_ref[t]
            m_new = jnp.broadcast_to(jnp.max(s, axis=1, keepdims=True), (rb, LANE))
            if not near:
                m_old = m_ref[g, rs, :]
                m_new = jnp.maximum(m_old, m_new)
            p = jnp.exp2(s - jnp.concatenate([m_new] * (w // LANE), axis=1))
            pv = jnp.dot(p.astype(BF16), v_ref[g, pl.ds(row, w), :], preferred_element_type=F32)
            acc_ref[g, rs, :] = pv if near else jnp.exp2(m_old - m_new) * acc_ref[g, rs, :] + pv
            m_ref[g, rs, :] = m_new

    attend(0, True)

    def far_body(j, carry):
        attend(j, False)
        return carry

    window_loop(1, nw, far_body, 0)

    for h in range(N_HEADS):
        g, jj = divmod(h, GQ)
        a = acc_ref[g, jj * tq:(jj + 1) * tq, :]
        o_ref[:, h * HEAD_DIM:(h + 1) * HEAD_DIM] = (
            a[:, 0:HEAD_DIM] / a[:, HEAD_DIM:HEAD_DIM + 1]).astype(o_ref.dtype)


def _rel_bucket(rel):
    half = NUM_BUCKETS // 2
    max_exact = half // 2
    base = jnp.where(rel > 0, half, 0)
    n = jnp.abs(rel)
    nf = jnp.maximum(n, 1).astype(jnp.float32)
    large = max_exact + jnp.floor(jnp.log(nf / max_exact) / math.log(MAX_DISTANCE / max_exact)
                                  * (half - max_exact)).astype(jnp.int32)
    large = jnp.minimum(large, half - 1)
    return base + jnp.where(n < max_exact, n, large)


def _near_bias(rel_bias, tq):
    r = jnp.arange(tq, dtype=jnp.int32)[:, None]
    c = jnp.arange(KEY_WINDOW, dtype=jnp.int32)[None, :]
    rel = (tq - KEY_WINDOW + c) - r
    tab = rel_bias.astype(F32)

    def pick(bucket):
        bucket = jnp.bitwise_and(bucket, NUM_BUCKETS - 1)
        d = bucket[..., None].astype(F32) - jnp.arange(NUM_BUCKETS, dtype=F32)
        return jnp.maximum(1.0 - jnp.abs(d), 0.0)

    far = _rel_bucket(jnp.full((1,), -(1 << 20), jnp.int32))
    far_row = jnp.dot(pick(far), tab, precision=lax.Precision.HIGHEST)
    nb = jnp.einsum('qcb,bh->hqc', pick(_rel_bucket(rel)), (tab - far_row) * LOG2E,
                    precision=lax.Precision.HIGHEST)
    return nb.reshape(N_KV_HEADS, GQ * tq, KEY_WINDOW)


def _dsa_mixer(projb, kiwi, past_k, past_v, past_ik, rel_bias, b, t):
    past = past_k.shape[1]
    n_keys = past + t
    topk = min(TOPK_MAX, n_keys // 4)
    tq = QUERY_ROWS if t % QUERY_ROWS == 0 else t
    assert KEY_WINDOW % tq == 0 and past % KEY_WINDOW == 0 and tq % 16 == 0 and n_keys >= KEY_WINDOW
    assert t % min(t, KEY_WINDOW) == 0
    front_pad = KEY_WINDOW - tq
    lp = front_pad + n_keys
    nq = t // tq
    nw_max = (n_keys + KEY_WINDOW - 1) // KEY_WINDOW
    rows = GQ * tq
    rb = min(rows, MAX_TILE_ROWS)
    keys_on_rows = tq % LANE == 0
    assert keys_on_rows or IDX_HEADS * tq <= LANE
    kvw = N_KV_HEADS * HEAD_DIM
    qw = N_HEADS * HEAD_DIM
    qiw = IDX_HEADS * IDX_DIM

    qrow = lambda bi, i: (bi * nq + i, 0)
    stream_buf = pl.Buffered(1 if nq > 1 else 2)
    in_specs = [pl.BlockSpec((tq, qw), qrow),
                pl.BlockSpec((tq, qiw), lambda bi, i: (bi * nq + i, QI_OFF // qiw)),
                pl.BlockSpec((tq, LANE), qrow),
                pl.BlockSpec((t, kvw), lambda bi, i: (bi, K_OFF // kvw), pipeline_mode=stream_buf),
                pl.BlockSpec((t, kvw), lambda bi, i: (bi, V_OFF // kvw), pipeline_mode=stream_buf),
                pl.BlockSpec((t, LANE), lambda bi, i: (bi, KI_OFF // LANE), pipeline_mode=stream_buf)]
    args = [projb, projb, kiwi, projb, projb, projb]
    if past:
        in_specs += [pl.BlockSpec((None, past, kvw), lambda bi, i: (bi, 0, 0)),
                     pl.BlockSpec((None, past, kvw), lambda bi, i: (bi, 0, 0)),
                     pl.BlockSpec((None, past, IDX_DIM), lambda bi, i: (bi, 0, 0))]
        args += [past_k.reshape(b, past, kvw), past_v.reshape(b, past, kvw), past_ik]
    in_specs.append(pl.BlockSpec((N_KV_HEADS, rows, KEY_WINDOW), lambda bi, i: (0, 0, 0),
                                 pipeline_mode=pl.Buffered(1)))
    args.append(_near_bias(rel_bias, tq))

    kern = functools.partial(_attn_kernel, tq=tq, t=t, past=past, front_pad=front_pad, topk=topk)
    return pl.pallas_call(
        kern,
        out_shape=jax.ShapeDtypeStruct((b * t, qw), BF16),
        grid=(b, nq),
        in_specs=in_specs,
        out_specs=pl.BlockSpec((tq, qw), qrow),
        scratch_shapes=[pltpu.VMEM((N_KV_HEADS, lp, HEAD_DIM), BF16),
                        pltpu.VMEM((N_KV_HEADS, lp, LANE), BF16),
                        pltpu.VMEM((lp, IDX_DIM), BF16),
                        pltpu.VMEM((nw_max, KEY_WINDOW, tq) if keys_on_rows
                                   else (nw_max, tq, KEY_WINDOW), F32),
                        pltpu.VMEM((N_KV_HEADS, rows, HEAD_DIM), BF16),
                        pltpu.VMEM((IDX_HEADS // 2, 2 * tq, IDX_DIM) if keys_on_rows
                                   else (1, IDX_HEADS * tq, IDX_DIM), BF16),
                        pltpu.VMEM((8, 1) if keys_on_rows else (IDX_HEADS * tq, 1), F32),
                        pltpu.VMEM((max(rb, tq), KEY_WINDOW), F32),
                        pltpu.VMEM((N_KV_HEADS * rows // rb, rb, KEY_WINDOW), F32),
                        pltpu.VMEM((N_KV_HEADS, rows, LANE), F32),
                        pltpu.VMEM((N_KV_HEADS, rows, LANE), F32)],
        compiler_params=_params("arbitrary", "arbitrary"),
        name="dsa_attention",
    )(*args)


def _ssd_kernel(z_ref, xlo_ref, xhi_ref, dt_ref, cst_ref, st0_ref, cw_ref, cb_ref, dtb_ref, alog_ref,
                dexp_ref, nw_ref, expand_ref, y_ref, ncv_ref, stout_ref,
                xpad_ref, act_ref, yd_ref, st_ref, *, q):
    c = pl.program_id(1)
    half = CONV_DIM // 2

    @pl.when(c == 0)
    def _():
        xpad_ref[8 - (D_CONV - 1):8, :] = cst_ref[...]
        for g in range(SSM_GROUPS):
            hs = slice(g * HEADS_PER_GROUP, (g + 1) * HEADS_PER_GROUP)
            st_ref[g] = st0_ref[hs].reshape(GROUP_W, D_STATE).T

    cc = CONV_COLS
    for c0 in range(0, CONV_DIM, cc):
        cs = slice(c0, c0 + cc)
        src = xlo_ref[:, c0:c0 + cc] if c0 < half else xhi_ref[:, c0 - half:c0 - half + cc]
        xpad_ref[8:8 + q, cs] = src
        conv = cb_ref[:, cs] + xpad_ref[5:5 + q, cs] * cw_ref[0:1, cs]
        conv = conv + xpad_ref[6:6 + q, cs] * cw_ref[1:2, cs]
        conv = conv + xpad_ref[7:7 + q, cs] * cw_ref[2:3, cs]
        conv = conv + xpad_ref[8:8 + q, cs] * cw_ref[3:4, cs]
        act_ref[:, cs] = _silu(conv)
        xpad_ref[0:8, cs] = xpad_ref[q:q + 8, cs]

    hi = lax.Precision.HIGHEST
    x_dt = dt_ref[:, 0:SSM_HEADS] + dtb_ref[...]
    dt = jnp.maximum(x_dt, 0.0) + jnp.log1p(jnp.exp(-jnp.abs(x_dt)))
    a = dt * (-jnp.exp(alog_ref[...]))
    rr = lax.broadcasted_iota(jnp.int32, (q, q), 0)
    cl = lax.broadcasted_iota(jnp.int32, (q, q), 1)
    causal = rr >= cl
    tri = jnp.where(causal, 1.0, 0.0)
    a_cs = jnp.dot(tri, a, precision=hi, preferred_element_type=F32)
    eye = jnp.where(lax.broadcasted_iota(jnp.int32, (SSM_HEADS, SSM_HEADS), 0)
                    == lax.broadcasted_iota(jnp.int32, (SSM_HEADS, SSM_HEADS), 1), 1.0, 0.0)
    a_cs_t = lax.dot_general(eye, a_cs, (((1,), (1,)), ((), ())), precision=hi,
                             preferred_element_type=F32)
    a_end = a_cs[q - 1:q, :]
    per_head = jnp.concatenate([dt, jnp.exp(a_cs), jnp.exp(a_end - a_cs)], axis=0)
    top = per_head.astype(BF16)
    rest = (per_head - top.astype(F32)).astype(BF16)
    spread = jnp.dot(jnp.concatenate([top, rest], axis=0), expand_ref[...],
                     preferred_element_type=F32)
    dt_x = spread[0:q] + spread[3 * q:4 * q]
    e_x = spread[q:2 * q] + spread[4 * q:5 * q]
    dte_x = spread[2 * q:3 * q] + spread[5 * q:6 * q]
    dec_x = e_x[q - 1:q, :]

    for g in range(SSM_GROUPS):
        gs = slice(g * GROUP_W, (g + 1) * GROUP_W)
        bg = act_ref[:, D_INNER + g * D_STATE:D_INNER + (g + 1) * D_STATE].astype(BF16)
        cg = act_ref[:, D_INNER + (SSM_GROUPS + g) * D_STATE:
                     D_INNER + (SSM_GROUPS + g + 1) * D_STATE].astype(BF16)
        cb = lax.dot_general(cg, bg, (((1,), (1,)), ((), ())), preferred_element_type=F32)
        xs_g = act_ref[:, gs]
        xd_g = xs_g * dt_x[:, gs]
        st_g = st_ref[g]
        y_off = jnp.dot(cg, st_g.astype(BF16), preferred_element_type=F32) * e_x[:, gs]
        for jj in range(HEADS_PER_GROUP):
            h = g * HEADS_PER_GROUP + jj
            seg = a_cs[:, h:h + 1] - a_cs_t[h:h + 1, :]
            decay = jnp.where(causal, jnp.exp(seg), 0.0)
            hs = slice(jj * SSM_HEAD_DIM, (jj + 1) * SSM_HEAD_DIM)
            yd_ref[:, hs] = jnp.dot((cb * decay).astype(BF16), xd_g[:, hs].astype(BF16),
                                    preferred_element_type=F32)
        st_ref[g] = st_g * dec_x[:, gs] + lax.dot_general(
            bg, (xd_g * dte_x[:, gs]).astype(BF16), (((0,), (0,)), ((), ())),
            preferred_element_type=F32)
        y = yd_ref[...] + y_off + xs_g * dexp_ref[:, gs]
        zg = z_ref[:, gs]
        yg = y * _silu(zg)
        ms = jnp.mean(yg * yg, axis=-1, keepdims=True)
        y_ref[:, gs] = (yg * lax.rsqrt(ms + EPS) * nw_ref[:, gs]).astype(y_ref.dtype)

    @pl.when(c == pl.num_programs(1) - 1)
    def _():
        ncv_ref[...] = xpad_ref[8 - (D_CONV - 1):8, :]
        for g in range(SSM_GROUPS):
            hs = slice(g * HEADS_PER_GROUP, (g + 1) * HEADS_PER_GROUP)
            stout_ref[hs] = st_ref[g].T.reshape(HEADS_PER_GROUP, SSM_HEAD_DIM, D_STATE)


def _ssd_mixer(proj, conv_state, ssm_state, conv_w, conv_b, dt_bias, a_log, d_skip, norm_w, b, t):
    q = 128 if t % 128 == 0 else t
    assert q % 8 == 0 and t >= D_CONV - 1 and CONV_DIM == 2 * D_INNER
    nc = t // q
    expand = jnp.repeat(jnp.eye(SSM_HEADS, dtype=BF16), SSM_HEAD_DIM, axis=1)
    dexp = jnp.repeat(d_skip.astype(F32), SSM_HEAD_DIM)[None, :]
    row = lambda k: (lambda bi, c: (bi * nc + c, k))
    const2 = lambda bi, c: (0, 0)
    state_spec = pl.BlockSpec((None, SSM_HEADS, SSM_HEAD_DIM, D_STATE), lambda bi, c: (bi, 0, 0, 0))
    conv_spec = pl.BlockSpec((None, D_CONV - 1, CONV_DIM), lambda bi, c: (bi, 0, 0))
    y, new_conv, new_ssm = pl.pallas_call(
        functools.partial(_ssd_kernel, q=q),
        out_shape=[jax.ShapeDtypeStruct((b * t, D_INNER), BF16),
                   jax.ShapeDtypeStruct((b, D_CONV - 1, CONV_DIM), F32),
                   jax.ShapeDtypeStruct((b, SSM_HEADS, SSM_HEAD_DIM, D_STATE), F32)],
        grid=(b, nc),
        in_specs=[pl.BlockSpec((q, D_INNER), row(SSM_Z_OFF // D_INNER)),
                  pl.BlockSpec((q, D_INNER), row(SSM_XBC_OFF // D_INNER)),
                  pl.BlockSpec((q, D_INNER), row(SSM_XBC_OFF // D_INNER + 1)),
                  pl.BlockSpec((q, LANE), row(SSM_DT_OFF // LANE)),
                  conv_spec,
                  state_spec,
                  pl.BlockSpec((D_CONV, CONV_DIM), const2),
                  pl.BlockSpec((1, CONV_DIM), const2),
                  pl.BlockSpec((1, SSM_HEADS), const2),
                  pl.BlockSpec((1, SSM_HEADS), const2),
                  pl.BlockSpec((1, D_INNER), const2),
                  pl.BlockSpec((1, D_INNER), const2),
                  pl.BlockSpec((SSM_HEADS, D_INNER), const2)],
        out_specs=[pl.BlockSpec((q, D_INNER), row(0)), conv_spec, state_spec],
        scratch_shapes=[pltpu.VMEM((q + 8, CONV_DIM), F32),
                        pltpu.VMEM((q, CONV_DIM), F32),
                        pltpu.VMEM((q, GROUP_W), F32),
                        pltpu.VMEM((SSM_GROUPS, D_STATE, GROUP_W), F32)],
        compiler_params=_params("arbitrary", "arbitrary"),
        name="ssd_mixer",
    )(proj, proj, proj, proj, conv_state.astype(F32), ssm_state.astype(F32), conv_w.astype(F32),
      conv_b.astype(F32)[None, :], dt_bias.astype(F32)[None, :], a_log.astype(F32)[None, :], dexp,
      norm_w.astype(F32)[None, :], expand)
    return y, new_conv, new_ssm.astype(ssm_state.dtype)


def _row_tile(m, cap):
    tm = cap
    while m % tm:
        tm //= 2
    assert tm >= 8, (m, cap)
    return tm


def _pad_cols(w, n):
    return jnp.pad(w, ((0, 0), (0, n - w.shape[1])))


def _prep_weights(norm_mix, norm_ffn, norm_final, attn_w_in, attn_w_o, ssm_w_in, ssm_w_out,
                  mlp_w_up, mlp_w_down):
    col_scale = jnp.where(jnp.arange(ATTN_COLS) < K_OFF, ATTN_SCALE * LOG2E, 1.0).astype(F32)
    return dict(
        attn_in=_pad_cols(attn_w_in[0] * col_scale, _round_up(ATTN_COLS, LANE)).astype(BF16),
        attn_o=attn_w_o[0].astype(BF16),
        ssm_in=ssm_w_in[0].astype(BF16),
        ssm_out=ssm_w_out[0].astype(BF16),
        up=[mlp_w_up[i].astype(BF16) for i in range(2)],
        down=[mlp_w_down[i].astype(BF16) for i in range(2)],
        g_mix=[norm_mix[i].astype(F32)[None, :] for i in range(2)],
        g_ffn=[norm_ffn[i].astype(F32)[None, :] for i in range(2)],
        g_final=norm_final.astype(F32)[None, :],
    )


def _trunk(x, past_k, past_v, past_ik, conv_st, ssm_st, rel_bias, wts, ssm_conv_w, ssm_conv_b,
           ssm_dt_bias, ssm_a_log, ssm_d, ssm_norm):
    b, t, d = x.shape
    m = b * t
    x2 = x.reshape(m, d).astype(F32)
    tail_rows = _row_tile(m, TAIL_ROWS)

    projb, k_new, v_new, ki_new, kiwi = _attn_proj(x2, wts["g_mix"][0], wts["attn_in"],
                                                   tm=_row_tile(m, PROJ_ROWS))
    ao = _dsa_mixer(projb, kiwi, past_k, past_v, past_ik, rel_bias, b, t)
    x2 = _layer_tail(ao, wts["attn_o"], x2, wts["g_ffn"][0], wts["up"][0], wts["down"][0],
                     wts["g_final"], tm=tail_rows, tf=MLP_CHUNK, final_norm=False)

    proj = _norm_matmul(x2, wts["g_mix"][1], wts["ssm_in"], tm=_row_tile(m, SSM_PROJ_ROWS),
                        tn=wts["ssm_in"].shape[1])
    y, new_conv, new_ssm = _ssd_mixer(proj, conv_st, ssm_st, ssm_conv_w, ssm_conv_b, ssm_dt_bias,
                                      ssm_a_log, ssm_d, ssm_norm, b, t)
    x2 = _layer_tail(y, wts["ssm_out"], x2, wts["g_ffn"][1], wts["up"][1], wts["down"][1],
                     wts["g_final"], tm=tail_rows, tf=MLP_CHUNK, final_norm=True)

    dt = x.dtype
    return (x2.reshape(b, t, d).astype(dt),
            k_new.reshape(1, b, t, N_KV_HEADS, HEAD_DIM).astype(dt),
            v_new.reshape(1, b, t, N_KV_HEADS, HEAD_DIM).astype(dt),
            ki_new.reshape(1, b, t, IDX_DIM).astype(dt), new_conv[None].astype(dt), new_ssm[None])


def kernel(x_prompt, x_sample, cache_k, cache_v, cache_idx_k, state_conv, state_ssm, rel_bias, norm_mix, norm_ffn, norm_final, attn_w_in, attn_w_o, ssm_w_in, ssm_conv_w, ssm_conv_b, ssm_dt_bias, ssm_a_log, ssm_d, ssm_norm, ssm_w_out, mlp_w_up, mlp_w_down):
    wts = _prep_weights(norm_mix, norm_ffn, norm_final, attn_w_in, attn_w_o, ssm_w_in, ssm_w_out,
                        mlp_w_up, mlp_w_down)
    bp = x_prompt.shape[0]
    dtp = x_prompt.dtype
    empty_k = jnp.zeros((bp, 0, N_KV_HEADS, HEAD_DIM), dtp)
    empty_ik = jnp.zeros((bp, 0, IDX_DIM), dtp)
    zero_conv = jnp.zeros((bp, D_CONV - 1, CONV_DIM), dtp)
    zero_ssm = jnp.zeros((bp, SSM_HEADS, SSM_HEAD_DIM, D_STATE), dtp)
    args = (rel_bias, wts, ssm_conv_w[0], ssm_conv_b[0], ssm_dt_bias[0], ssm_a_log[0], ssm_d[0],
            ssm_norm[0])
    yp, kp, vp, ikp, cp, sp = _trunk(x_prompt, empty_k, empty_k, empty_ik, zero_conv, zero_ssm, *args)
    ys, ks, vs, iks, cs, ss = _trunk(x_sample, cache_k[0], cache_v[0], cache_idx_k[0],
                                     state_conv[0], state_ssm[0], *args)
    return (yp, ys, kp, vp, ikp, cp, sp, ks, vs, iks, cs, ss)
```
